```python
import math
import jax, jax.numpy as jnp
from jax import lax
import numpy as np

D_MODEL = 1024
BATCH = 8
SEQ = 4096
DEPTH = 2

GRID_W = 64
Q_BLOCK = 128
EPS = 1e-6
ROPE_THETA = 10000.0

A_HEADS = 8
A_Q_LORA = 256
A_KV_LORA = 128
A_NOPE = 64
A_ROPE = 32
A_V = 64
B_HEADS = 8
B_KV_HEADS = 2
B_HEAD_DIM = 64
C_HEADS = 16
C_KV_HEADS = 4
C_HEAD_DIM = 64
C_WINDOW = 128
REL_BUCKETS = 32
REL_MAX_DIST = 128
N_GROUPS = 4
EXPERTS_PER_GROUP = 4
N_EXPERTS = N_GROUPS * EXPERTS_PER_GROUP
TOP_K_IN_GROUP = 2
D_EXPERT = 512

AB_PARTS = (A_Q_LORA, A_KV_LORA, A_ROPE, B_HEADS * B_HEAD_DIM,
            B_KV_HEADS * B_HEAD_DIM, B_KV_HEADS * B_HEAD_DIM)
AB_IN = sum(AB_PARTS)
AB_SPLITS = tuple(int(v) for v in np.cumsum(AB_PARTS)[:-1])
AB_OUT = A_HEADS * A_V + B_HEADS * B_HEAD_DIM
C_PARTS = (C_HEADS * C_HEAD_DIM, C_KV_HEADS * C_HEAD_DIM, C_KV_HEADS * C_HEAD_DIM)
C_IN = sum(C_PARTS)
C_SPLITS = tuple(int(v) for v in np.cumsum(C_PARTS)[:-1])
C_OUT = C_HEADS * C_HEAD_DIM
N_EVEN = (DEPTH + 1) // 2
N_ODD = DEPTH // 2

kernel_name = "hybrid_mla_axialgqa_swa_hmoe_encoder"


def rms_norm(x, gain):
    xf = x.astype(jnp.float32)
    y = xf * lax.rsqrt(jnp.mean(xf * xf, axis=-1, keepdims=True) + EPS)
    return (y * gain.astype(jnp.float32)).astype(x.dtype)


def rope_cos_sin(pos, dim):
    inv = ROPE_THETA ** (-jnp.arange(0, dim, 2, dtype=jnp.float32) / dim)
    ang = pos.astype(jnp.float32)[:, None] * inv[None, :]
    return jnp.cos(ang), jnp.sin(ang)


def apply_rope(x, cos, sin):
    x1, x2 = jnp.split(x, 2, axis=-1)
    cos = cos.astype(x.dtype)
    sin = sin.astype(x.dtype)
    return jnp.concatenate([x1 * cos - x2 * sin, x2 * cos + x1 * sin], axis=-1)


def to_blocks(t):
    b, s = t.shape[:2]
    t = t.reshape((b, s // Q_BLOCK, Q_BLOCK) + t.shape[2:])
    return jnp.moveaxis(t, 1, 0)


def from_blocks(t):
    t = jnp.moveaxis(t, 0, 1)
    return t.reshape((t.shape[0], t.shape[1] * t.shape[2]) + t.shape[3:])


def mla_mix(q_lat, kv_lat, k_rope, q_a_gain, w_q_up, kv_a_gain, w_kv_up,
            qn_gain, kn_gain, qr_gain, kr_gain):
    b, s, _ = q_lat.shape
    cos, sin = rope_cos_sin(jnp.arange(s), A_ROPE)
    q = (rms_norm(q_lat, q_a_gain) @ w_q_up).reshape(b, s, A_HEADS, A_NOPE + A_ROPE)
    kv = (rms_norm(kv_lat, kv_a_gain) @ w_kv_up).reshape(b, s, A_HEADS, A_NOPE + A_V)
    q_nope = rms_norm(q[..., :A_NOPE], qn_gain)
    q_rot = apply_rope(rms_norm(q[..., A_NOPE:], qr_gain), cos[:, None], sin[:, None])
    k_nope = rms_norm(kv[..., :A_NOPE], kn_gain)
    v = kv[..., A_NOPE:]
    k_rot = apply_rope(rms_norm(k_rope, kr_gain), cos, sin)
    scale = (A_NOPE + A_ROPE) ** -0.5

    def block(args):
        qn, qr = args
        sc = (jnp.einsum('bqhd,bkhd->bhqk', qn, k_nope)
              + jnp.einsum('bqhr,bkr->bhqk', qr, k_rot))
        p = jax.nn.softmax(sc.astype(jnp.float32) * scale, axis=-1).astype(v.dtype)
        return jnp.einsum('bhqk,bkhd->bqhd', p, v)

    o = lax.map(block, (to_blocks(q_nope), to_blocks(q_rot)))
    return from_blocks(o).reshape(b, s, A_HEADS * A_V)


def axial_gqa_mix(q, k, v, q_gain, k_gain):
    b, s = q.shape[:2]
    rows = s // GRID_W
    row_pos = jnp.repeat(jnp.arange(rows), GRID_W)
    col_pos = jnp.tile(jnp.arange(GRID_W), rows)
    half = B_HEAD_DIM // 2
    cos_r, sin_r = rope_cos_sin(row_pos, half)
    cos_c, sin_c = rope_cos_sin(col_pos, half)

    def axial(t):
        return jnp.concatenate(
            [apply_rope(t[..., :half], cos_r[:, None], sin_r[:, None]),
             apply_rope(t[..., half:], cos_c[:, None], sin_c[:, None])], axis=-1)

    grp = B_HEADS // B_KV_HEADS
    qn = axial(rms_norm(q, q_gain)).reshape(b, s, B_KV_HEADS, grp, B_HEAD_DIM)
    kn = axial(rms_norm(k, k_gain))
    scale = B_HEAD_DIM ** -0.5

    def block(qb):
        sc = jnp.einsum('bqhgd,bkhd->bhgqk', qb, kn)
        p = jax.nn.softmax(sc.astype(jnp.float32) * scale, axis=-1).astype(v.dtype)
        return jnp.einsum('bhgqk,bkhd->bqhgd', p, v)

    o = lax.map(block, to_blocks(qn))
    return from_blocks(o).reshape(b, s, B_HEADS * B_HEAD_DIM)


def t5_bucket(rel):
    nb = REL_BUCKETS // 2
    max_exact = nb // 2
    ret = jnp.where(rel > 0, nb, 0)
    n = jnp.abs(rel)
    nf = jnp.maximum(n, 1).astype(jnp.float32)
    large = max_exact + (jnp.log(nf / max_exact) / math.log(REL_MAX_DIST / max_exact)
                         * (nb - max_exact)).astype(jnp.int32)
    large = jnp.minimum(large, nb - 1)
    return ret + jnp.where(n < max_exact, n, large)


def window_gqa_mix(q, k, v, q_gain, k_gain, sink, rel_table):
    b, s = q.shape[:2]
    grp = C_HEADS // C_KV_HEADS
    span = Q_BLOCK + 2 * C_WINDOW
    qn = rms_norm(q, q_gain).reshape(b, s, C_KV_HEADS, grp, C_HEAD_DIM)
    kn = rms_norm(k, k_gain)
    pad = ((0, 0), (C_WINDOW, C_WINDOW), (0, 0), (0, 0))
    kp = jnp.pad(kn, pad)
    vp = jnp.pad(v, pad)
    qi = jnp.arange(Q_BLOCK)[:, None]
    kj = jnp.arange(span)[None, :]
    rel = kj - C_WINDOW - qi
    in_band = jnp.abs(rel) <= C_WINDOW
    bias = rel_table[t5_bucket(rel)]
    bias = jnp.transpose(bias, (2, 0, 1)).reshape(C_KV_HEADS, grp, Q_BLOCK, span).astype(jnp.float32)
    sink_f = sink.astype(jnp.float32).reshape(C_KV_HEADS, grp, 1, 1)
    scale = C_HEAD_DIM ** -0.5

    def block(args):
        i, qb = args
        start = i * Q_BLOCK
        kb = lax.dynamic_slice_in_dim(kp, start, span, axis=1)
        vb = lax.dynamic_slice_in_dim(vp, start, span, axis=1)
        key_pos = start - C_WINDOW + kj
        mask = in_band & (key_pos >= 0) & (key_pos < s)
        sc = jnp.einsum('bqhgd,bkhd->bhgqk', qb, kb).astype(jnp.float32) * scale + bias
        sc = jnp.where(mask, sc, -jnp.inf)
        sink_col = jnp.broadcast_to(sink_f, sc.shape[:-1] + (1,))
        p = jax.nn.softmax(jnp.concatenate([sc, sink_col], axis=-1), axis=-1)[..., :span]
        return jnp.einsum('bhgqk,bkhd->bqhgd', p.astype(vb.dtype), vb)

    o = lax.map(block, (jnp.arange(s // Q_BLOCK), to_blocks(qn)))
    return from_blocks(o).reshape(b, s, C_HEADS * C_HEAD_DIM)


def hier_moe(h, w_group, b_group, w_router, b_router, w_gate, w_up, w_down):
    b, s, d = h.shape
    xt = h.reshape(b * s, d)
    g_prob = jax.nn.softmax((xt @ w_group).astype(jnp.float32) + b_group.astype(jnp.float32), axis=-1)
    g_p, g_idx = lax.top_k(g_prob, 1)
    e_logits = ((xt @ w_router).astype(jnp.float32) + b_router.astype(jnp.float32)
                ).reshape(-1, N_GROUPS, EXPERTS_PER_GROUP)
    e_logits = jnp.take_along_axis(e_logits, g_idx[:, :, None], axis=1)[:, 0]
    e_prob = jax.nn.softmax(e_logits, axis=-1)
    e_p, e_loc = lax.top_k(e_prob, TOP_K_IN_GROUP)
    wts = g_p * e_p / jnp.sum(e_p, axis=-1, keepdims=True)
    e_idx = g_idx * EXPERTS_PER_GROUP + e_loc
    gates = jnp.sum(jax.nn.one_hot(e_idx, N_EXPERTS, dtype=jnp.float32) * wts[..., None], axis=1)
    gates = gates.astype(xt.dtype)
    out = jnp.zeros_like(xt)
    for e in range(N_EXPERTS):
        hid = jax.nn.silu(xt @ w_gate[e]) * (xt @ w_up[e])
        out = out + gates[:, e:e + 1] * (hid @ w_down[e])
    return out.reshape(b, s, d)


def setup_inputs(seed: int = 0) -> dict:
    key = jax.random.key(seed)
    ks = iter(jax.random.split(key, 40))

    def dense(shape, fan_in):
        return jax.random.normal(next(ks), shape, jnp.float32) * (fan_in ** -0.5)

    def gain(shape):
        return 1.0 + 0.02 * jax.random.normal(next(ks), shape, jnp.float32)

    def small(shape, scale):
        return scale * jax.random.normal(next(ks), shape, jnp.float32)

    return {
        "x": jax.random.normal(next(ks), (BATCH, SEQ, D_MODEL), jnp.float32),
        "mix_norm": gain((DEPTH, D_MODEL)),
        "ffn_norm": gain((DEPTH, D_MODEL)),
        "w_in_ab": dense((N_EVEN, D_MODEL, AB_IN), D_MODEL),
        "mla_q_a_norm": gain((N_EVEN, A_Q_LORA)),
        "mla_w_q_up": dense((N_EVEN, A_Q_LORA, A_HEADS * (A_NOPE + A_ROPE)), A_Q_LORA),
        "mla_kv_a_norm": gain((N_EVEN, A_KV_LORA)),
        "mla_w_kv_up": dense((N_EVEN, A_KV_LORA, A_HEADS * (A_NOPE + A_V)), A_KV_LORA),
        "mla_qn_gain": gain((N_EVEN, A_NOPE)),
        "mla_kn_gain": gain((N_EVEN, A_NOPE)),
        "mla_qr_gain": gain((N_EVEN, A_ROPE)),
        "mla_kr_gain": gain((N_EVEN, A_ROPE)),
        "gqa_q_gain": gain((N_EVEN, B_HEAD_DIM)),
        "gqa_k_gain": gain((N_EVEN, B_HEAD_DIM)),
        "w_out_ab": dense((N_EVEN, AB_OUT, D_MODEL), AB_OUT),
        "w_in_c": dense((N_ODD, D_MODEL, C_IN), D_MODEL),
        "win_q_gain": gain((N_ODD, C_HEAD_DIM)),
        "win_k_gain": gain((N_ODD, C_HEAD_DIM)),
        "win_sink": small((N_ODD, C_HEADS), 0.5),
        "w_out_c": dense((N_ODD, C_OUT, D_MODEL), C_OUT),
        "rel_bias": small((REL_BUCKETS, C_HEADS), 0.5),
        "moe_w_group": dense((DEPTH, D_MODEL, N_GROUPS), D_MODEL),
        "moe_b_group": small((DEPTH, N_GROUPS), 0.01),
        "moe_w_router": dense((DEPTH, D_MODEL, N_EXPERTS), D_MODEL),
        "moe_b_router": small((DEPTH, N_EXPERTS), 0.01),
        "moe_w_gate": dense((DEPTH, N_EXPERTS, D_MODEL, D_EXPERT), D_MODEL),
        "moe_w_up": dense((DEPTH, N_EXPERTS, D_MODEL, D_EXPERT), D_MODEL),
        "moe_w_down": dense((DEPTH, N_EXPERTS, D_EXPERT, D_MODEL), D_EXPERT),
    }


def reference(x, mix_norm, ffn_norm, w_in_ab, mla_q_a_norm, mla_w_q_up, mla_kv_a_norm,
              mla_w_kv_up, mla_qn_gain, mla_kn_gain, mla_qr_gain, mla_kr_gain,
              gqa_q_gain, gqa_k_gain, w_out_ab, w_in_c, win_q_gain, win_k_gain,
              win_sink, w_out_c, rel_bias, moe_w_group, moe_b_group, moe_w_router,
              moe_b_router, moe_w_gate, moe_w_up, moe_w_down):
    b, s, _ = x.shape
    for layer in range(DEPTH):
        i = layer // 2
        h = rms_norm(x, mix_norm[layer])
        if layer % 2 == 0:
            proj = h @ w_in_ab[i]
            q_lat, kv_lat, k_rope, bq, bk, bv = jnp.split(proj, AB_SPLITS, axis=-1)
            out_a = mla_mix(q_lat, kv_lat, k_rope, mla_q_a_norm[i], mla_w_q_up[i],
                            mla_kv_a_norm[i], mla_w_kv_up[i], mla_qn_gain[i],
                            mla_kn_gain[i], mla_qr_gain[i], mla_kr_gain[i])
            out_b = axial_gqa_mix(bq.reshape(b, s, B_HEADS, B_HEAD_DIM),
                                  bk.reshape(b, s, B_KV_HEADS, B_HEAD_DIM),
                                  bv.reshape(b, s, B_KV_HEADS, B_HEAD_DIM),
                                  gqa_q_gain[i], gqa_k_gain[i])
            x = x + jnp.concatenate([out_a, out_b], axis=-1) @ w_out_ab[i]
        else:
            proj = h @ w_in_c[i]
            cq, ck, cv = jnp.split(proj, C_SPLITS, axis=-1)
            out_c = window_gqa_mix(cq.reshape(b, s, C_HEADS, C_HEAD_DIM),
                                   ck.reshape(b, s, C_KV_HEADS, C_HEAD_DIM),
                                   cv.reshape(b, s, C_KV_HEADS, C_HEAD_DIM),
                                   win_q_gain[i], win_k_gain[i], win_sink[i], rel_bias)
            x = x + out_c @ w_out_c[i]
        x = x + hier_moe(rms_norm(x, ffn_norm[layer]), moe_w_group[layer], moe_b_group[layer],
                         moe_w_router[layer], moe_b_router[layer], moe_w_gate[layer],
                         moe_w_up[layer], moe_w_down[layer])
    return x
```

```python
import functools
import math

import numpy as np
import jax
import jax.numpy as jnp
from jax import lax
from jax.experimental import pallas as pl
from jax.experimental.pallas import tpu as pltpu

F32 = jnp.float32
BF16 = jnp.bfloat16

EPS = 1e-6
ROPE_THETA = 10000.0
LANES = 128
HEAD = 64
ROPE_HALF = 16
GRID_W = 64
WINDOW = 128
Q_BLOCK = 128
REL_BUCKETS = 32
REL_MAX_DIST = 128
N_GROUPS = 4
EXPERTS_PER_GROUP = 4
N_EXPERTS = 16
ROUTER_ROWS = 32
VMEM_LIMIT = 56 * 1024 * 1024

_NT = (((1,), (1,)), ((), ()))


def _dot(a, b):
    return jnp.dot(a, b, preferred_element_type=F32)


def _dot_nt(a, b):
    return lax.dot_general(a, b, _NT, preferred_element_type=F32)


def _split_bf16(a):
    hi = a.astype(BF16)
    lo = (a - hi.astype(F32)).astype(BF16)
    return hi, lo


def _row_rmsnorm(t, gain):
    return t * lax.rsqrt(jnp.mean(t * t, axis=-1, keepdims=True) + EPS) * gain


def _seg_rmsnorm(t, mseg, msegt, invlen, gain):
    hi, lo = _split_bf16(t * t)
    sums = _dot(hi, mseg) + _dot(lo, mseg)
    inv = lax.rsqrt(sums * invlen + EPS)
    ihi, ilo = _split_bf16(inv)
    scale = _dot(ihi, msegt) + _dot(ilo, msegt)
    return t * scale * gain


def _rope128(t, cos, sin_signed, first_half):
    up = pltpu.roll(t, LANES - ROPE_HALF, 1)
    dn = pltpu.roll(t, ROPE_HALF, 1)
    return t * cos + jnp.where(first_half, up, dn) * sin_signed


def _dup_halves(blk, lo):
    sw = pltpu.roll(blk, HEAD, 1)
    return jnp.where(lo, blk, sw), jnp.where(lo, sw, blk)


def _lane_masks():
    lane = lax.broadcasted_iota(jnp.int32, (1, LANES), 1)
    return lane < HEAD, (lane % (2 * ROPE_HALF)) < ROPE_HALF


def _prep_ab_kernel(x_ref, gmix_ref, w0_ref, qag_ref, wq_ref, kvag_ref, wkv_ref,
                    mq_ref, mqt_ref, ilq_ref, gq_ref,
                    mk_ref, mkt_ref, ilk_ref, gk_ref,
                    mr_ref, mrt_ref, ilr_ref, gr_ref,
                    mbq_ref, mbqt_ref, ilbq_ref, gbq_ref,
                    mbk_ref, mbkt_ref, ilbk_ref, gbk_ref,
                    cosa_ref, sina_ref, cosb_ref, sinb_ref,
                    qa_ref, ka_ref, va_ref, qb_ref, kb_ref, vb_ref):
    lo, first_half = _lane_masks()
    h = _row_rmsnorm(x_ref[...], gmix_ref[...]).astype(BF16)
    proj = _dot(h, w0_ref[...])
    cosa, sina = cosa_ref[...], sina_ref[...]
    cosb, sinb = cosb_ref[...], sinb_ref[...]

    qn = _row_rmsnorm(proj[:, 0:256], qag_ref[...]).astype(BF16)
    q = _seg_rmsnorm(_dot(qn, wq_ref[...]), mq_ref[...], mqt_ref[...], ilq_ref[...], gq_ref[...])
    for hd in range(8):
        sl = slice(LANES * hd, LANES * (hd + 1))
        qa_ref[:, sl] = _rope128(q[:, sl], cosa, sina, first_half).astype(BF16)

    kvn = _row_rmsnorm(proj[:, 256:384], kvag_ref[...]).astype(BF16)
    kv = _dot(kvn, wkv_ref[...])
    kn = _seg_rmsnorm(kv[:, 0:1024], mk_ref[...], mkt_ref[...], ilk_ref[...], gk_ref[...])
    kr = _seg_rmsnorm(proj[:, 384:512], mr_ref[...], mrt_ref[...], ilr_ref[...], gr_ref[...])
    kr = _rope128(kr, cosa, sina, first_half)
    for hd in range(8):
        sl = slice(LANES * hd, LANES * (hd + 1))
        ka_ref[:, sl] = (kn[:, sl] + kr).astype(BF16)
    va_ref[...] = kv[:, 1024:1536].astype(BF16)

    bq = _seg_rmsnorm(proj[:, 512:1024], mbq_ref[...], mbqt_ref[...], ilbq_ref[...], gbq_ref[...])
    for blk in range(4):
        sl = slice(LANES * blk, LANES * (blk + 1))
        qb_ref[:, sl] = _rope128(bq[:, sl], cosb, sinb, first_half).astype(BF16)
    bk = _seg_rmsnorm(proj[:, 1024:1152], mbk_ref[...], mbkt_ref[...], ilbk_ref[...], gbk_ref[...])
    bk = _rope128(bk, cosb, sinb, first_half)
    k0, k1 = _dup_halves(bk, lo)
    kb_ref[:, 0:LANES] = k0.astype(BF16)
    kb_ref[:, LANES:2 * LANES] = k1.astype(BF16)
    v0, v1 = _dup_halves(proj[:, 1152:1280], lo)
    vb_ref[:, 0:LANES] = v0.astype(BF16)
    vb_ref[:, LANES:2 * LANES] = v1.astype(BF16)


def _prep_c_kernel(x_ref, gmix_ref, w_ref,
                   mq_ref, mqt_ref, ilq_ref, gq_ref,
                   mk_ref, mkt_ref, ilk_ref, gk_ref,
                   qc_ref, kc_ref, vc_ref):
    lo, _ = _lane_masks()
    h = _row_rmsnorm(x_ref[...], gmix_ref[...]).astype(BF16)
    proj = _dot(h, w_ref[...])
    q = _seg_rmsnorm(proj[:, 0:1024], mq_ref[...], mqt_ref[...], ilq_ref[...], gq_ref[...])
    qc_ref[...] = q.astype(BF16)
    k = _seg_rmsnorm(proj[:, 1024:1280], mk_ref[...], mkt_ref[...], ilk_ref[...], gk_ref[...])
    for blk in range(2):
        sl = slice(LANES * blk, LANES * (blk + 1))
        k0, k1 = _dup_halves(k[:, sl], lo)
        kc_ref[:, 2 * blk * LANES:(2 * blk + 1) * LANES] = k0.astype(BF16)
        kc_ref[:, (2 * blk + 1) * LANES:(2 * blk + 2) * LANES] = k1.astype(BF16)
        v0, v1 = _dup_halves(proj[:, 1280 + LANES * blk:1280 + LANES * (blk + 1)], lo)
        vc_ref[:, 2 * blk * LANES:(2 * blk + 1) * LANES] = v0.astype(BF16)
        vc_ref[:, (2 * blk + 1) * LANES:(2 * blk + 2) * LANES] = v1.astype(BF16)


def _softmax_pv(s, v):
    m = jnp.max(s, axis=-1, keepdims=True)
    p = jnp.exp(s - m)
    l = jnp.sum(p, axis=-1, keepdims=True)
    return _dot(p.astype(BF16), v) / l


def _mla_attn_kernel(q_ref, k_ref, v_ref, o_ref):
    lo, _ = _lane_masks()
    v = v_ref[0]
    outs = []
    for j in range(2):
        sl = slice(LANES * j, LANES * (j + 1))
        s = _dot_nt(q_ref[0, :, sl], k_ref[0, :, sl])
        outs.append(_softmax_pv(s, v))
    o_ref[0] = jnp.where(lo, outs[0], outs[1]).astype(o_ref.dtype)


def _stack_heads(q_ref, first_block, lo):
    parts = []
    for a in range(4):
        blk = q_ref[0, :, LANES * (first_block + a // 2):LANES * (first_block + a // 2 + 1)]
        keep = lo if a % 2 == 0 else jnp.logical_not(lo)
        parts.append(jnp.where(keep, blk, jnp.zeros_like(blk)))
    return jnp.concatenate(parts, axis=0)


def _gqa_attn_kernel(q_ref, k_ref, v_ref, o_ref):
    lo, _ = _lane_masks()
    tq = q_ref.shape[1]
    qs = _stack_heads(q_ref, 0, lo)
    pv = _softmax_pv(_dot_nt(qs, k_ref[0]), v_ref[0])
    for j in range(2):
        o_ref[0, :, LANES * j:LANES * (j + 1)] = jnp.where(
            lo, pv[2 * j * tq:(2 * j + 1) * tq], pv[(2 * j + 1) * tq:(2 * j + 2) * tq]).astype(o_ref.dtype)


def _win_attn_kernel(sink_ref, q_ref, kp_ref, kc_ref, kn_ref, vp_ref, vc_ref, vn_ref, bias_ref, o_ref,
                     *, seq_len):
    lo, _ = _lane_masks()
    tq = q_ref.shape[1]
    i = pl.program_id(1)
    kj = lax.broadcasted_iota(jnp.int32, (1, 3 * tq), 1)
    key_pos = (i - 1) * tq + kj
    valid = jnp.logical_and(key_pos >= 0, key_pos < seq_len)
    for g in range(4):
        sl = slice(LANES * g, LANES * (g + 1))
        k = jnp.concatenate([kp_ref[0, :, sl], kc_ref[0, :, sl], kn_ref[0, :, sl]], axis=0)
        v = jnp.concatenate([vp_ref[0, :, sl], vc_ref[0, :, sl], vn_ref[0, :, sl]], axis=0)
        qs = _stack_heads(q_ref, 2 * g, lo)
        s = _dot_nt(qs, k) + bias_ref[4 * g:4 * g + 4].reshape(4 * tq, 3 * tq)
        s = jnp.where(valid, s, -jnp.inf)
        sink = jnp.concatenate([jnp.full((tq, 1), sink_ref[4 * g + a], F32) for a in range(4)], axis=0)
        m = jnp.maximum(jnp.max(s, axis=-1, keepdims=True), sink)
        p = jnp.exp(s - m)
        l = jnp.sum(p, axis=-1, keepdims=True) + jnp.exp(sink - m)
        pv = _dot(p.astype(BF16), v) / l
        for j in range(2):
            blk = 2 * g + j
            o_ref[0, :, LANES * blk:LANES * (blk + 1)] = jnp.where(
                lo, pv[2 * j * tq:(2 * j + 1) * tq], pv[(2 * j + 1) * tq:(2 * j + 2) * tq]).astype(o_ref.dtype)


def _rel_bias_kernel(bucket_ref, band_ref, relt_ref, o_ref):
    bucket = bucket_ref[...]
    acc = jnp.zeros(o_ref.shape, F32)
    for r in range(REL_BUCKETS):
        acc = acc + jnp.where(bucket == r, relt_ref[:, r:r + 1], 0.0)
    o_ref[...] = jnp.where(band_ref[...] > 0, acc, -jnp.inf)


def _post_kernel(*refs, n_parts):
    parts = refs[:n_parts]
    ws = refs[n_parts:2 * n_parts]
    x_ref, gffn_ref, wrh_ref, wrl_ref, rb_ref, x1_ref, xn_ref, gt_ref = refs[2 * n_parts:]
    acc = x_ref[...]
    for p_ref, w_ref in zip(parts, ws):
        acc = acc + _dot(p_ref[...], w_ref[...])
    x1_ref[...] = acc
    xn = _row_rmsnorm(acc, gffn_ref[...])
    xn_ref[...] = xn.astype(BF16)

    xh, xl = _split_bf16(xn)
    wh, wl = wrh_ref[...], wrl_ref[...]
    logit = _dot_nt(wh, xh) + _dot_nt(wh, xl) + _dot_nt(wl, xh) + rb_ref[...]
    g = [logit[r:r + 1, :] for r in range(N_GROUPS)]
    gmax = jnp.maximum(jnp.maximum(g[0], g[1]), jnp.maximum(g[2], g[3]))
    gsum = sum(jnp.exp(gi - gmax) for gi in g)
    g_p = 1.0 / gsum
    gidx = jnp.where(g[0] == gmax, 0, jnp.where(g[1] == gmax, 1, jnp.where(g[2] == gmax, 2, 3)))
    e = []
    for j in range(EXPERTS_PER_GROUP):
        rows = [logit[N_GROUPS + EXPERTS_PER_GROUP * gg + j:N_GROUPS + EXPERTS_PER_GROUP * gg + j + 1, :]
                for gg in range(N_GROUPS)]
        e.append(jnp.where(gidx == 0, rows[0], jnp.where(gidx == 1, rows[1],
                                                         jnp.where(gidx == 2, rows[2], rows[3]))))
    emax = jnp.maximum(jnp.maximum(e[0], e[1]), jnp.maximum(e[2], e[3]))
    ex = [jnp.exp(ej - emax) for ej in e]
    esum = ex[0] + ex[1] + ex[2] + ex[3]
    pr = [exj / esum for exj in ex]
    p1 = jnp.maximum(jnp.maximum(pr[0], pr[1]), jnp.maximum(pr[2], pr[3]))
    i1 = jnp.where(pr[0] == p1, 0, jnp.where(pr[1] == p1, 1, jnp.where(pr[2] == p1, 2, 3)))
    rest = [jnp.where(i1 == j, -1.0, pr[j]) for j in range(EXPERTS_PER_GROUP)]
    p2 = jnp.maximum(jnp.maximum(rest[0], rest[1]), jnp.maximum(rest[2], rest[3]))
    i2 = jnp.where(rest[0] == p2, 0, jnp.where(rest[1] == p2, 1, jnp.where(rest[2] == p2, 2, 3)))
    den = p1 + p2
    w1 = g_p * p1 / den
    w2 = g_p * p2 / den
    rows = []
    for gg in range(N_GROUPS):
        for j in range(EXPERTS_PER_GROUP):
            wj = jnp.where(i1 == j, w1, jnp.where(i2 == j, w2, 0.0))
            rows.append(jnp.where(gidx == gg, wj, 0.0))
    gt_ref[...] = jnp.concatenate(rows, axis=0)


def _moe_dense_kernel(xn_ref, gates_ref, x1_ref, wg_ref, wu_ref, wd_ref, o_ref):
    e = pl.program_id(1)

    @pl.when(e == 0)
    def _():
        o_ref[...] = x1_ref[...]

    xn = xn_ref[...]
    hid = jax.nn.silu(_dot(xn, wg_ref[0])) * _dot(xn, wu_ref[0])
    y = _dot(hid.astype(BF16), wd_ref[0])
    lane = lax.broadcasted_iota(jnp.int32, (1, N_EXPERTS), 1)
    gate = jnp.sum(jnp.where(lane == e, gates_ref[...], 0.0), axis=-1, keepdims=True)
    o_ref[...] += gate * y


def _seg_mats(width, segments):
    m = np.zeros((width, LANES), np.float32)
    invlen = np.ones((1, LANES), np.float32)
    for c, (start, length) in enumerate(segments):
        m[start:start + length, c] = 1.0
        invlen[0, c] = 1.0 / length
    return jnp.asarray(m, BF16), jnp.asarray(m.T, BF16), jnp.asarray(invlen)


def _rope_tables(pos_list, seq_len):
    inv = ROPE_THETA ** (-jnp.arange(0, 2 * ROPE_HALF, 2, dtype=F32) / (2 * ROPE_HALF))
    cos_cols, sin_cols = [], []
    for pos in pos_list:
        if pos is None:
            cos_cols.append(jnp.ones((seq_len, 2 * ROPE_HALF), F32))
            sin_cols.append(jnp.zeros((seq_len, 2 * ROPE_HALF), F32))
        else:
            ang = pos.astype(F32)[:, None] * inv[None, :]
            c, s = jnp.cos(ang), jnp.sin(ang)
            cos_cols.append(jnp.concatenate([c, c], axis=1))
            sin_cols.append(jnp.concatenate([-s, s], axis=1))
    return jnp.concatenate(cos_cols, axis=1), jnp.concatenate(sin_cols, axis=1)


def _t5_bucket_np(rel):
    nb = REL_BUCKETS // 2
    max_exact = nb // 2
    ret = np.where(rel > 0, nb, 0)
    n = np.abs(rel)
    nf = np.maximum(n, 1).astype(np.float32)
    large = max_exact + (np.log(nf / np.float32(max_exact)) / np.float32(math.log(REL_MAX_DIST / max_exact))
                         * np.float32(nb - max_exact)).astype(np.int32)
    large = np.minimum(large, nb - 1)
    return ret + np.where(n < max_exact, n, large)


def _full(shape):
    nd = len(shape)
    return pl.BlockSpec(shape, lambda *_: (0,) * nd)


def _params(*sem):
    return pltpu.CompilerParams(dimension_semantics=sem, vmem_limit_bytes=VMEM_LIMIT)


def _prep_ab(x2, seq_len, gmix, w_in, qag, wq, kvag, wkv, qn_g, kn_g, qr_g, kr_g, bq_g, bk_g, tm=256):
    n = x2.shape[0]
    scale_a = (HEAD + 2 * ROPE_HALF) ** -0.5
    scale_b = HEAD ** -0.5
    zeros = lambda r, c: jnp.zeros((r, c), F32)
    d = w_in.shape[0]
    w0 = jnp.concatenate([w_in[:, 0:384], zeros(d, HEAD), w_in[:, 384:416], zeros(d, 32), w_in[:, 416:1184]],
                         axis=1).astype(BF16)
    wq_p = jnp.concatenate([wq.reshape(-1, 8, 96), jnp.zeros((wq.shape[0], 8, 32), F32)], axis=2)
    wq_p = wq_p.reshape(-1, 1024).astype(BF16)
    wkv_r = wkv.reshape(-1, 8, 128)
    wk_p = jnp.concatenate([wkv_r[:, :, :HEAD], jnp.zeros_like(wkv_r[:, :, :HEAD])], axis=2).reshape(-1, 1024)
    wkv_p = jnp.concatenate([wk_p, wkv_r[:, :, HEAD:].reshape(-1, 512)], axis=1).astype(BF16)

    seg_q = [(LANES * h, HEAD) for h in range(8)] + [(LANES * h + HEAD, 32) for h in range(8)]
    gq = jnp.tile(jnp.concatenate([qn_g, qr_g, jnp.zeros((32,), F32)]), 8)[None, :] * scale_a
    seg_k = [(LANES * h, HEAD) for h in range(8)]
    gk = jnp.tile(jnp.concatenate([kn_g, jnp.zeros((HEAD,), F32)]), 8)[None, :]
    seg_r = [(HEAD, 32)]
    gr = jnp.concatenate([jnp.zeros((HEAD,), F32), kr_g, jnp.zeros((32,), F32)])[None, :]
    seg_bq = [(HEAD * h, HEAD) for h in range(8)]
    gbq = jnp.tile(bq_g, 8)[None, :] * scale_b
    seg_bk = [(HEAD * h, HEAD) for h in range(2)]
    gbk = jnp.tile(bk_g, 2)[None, :]

    pos = jnp.arange(seq_len)
    cosa, sina = _rope_tables([None, None, pos, None], seq_len)
    row, col = pos // GRID_W, pos % GRID_W
    cosb, sinb = _rope_tables([row, col, row, col], seq_len)

    consts = [gmix[None, :], w0, qag[None, :], wq_p, kvag[None, :], wkv_p,
              *_seg_mats(1024, seg_q), gq, *_seg_mats(1024, seg_k), gk, *_seg_mats(LANES, seg_r), gr,
              *_seg_mats(512, seg_bq), gbq, *_seg_mats(LANES, seg_bk), gbk]
    nsb = seq_len // tm
    rope_spec = pl.BlockSpec((tm, LANES), lambda i: (i % nsb, 0))
    row_spec = lambda w: pl.BlockSpec((tm, w), lambda i: (i, 0))
    out_widths = (1024, 1024, 512, 512, 256, 256)
    return pl.pallas_call(
        _prep_ab_kernel,
        grid=(n // tm,),
        in_specs=[row_spec(x2.shape[1])] + [_full(c.shape) for c in consts] + [rope_spec] * 4,
        out_specs=[row_spec(w) for w in out_widths],
        out_shape=[jax.ShapeDtypeStruct((n, w), BF16) for w in out_widths],
        compiler_params=_params("parallel"),
        name="prep_ab",
    )(x2, *consts, cosa, sina, cosb, sinb)


def _prep_c(x2, gmix, w_in, q_g, k_g, tm=256):
    n = x2.shape[0]
    seg_q = [(HEAD * h, HEAD) for h in range(16)]
    gq = jnp.tile(q_g, 16)[None, :] * (HEAD ** -0.5)
    seg_k = [(HEAD * h, HEAD) for h in range(4)]
    gk = jnp.tile(k_g, 4)[None, :]
    consts = [gmix[None, :], w_in.astype(BF16), *_seg_mats(1024, seg_q), gq, *_seg_mats(256, seg_k), gk]
    row_spec = lambda w: pl.BlockSpec((tm, w), lambda i: (i, 0))
    out_widths = (1024, 512, 512)
    return pl.pallas_call(
        _prep_c_kernel,
        grid=(n // tm,),
        in_specs=[row_spec(x2.shape[1])] + [_full(c.shape) for c in consts],
        out_specs=[row_spec(w) for w in out_widths],
        out_shape=[jax.ShapeDtypeStruct((n, w), BF16) for w in out_widths],
        compiler_params=_params("parallel"),
        name="prep_c",
    )(x2, *consts)


def _mla_attn(qa, ka, va, tq=256):
    b, s, _ = qa.shape
    return pl.pallas_call(
        _mla_attn_kernel,
        grid=(b, 4, s // tq),
        in_specs=[pl.BlockSpec((1, tq, 2 * LANES), lambda bi, hp, qi: (bi, qi, hp)),
                  pl.BlockSpec((1, s, 2 * LANES), lambda bi, hp, qi: (bi, 0, hp)),
                  pl.BlockSpec((1, s, LANES), lambda bi, hp, qi: (bi, 0, hp))],
        out_specs=pl.BlockSpec((1, tq, LANES), lambda bi, hp, qi: (bi, qi, hp)),
        out_shape=jax.ShapeDtypeStruct((b, s, 512), BF16),
        compiler_params=_params("parallel", "parallel", "parallel"),
        name="mla_attn",
    )(qa, ka, va)


def _gqa_attn(qb, kb, vb, tq=128):
    b, s, _ = qb.shape
    return pl.pallas_call(
        _gqa_attn_kernel,
        grid=(b, 2, s // tq),
        in_specs=[pl.BlockSpec((1, tq, 2 * LANES), lambda bi, g, qi: (bi, qi, g)),
                  pl.BlockSpec((1, s, LANES), lambda bi, g, qi: (bi, 0, g)),
                  pl.BlockSpec((1, s, LANES), lambda bi, g, qi: (bi, 0, g))],
        out_specs=pl.BlockSpec((1, tq, 2 * LANES), lambda bi, g, qi: (bi, qi, g)),
        out_shape=jax.ShapeDtypeStruct((b, s, 512), BF16),
        compiler_params=_params("parallel", "parallel", "parallel"),
        name="gqa_attn",
    )(qb, kb, vb)


def _rel_bias_table(rel_bias):
    span = Q_BLOCK + 2 * WINDOW
    rel = np.arange(span)[None, :] - WINDOW - np.arange(Q_BLOCK)[:, None]
    bucket = _t5_bucket_np(rel).astype(np.int32).reshape(1, -1)
    band = (np.abs(rel) <= WINDOW).astype(np.int32).reshape(1, -1)
    heads = rel_bias.shape[1]
    cols = bucket.shape[1]
    chunk = cols // 8
    table = pl.pallas_call(
        _rel_bias_kernel,
        grid=(8,),
        in_specs=[pl.BlockSpec((1, chunk), lambda i: (0, i)), pl.BlockSpec((1, chunk), lambda i: (0, i)),
                  _full((heads, REL_BUCKETS))],
        out_specs=pl.BlockSpec((heads, chunk), lambda i: (0, i)),
        out_shape=jax.ShapeDtypeStruct((heads, cols), F32),
        compiler_params=_params("parallel"),
        name="rel_bias",
    )(jnp.asarray(bucket), jnp.asarray(band), rel_bias.T)
    return table.reshape(heads, Q_BLOCK, span)


def _win_attn(qc, kc, vc, sink, bias):
    b, s, _ = qc.shape
    tq = Q_BLOCK
    nb = s // tq
    prev = lambda bi, i: (bi, jnp.maximum(i - 1, 0), 0)
    cur = lambda bi, i: (bi, i, 0)
    nxt = lambda bi, i: (bi, jnp.minimum(i + 1, nb - 1), 0)
    kv_spec = lambda im: pl.BlockSpec((1, tq, 4 * LANES), im)
    return pl.pallas_call(
        functools.partial(_win_attn_kernel, seq_len=s),
        grid=(b, nb),
        in_specs=[pl.BlockSpec(memory_space=pltpu.SMEM),
                  pl.BlockSpec((1, tq, 8 * LANES), cur),
                  kv_spec(prev), kv_spec(cur), kv_spec(nxt),
                  kv_spec(prev), kv_spec(cur), kv_spec(nxt),
                  _full(bias.shape)],
        out_specs=pl.BlockSpec((1, tq, 8 * LANES), cur),
        out_shape=jax.ShapeDtypeStruct((b, s, 1024), BF16),
        compiler_params=_params("parallel", "parallel"),
        name="win_attn",
    )(sink, qc, kc, kc, kc, vc, vc, vc, bias)


def _post(parts, ws, x2, gffn, w_group, b_group, w_router, b_router, tm=256):
    n, d = x2.shape
    wr = jnp.concatenate([w_group.T, w_router.T, jnp.zeros((ROUTER_ROWS - N_GROUPS - N_EXPERTS, d), F32)], axis=0)
    wrh = wr.astype(BF16)
    wrl = (wr - wrh.astype(F32)).astype(BF16)
    rb = jnp.concatenate([b_group, b_router, jnp.zeros((ROUTER_ROWS - N_GROUPS - N_EXPERTS,), F32)])[:, None]
    row_spec = lambda w: pl.BlockSpec((tm, w), lambda i: (i, 0))
    ws = [w.astype(BF16) for w in ws]
    return pl.pallas_call(
        functools.partial(_post_kernel, n_parts=len(parts)),
        grid=(n // tm,),
        in_specs=[row_spec(p.shape[1]) for p in parts] + [_full(w.shape) for w in ws]
                 + [row_spec(d), _full((1, d)), _full(wrh.shape), _full(wrl.shape), _full(rb.shape)],
        out_specs=[row_spec(d), row_spec(d), pl.BlockSpec((N_EXPERTS, tm), lambda i: (0, i))],
        out_shape=[jax.ShapeDtypeStruct((n, d), F32), jax.ShapeDtypeStruct((n, d), BF16),
                   jax.ShapeDtypeStruct((N_EXPERTS, n), F32)],
        compiler_params=_params("parallel"),
        name="post",
    )(*parts, *ws, x2, gffn[None, :], wrh, wrl, rb)


def _moe_dense(xn, gates, x1, w_gate, w_up, w_down, tm=512):
    n, d = x1.shape
    ne, _, de = w_gate.shape
    return pl.pallas_call(
        _moe_dense_kernel,
        grid=(n // tm, ne),
        in_specs=[pl.BlockSpec((tm, d), lambda i, e: (i, 0)),
                  pl.BlockSpec((tm, ne), lambda i, e: (i, 0)),
                  pl.BlockSpec((tm, d), lambda i, e: (i, 0)),
                  pl.BlockSpec((1, d, de), lambda i, e: (e, 0, 0)),
                  pl.BlockSpec((1, d, de), lambda i, e: (e, 0, 0)),
                  pl.BlockSpec((1, de, d), lambda i, e: (e, 0, 0))],
        out_specs=pl.BlockSpec((tm, d), lambda i, e: (i, 0)),
        out_shape=jax.ShapeDtypeStruct((n, d), F32),
        compiler_params=_params("parallel", "arbitrary"),
        name="moe_dense",
    )(xn, gates, x1, w_gate.astype(BF16), w_up.astype(BF16), w_down.astype(BF16))


def kernel(x, mix_norm, ffn_norm, w_in_ab, mla_q_a_norm, mla_w_q_up, mla_kv_a_norm, mla_w_kv_up, mla_qn_gain, mla_kn_gain, mla_qr_gain, mla_kr_gain, gqa_q_gain, gqa_k_gain, w_out_ab, w_in_c, win_q_gain, win_k_gain, win_sink, w_out_c, rel_bias, moe_w_group, moe_b_group, moe_w_router, moe_b_router, moe_w_gate, moe_w_up, moe_w_down):
    b, s, d = x.shape
    n = b * s
    depth = mix_norm.shape[0]
    x2 = x.reshape(n, d)
    bias = None
    for layer in range(depth):
        i = layer // 2
        if layer % 2 == 0:
            qa, ka, va, qb, kb, vb = _prep_ab(
                x2, s, mix_norm[layer], w_in_ab[i], mla_q_a_norm[i], mla_w_q_up[i], mla_kv_a_norm[i],
                mla_w_kv_up[i], mla_qn_gain[i], mla_kn_gain[i], mla_qr_gain[i], mla_kr_gain[i],
                gqa_q_gain[i], gqa_k_gain[i])
            r3 = lambda t: t.reshape(b, s, t.shape[1])
            out_a = _mla_attn(r3(qa), r3(ka), r3(va)).reshape(n, 512)
            out_b = _gqa_attn(r3(qb), r3(kb), r3(vb)).reshape(n, 512)
            parts, ws = [out_a, out_b], [w_out_ab[i][:512], w_out_ab[i][512:]]
        else:
            if bias is None:
                bias = _rel_bias_table(rel_bias)
            qc, kc, vc = _prep_c(x2, mix_norm[layer], w_in_c[i], win_q_gain[i], win_k_gain[i])
            r3 = lambda t: t.reshape(b, s, t.shape[1])
            out_c = _win_attn(r3(qc), r3(kc), r3(vc), win_sink[i], bias).reshape(n, 1024)
            parts, ws = [out_c], [w_out_c[i]]
        x1, xn, gates_t = _post(parts, ws, x2, ffn_norm[layer], moe_w_group[layer], moe_b_group[layer],
                                moe_w_router[layer], moe_b_router[layer])
        x2 = _moe_dense(xn, gates_t.T, x1, moe_w_gate[layer], moe_w_up[layer], moe_w_down[layer])
    return x2.reshape(b, s, d)
```

```python
import functools
import math

import numpy as np
import jax
import jax.numpy as jnp
from jax import lax
from jax.experimental import pallas as pl
from jax.experimental.pallas import tpu as pltpu

F32 = jnp.float32
BF16 = jnp.bfloat16

EPS = 1e-6
ROPE_THETA = 10000.0
LANES = 128
HEAD = 64
ROPE_HALF = 16
GRID_W = 64
WINDOW = 128
Q_BLOCK = 128
REL_BUCKETS = 32
REL_MAX_DIST = 128
N_GROUPS = 4
EXPERTS_PER_GROUP = 4
N_EXPERTS = 16
PAIRS_PER_GROUP = 6
N_BUCKETS = N_GROUPS * PAIRS_PER_GROUP
PAIR_LO = (0, 0, 0, 1, 1, 2)
PAIR_HI = (1, 2, 3, 2, 3, 3)
TOKEN_TILE_ROWS = 8
MOE_TILE = 256
ROUTER_ROWS = 32
VMEM_LIMIT = 56 * 1024 * 1024

_NT = (((1,), (1,)), ((), ()))


def _dot(a, b):
    return jnp.dot(a, b, preferred_element_type=F32)


def _dot_nt(a, b):
    return lax.dot_general(a, b, _NT, preferred_element_type=F32)


def _split_bf16(a):
    hi = a.astype(BF16)
    lo = (a - hi.astype(F32)).astype(BF16)
    return hi, lo


def _row_rmsnorm(t, gain):
    return t * lax.rsqrt(jnp.mean(t * t, axis=-1, keepdims=True) + EPS) * gain


def _seg_rmsnorm(t, mseg, msegt, invlen, gain):
    hi, lo = _split_bf16(t * t)
    sums = _dot(hi, mseg) + _dot(lo, mseg)
    inv = lax.rsqrt(sums * invlen + EPS)
    ihi, ilo = _split_bf16(inv)
    scale = _dot(ihi, msegt) + _dot(ilo, msegt)
    return t * scale * gain


def _rope128(t, cos, sin_signed, first_half):
    up = pltpu.roll(t, LANES - ROPE_HALF, 1)
    dn = pltpu.roll(t, ROPE_HALF, 1)
    return t * cos + jnp.where(first_half, up, dn) * sin_signed


def _dup_halves(blk, lo):
    sw = pltpu.roll(blk, HEAD, 1)
    return jnp.where(lo, blk, sw), jnp.where(lo, sw, blk)


def _lane_masks():
    lane = lax.broadcasted_iota(jnp.int32, (1, LANES), 1)
    return lane < HEAD, (lane % (2 * ROPE_HALF)) < ROPE_HALF


def _prep_ab_kernel(x_ref, gmix_ref, w0_ref, qag_ref, wq_ref, kvag_ref, wkv_ref,
                    mq_ref, mqt_ref, ilq_ref, gq_ref,
                    mk_ref, mkt_ref, ilk_ref, gk_ref,
                    mr_ref, mrt_ref, ilr_ref, gr_ref,
                    mbq_ref, mbqt_ref, ilbq_ref, gbq_ref,
                    mbk_ref, mbkt_ref, ilbk_ref, gbk_ref,
                    cosa_ref, sina_ref, cosb_ref, sinb_ref,
                    qa_ref, ka_ref, va_ref, qb_ref, kb_ref, vb_ref):
    lo, first_half = _lane_masks()
    h = _row_rmsnorm(x_ref[...], gmix_ref[...]).astype(BF16)
    proj = _dot(h, w0_ref[...])
    cosa, sina = cosa_ref[...], sina_ref[...]
    cosb, sinb = cosb_ref[...], sinb_ref[...]

    qn = _row_rmsnorm(proj[:, 0:256], qag_ref[...]).astype(BF16)
    q = _seg_rmsnorm(_dot(qn, wq_ref[...]), mq_ref[...], mqt_ref[...], ilq_ref[...], gq_ref[...])
    for hd in range(8):
        sl = slice(LANES * hd, LANES * (hd + 1))
        qa_ref[:, sl] = _rope128(q[:, sl], cosa, sina, first_half).astype(BF16)

    kvn = _row_rmsnorm(proj[:, 256:384], kvag_ref[...]).astype(BF16)
    kv = _dot(kvn, wkv_ref[...])
    kn = _seg_rmsnorm(kv[:, 0:1024], mk_ref[...], mkt_ref[...], ilk_ref[...], gk_ref[...])
    kr = _seg_rmsnorm(proj[:, 384:512], mr_ref[...], mrt_ref[...], ilr_ref[...], gr_ref[...])
    kr = _rope128(kr, cosa, sina, first_half)
    for hd in range(8):
        sl = slice(LANES * hd, LANES * (hd + 1))
        ka_ref[:, sl] = (kn[:, sl] + kr).astype(BF16)
    va_ref[...] = kv[:, 1024:1536].astype(BF16)

    bq = _seg_rmsnorm(proj[:, 512:1024], mbq_ref[...], mbqt_ref[...], ilbq_ref[...], gbq_ref[...])
    for blk in range(4):
        sl = slice(LANES * blk, LANES * (blk + 1))
        qb_ref[:, sl] = _rope128(bq[:, sl], cosb, sinb, first_half).astype(BF16)
    bk = _seg_rmsnorm(proj[:, 1024:1152], mbk_ref[...], mbkt_ref[...], ilbk_ref[...], gbk_ref[...])
    bk = _rope128(bk, cosb, sinb, first_half)
    k0, k1 = _dup_halves(bk, lo)
    kb_ref[:, 0:LANES] = k0.astype(BF16)
    kb_ref[:, LANES:2 * LANES] = k1.astype(BF16)
    v0, v1 = _dup_halves(proj[:, 1152:1280], lo)
    vb_ref[:, 0:LANES] = v0.astype(BF16)
    vb_ref[:, LANES:2 * LANES] = v1.astype(BF16)


def _prep_c_kernel(x_ref, gmix_ref, w_ref,
                   mq_ref, mqt_ref, ilq_ref, gq_ref,
                   mk_ref, mkt_ref, ilk_ref, gk_ref,
                   qc_ref, kc_ref, vc_ref):
    lo, _ = _lane_masks()
    h = _row_rmsnorm(x_ref[...], gmix_ref[...]).astype(BF16)
    proj = _dot(h, w_ref[...])
    q = _seg_rmsnorm(proj[:, 0:1024], mq_ref[...], mqt_ref[...], ilq_ref[...], gq_ref[...])
    qc_ref[...] = q.astype(BF16)
    k = _seg_rmsnorm(proj[:, 1024:1280], mk_ref[...], mkt_ref[...], ilk_ref[...], gk_ref[...])
    for blk in range(2):
        sl = slice(LANES * blk, LANES * (blk + 1))
        k0, k1 = _dup_halves(k[:, sl], lo)
        kc_ref[:, 2 * blk * LANES:(2 * blk + 1) * LANES] = k0.astype(BF16)
        kc_ref[:, (2 * blk + 1) * LANES:(2 * blk + 2) * LANES] = k1.astype(BF16)
        v0, v1 = _dup_halves(proj[:, 1280 + LANES * blk:1280 + LANES * (blk + 1)], lo)
        vc_ref[:, 2 * blk * LANES:(2 * blk + 1) * LANES] = v0.astype(BF16)
        vc_ref[:, (2 * blk + 1) * LANES:(2 * blk + 2) * LANES] = v1.astype(BF16)


def _softmax_pv(s, v):
    m = jnp.max(s, axis=-1, keepdims=True)
    p = jnp.exp(s - m)
    l = jnp.sum(p, axis=-1, keepdims=True)
    return _dot(p.astype(BF16), v) / l


def _mla_attn_kernel(q_ref, k_ref, v_ref, o_ref):
    lo, _ = _lane_masks()
    v = v_ref[0]
    outs = []
    for j in range(2):
        sl = slice(LANES * j, LANES * (j + 1))
        s = _dot_nt(q_ref[0, :, sl], k_ref[0, :, sl])
        outs.append(_softmax_pv(s, v))
    o_ref[0] = jnp.where(lo, outs[0], outs[1]).astype(o_ref.dtype)


def _stack_heads(q_ref, first_block, lo):
    parts = []
    for a in range(4):
        blk = q_ref[0, :, LANES * (first_block + a // 2):LANES * (first_block + a // 2 + 1)]
        keep = lo if a % 2 == 0 else jnp.logical_not(lo)
        parts.append(jnp.where(keep, blk, jnp.zeros_like(blk)))
    return jnp.concatenate(parts, axis=0)


def _gqa_attn_kernel(q_ref, k_ref, v_ref, o_ref):
    lo, _ = _lane_masks()
    tq = q_ref.shape[1]
    qs = _stack_heads(q_ref, 0, lo)
    pv = _softmax_pv(_dot_nt(qs, k_ref[0]), v_ref[0])
    for j in range(2):
        o_ref[0, :, LANES * j:LANES * (j + 1)] = jnp.where(
            lo, pv[2 * j * tq:(2 * j + 1) * tq], pv[(2 * j + 1) * tq:(2 * j + 2) * tq]).astype(o_ref.dtype)


def _win_attn_kernel(sink_ref, q_ref, kp_ref, kc_ref, kn_ref, vp_ref, vc_ref, vn_ref, bias_ref, o_ref,
                     *, seq_len):
    lo, _ = _lane_masks()
    tq = q_ref.shape[1]
    i = pl.program_id(1)
    kj = lax.broadcasted_iota(jnp.int32, (1, 3 * tq), 1)
    key_pos = (i - 1) * tq + kj
    valid = jnp.logical_and(key_pos >= 0, key_pos < seq_len)
    for g in range(4):
        sl = slice(LANES * g, LANES * (g + 1))
        k = jnp.concatenate([kp_ref[0, :, sl], kc_ref[0, :, sl], kn_ref[0, :, sl]], axis=0)
        v = jnp.concatenate([vp_ref[0, :, sl], vc_ref[0, :, sl], vn_ref[0, :, sl]], axis=0)
        qs = _stack_heads(q_ref, 2 * g, lo)
        s = _dot_nt(qs, k) + bias_ref[4 * g:4 * g + 4].reshape(4 * tq, 3 * tq)
        s = jnp.where(valid, s, -jnp.inf)
        sink = jnp.concatenate([jnp.full((tq, 1), sink_ref[4 * g + a], F32) for a in range(4)], axis=0)
        m = jnp.maximum(jnp.max(s, axis=-1, keepdims=True), sink)
        p = jnp.exp(s - m)
        l = jnp.sum(p, axis=-1, keepdims=True) + jnp.exp(sink - m)
        pv = _dot(p.astype(BF16), v) / l
        for j in range(2):
            blk = 2 * g + j
            o_ref[0, :, LANES * blk:LANES * (blk + 1)] = jnp.where(
                lo, pv[2 * j * tq:(2 * j + 1) * tq], pv[(2 * j + 1) * tq:(2 * j + 2) * tq]).astype(o_ref.dtype)


def _rel_bias_kernel(bucket_ref, band_ref, relt_ref, o_ref):
    bucket = bucket_ref[...]
    acc = jnp.zeros(o_ref.shape, F32)
    for r in range(REL_BUCKETS):
        acc = acc + jnp.where(bucket == r, relt_ref[:, r:r + 1], 0.0)
    o_ref[...] = jnp.where(band_ref[...] > 0, acc, -jnp.inf)


def _to_tiles(ref, val):
    for c in range(TOKEN_TILE_ROWS):
        ref[:, c, :] = val[:, LANES * c:LANES * (c + 1)]


def _from_tiles(ref):
    return jnp.concatenate([ref[:, c, :] for c in range(TOKEN_TILE_ROWS)], axis=1)


def _post_kernel(*refs, n_parts):
    parts = refs[:n_parts]
    ws = refs[n_parts:2 * n_parts]
    (x_ref, gffn_ref, wrh_ref, wrl_ref, rb_ref, tri_ref,
     x1t_ref, bucket_ref, rank_ref, wab_ref, counts_ref, carry_ref) = refs[2 * n_parts:]
    acc = x_ref[...]
    for p_ref, w_ref in zip(parts, ws):
        acc = acc + _dot(p_ref[...], w_ref[...])
    _to_tiles(x1t_ref, acc)
    xn = _row_rmsnorm(acc, gffn_ref[...])

    xh, xl = _split_bf16(xn)
    wh, wl = wrh_ref[...], wrl_ref[...]
    logit = _dot_nt(wh, xh) + _dot_nt(wh, xl) + _dot_nt(wl, xh) + rb_ref[...]
    g = [logit[r:r + 1, :] for r in range(N_GROUPS)]
    gmax = jnp.maximum(jnp.maximum(g[0], g[1]), jnp.maximum(g[2], g[3]))
    gsum = sum(jnp.exp(gi - gmax) for gi in g)
    g_p = 1.0 / gsum
    gidx = jnp.where(g[0] == gmax, 0, jnp.where(g[1] == gmax, 1, jnp.where(g[2] == gmax, 2, 3)))
    e = []
    for j in range(EXPERTS_PER_GROUP):
        rows = [logit[N_GROUPS + EXPERTS_PER_GROUP * gg + j:N_GROUPS + EXPERTS_PER_GROUP * gg + j + 1, :]
                for gg in range(N_GROUPS)]
        e.append(jnp.where(gidx == 0, rows[0], jnp.where(gidx == 1, rows[1],
                                                         jnp.where(gidx == 2, rows[2], rows[3]))))
    emax = jnp.maximum(jnp.maximum(e[0], e[1]), jnp.maximum(e[2], e[3]))
    ex = [jnp.exp(ej - emax) for ej in e]
    esum = ex[0] + ex[1] + ex[2] + ex[3]
    pr = [exj / esum for exj in ex]
    p1 = jnp.maximum(jnp.maximum(pr[0], pr[1]), jnp.maximum(pr[2], pr[3]))
    i1 = jnp.where(pr[0] == p1, 0, jnp.where(pr[1] == p1, 1, jnp.where(pr[2] == p1, 2, 3)))
    rest = [jnp.where(i1 == j, -1.0, pr[j]) for j in range(EXPERTS_PER_GROUP)]
    p2 = jnp.maximum(jnp.maximum(rest[0], rest[1]), jnp.maximum(rest[2], rest[3]))
    i2 = jnp.where(rest[0] == p2, 0, jnp.where(rest[1] == p2, 1, jnp.where(rest[2] == p2, 2, 3)))
    den = p1 + p2
    w1 = g_p * p1 / den
    w2 = g_p * p2 / den
    lo_e = jnp.minimum(i1, i2)
    hi_e = jnp.maximum(i1, i2)
    pair = jnp.where(lo_e == 0, hi_e - 1, jnp.where(lo_e == 1, hi_e + 1, 5))
    bucket = gidx * PAIRS_PER_GROUP + pair
    bucket_ref[...] = bucket
    first_is_lo = i1 < i2
    wab_ref[...] = jnp.concatenate([jnp.where(first_is_lo, w1, w2), jnp.where(first_is_lo, w2, w1)], axis=0)

    @pl.when(pl.program_id(0) == 0)
    def _():
        carry_ref[...] = jnp.zeros_like(carry_ref)

    onehot = (lax.broadcasted_iota(jnp.int32, (ROUTER_ROWS, bucket.shape[1]), 0) == bucket).astype(F32)
    before = _dot(onehot.astype(BF16), tri_ref[...]) + carry_ref[:, 0:1]
    rank_ref[...] = jnp.sum(onehot * before, axis=0, keepdims=True).astype(jnp.int32)
    carry_ref[...] = carry_ref[...] + jnp.sum(onehot, axis=1, keepdims=True)
    counts_ref[...] = carry_ref[...]


def _start_row_gather(idx_ref, table_hbm, dst, sem):
    def body(r, carry):
        pltpu.make_async_copy(table_hbm.at[idx_ref[0, 0, r]], dst.at[r], sem).start()
        return carry
    lax.fori_loop(0, dst.shape[0], body, 0)


def _wait_row_gather(table_hbm, dst, sem):
    pltpu.make_async_copy(table_hbm.at[pl.ds(0, dst.shape[0])], dst, sem).wait()


def _moe_sparse_kernel(ea_ref, eb_ref, nt_ref, src_cur_ref, src_nxt_ref, wab_ref, gffn_ref, x1t_hbm,
                       wga_ref, wua_ref, wda_ref, wgb_ref, wub_ref, wdb_ref, o_ref, buf, sem):
    t = pl.program_id(0)
    nt = nt_ref[0]
    slot = t % 2

    @pl.when(t == 0)
    def _():
        _start_row_gather(src_cur_ref, x1t_hbm, buf.at[0], sem.at[0])

    @pl.when(t + 1 < nt)
    def _():
        _start_row_gather(src_nxt_ref, x1t_hbm, buf.at[1 - slot], sem.at[1 - slot])

    @pl.when(t < nt)
    def _():
        _wait_row_gather(x1t_hbm, buf.at[slot], sem.at[slot])
        x = _from_tiles(buf.at[slot])
        xn = _row_rmsnorm(x, gffn_ref[...]).astype(BF16)
        wab = wab_ref[...]
        out = x
        for col, (wg, wu, wd) in enumerate(((wga_ref, wua_ref, wda_ref), (wgb_ref, wub_ref, wdb_ref))):
            hid = jax.nn.silu(_dot(xn, wg[0])) * _dot(xn, wu[0])
            out = out + wab[:, col:col + 1] * _dot(hid.astype(BF16), wd[0])
        _to_tiles(o_ref, out)

    @pl.when(t >= nt)
    def _():
        o_ref[...] = jnp.zeros_like(o_ref)


def _unpermute_kernel(pos_cur_ref, pos_nxt_ref, sorted_hbm, o_ref, buf, sem):
    t = pl.program_id(0)
    slot = t % 2

    @pl.when(t == 0)
    def _():
        _start_row_gather(pos_cur_ref, sorted_hbm, buf.at[0], sem.at[0])

    @pl.when(t + 1 < pl.num_programs(0))
    def _():
        _start_row_gather(pos_nxt_ref, sorted_hbm, buf.at[1 - slot], sem.at[1 - slot])

    _wait_row_gather(sorted_hbm, buf.at[slot], sem.at[slot])
    o_ref[...] = _from_tiles(buf.at[slot])


def _seg_mats(width, segments):
    m = np.zeros((width, LANES), np.float32)
    invlen = np.ones((1, LANES), np.float32)
    for c, (start, length) in enumerate(segments):
        m[start:start + length, c] = 1.0
        invlen[0, c] = 1.0 / length
    return jnp.asarray(m, BF16), jnp.asarray(m.T, BF16), jnp.asarray(invlen)


def _rope_tables(pos_list, seq_len):
    inv = ROPE_THETA ** (-jnp.arange(0, 2 * ROPE_HALF, 2, dtype=F32) / (2 * ROPE_HALF))
    cos_cols, sin_cols = [], []
    for pos in pos_list:
        if pos is None:
            cos_cols.append(jnp.ones((seq_len, 2 * ROPE_HALF), F32))
            sin_cols.append(jnp.zeros((seq_len, 2 * ROPE_HALF), F32))
        else:
            ang = pos.astype(F32)[:, None] * inv[None, :]
            c, s = jnp.cos(ang), jnp.sin(ang)
            cos_cols.append(jnp.concatenate([c, c], axis=1))
            sin_cols.append(jnp.concatenate([-s, s], axis=1))
    return jnp.concatenate(cos_cols, axis=1), jnp.concatenate(sin_cols, axis=1)


def _t5_bucket_np(rel):
    nb = REL_BUCKETS // 2
    max_exact = nb // 2
    ret = np.where(rel > 0, nb, 0)
    n = np.abs(rel)
    nf = np.maximum(n, 1).astype(np.float32)
    large = max_exact + (np.log(nf / np.float32(max_exact)) / np.float32(math.log(REL_MAX_DIST / max_exact))
                         * np.float32(nb - max_exact)).astype(np.int32)
    large = np.minimum(large, nb - 1)
    return ret + np.where(n < max_exact, n, large)


def _full(shape):
    nd = len(shape)
    return pl.BlockSpec(shape, lambda *_: (0,) * nd)


def _params(*sem):
    return pltpu.CompilerParams(dimension_semantics=sem, vmem_limit_bytes=VMEM_LIMIT)


def _prep_ab(x2, seq_len, gmix, w_in, qag, wq, kvag, wkv, qn_g, kn_g, qr_g, kr_g, bq_g, bk_g, tm=256):
    n = x2.shape[0]
    scale_a = (HEAD + 2 * ROPE_HALF) ** -0.5
    scale_b = HEAD ** -0.5
    zeros = lambda r, c: jnp.zeros((r, c), F32)
    d = w_in.shape[0]
    w0 = jnp.concatenate([w_in[:, 0:384], zeros(d, HEAD), w_in[:, 384:416], zeros(d, 32), w_in[:, 416:1184]],
                         axis=1).astype(BF16)
    wq_p = jnp.concatenate([wq.reshape(-1, 8, 96), jnp.zeros((wq.shape[0], 8, 32), F32)], axis=2)
    wq_p = wq_p.reshape(-1, 1024).astype(BF16)
    wkv_r = wkv.reshape(-1, 8, 128)
    wk_p = jnp.concatenate([wkv_r[:, :, :HEAD], jnp.zeros_like(wkv_r[:, :, :HEAD])], axis=2).reshape(-1, 1024)
    wkv_p = jnp.concatenate([wk_p, wkv_r[:, :, HEAD:].reshape(-1, 512)], axis=1).astype(BF16)

    seg_q = [(LANES * h, HEAD) for h in range(8)] + [(LANES * h + HEAD, 32) for h in range(8)]
    gq = jnp.tile(jnp.concatenate([qn_g, qr_g, jnp.zeros((32,), F32)]), 8)[None, :] * scale_a
    seg_k = [(LANES * h, HEAD) for h in range(8)]
    gk = jnp.tile(jnp.concatenate([kn_g, jnp.zeros((HEAD,), F32)]), 8)[None, :]
    seg_r = [(HEAD, 32)]
    gr = jnp.concatenate([jnp.zeros((HEAD,), F32), kr_g, jnp.zeros((32,), F32)])[None, :]
    seg_bq = [(HEAD * h, HEAD) for h in range(8)]
    gbq = jnp.tile(bq_g, 8)[None, :] * scale_b
    seg_bk = [(HEAD * h, HEAD) for h in range(2)]
    gbk = jnp.tile(bk_g, 2)[None, :]

    pos = jnp.arange(seq_len)
    cosa, sina = _rope_tables([None, None, pos, None], seq_len)
    row, col = pos // GRID_W, pos % GRID_W
    cosb, sinb = _rope_tables([row, col, row, col], seq_len)

    consts = [gmix[None, :], w0, qag[None, :], wq_p, kvag[None, :], wkv_p,
              *_seg_mats(1024, seg_q), gq, *_seg_mats(1024, seg_k), gk, *_seg_mats(LANES, seg_r), gr,
              *_seg_mats(512, seg_bq), gbq, *_seg_mats(LANES, seg_bk), gbk]
    nsb = seq_len // tm
    rope_spec = pl.BlockSpec((tm, LANES), lambda i: (i % nsb, 0))
    row_spec = lambda w: pl.BlockSpec((tm, w), lambda i: (i, 0))
    out_widths = (1024, 1024, 512, 512, 256, 256)
    return pl.pallas_call(
        _prep_ab_kernel,
        grid=(n // tm,),
        in_specs=[row_spec(x2.shape[1])] + [_full(c.shape) for c in consts] + [rope_spec] * 4,
        out_specs=[row_spec(w) for w in out_widths],
        out_shape=[jax.ShapeDtypeStruct((n, w), BF16) for w in out_widths],
        compiler_params=_params("parallel"),
        name="prep_ab",
    )(x2, *consts, cosa, sina, cosb, sinb)


def _prep_c(x2, gmix, w_in, q_g, k_g, tm=256):
    n = x2.shape[0]
    seg_q = [(HEAD * h, HEAD) for h in range(16)]
    gq = jnp.tile(q_g, 16)[None, :] * (HEAD ** -0.5)
    seg_k = [(HEAD * h, HEAD) for h in range(4)]
    gk = jnp.tile(k_g, 4)[None, :]
    consts = [gmix[None, :], w_in.astype(BF16), *_seg_mats(1024, seg_q), gq, *_seg_mats(256, seg_k), gk]
    row_spec = lambda w: pl.BlockSpec((tm, w), lambda i: (i, 0))
    out_widths = (1024, 512, 512)
    return pl.pallas_call(
        _prep_c_kernel,
        grid=(n // tm,),
        in_specs=[row_spec(x2.shape[1])] + [_full(c.shape) for c in consts],
        out_specs=[row_spec(w) for w in out_widths],
        out_shape=[jax.ShapeDtypeStruct((n, w), BF16) for w in out_widths],
        compiler_params=_params("parallel"),
        name="prep_c",
    )(x2, *consts)


def _mla_attn(qa, ka, va, tq=256):
    b, s, _ = qa.shape
    return pl.pallas_call(
        _mla_attn_kernel,
        grid=(b, 4, s // tq),
        in_specs=[pl.BlockSpec((1, tq, 2 * LANES), lambda bi, hp, qi: (bi, qi, hp)),
                  pl.BlockSpec((1, s, 2 * LANES), lambda bi, hp, qi: (bi, 0, hp)),
                  pl.BlockSpec((1, s, LANES), lambda bi, hp, qi: (bi, 0, hp))],
        out_specs=pl.BlockSpec((1, tq, LANES), lambda bi, hp, qi: (bi, qi, hp)),
        out_shape=jax.ShapeDtypeStruct((b, s, 512), BF16),
        compiler_params=_params("parallel", "parallel", "parallel"),
        name="mla_attn",
    )(qa, ka, va)


def _gqa_attn(qb, kb, vb, tq=128):
    b, s, _ = qb.shape
    return pl.pallas_call(
        _gqa_attn_kernel,
        grid=(b, 2, s // tq),
        in_specs=[pl.BlockSpec((1, tq, 2 * LANES), lambda bi, g, qi: (bi, qi, g)),
                  pl.BlockSpec((1, s, LANES), lambda bi, g, qi: (bi, 0, g)),
                  pl.BlockSpec((1, s, LANES), lambda bi, g, qi: (bi, 0, g))],
        out_specs=pl.BlockSpec((1, tq, 2 * LANES), lambda bi, g, qi: (bi, qi, g)),
        out_shape=jax.ShapeDtypeStruct((b, s, 512), BF16),
        compiler_params=_params("parallel", "parallel", "parallel"),
        name="gqa_attn",
    )(qb, kb, vb)


def _rel_bias_table(rel_bias):
    span = Q_BLOCK + 2 * WINDOW
    rel = np.arange(span)[None, :] - WINDOW - np.arange(Q_BLOCK)[:, None]
    bucket = _t5_bucket_np(rel).astype(np.int32).reshape(1, -1)
    band = (np.abs(rel) <= WINDOW).astype(np.int32).reshape(1, -1)
    heads = rel_bias.shape[1]
    cols = bucket.shape[1]
    chunk = cols // 8
    table = pl.pallas_call(
        _rel_bias_kernel,
        grid=(8,),
        in_specs=[pl.BlockSpec((1, chunk), lambda i: (0, i)), pl.BlockSpec((1, chunk), lambda i: (0, i)),
                  _full((heads, REL_BUCKETS))],
        out_specs=pl.BlockSpec((heads, chunk), lambda i: (0, i)),
        out_shape=jax.ShapeDtypeStruct((heads, cols), F32),
        compiler_params=_params("parallel"),
        name="rel_bias",
    )(jnp.asarray(bucket), jnp.asarray(band), rel_bias.T)
    return table.reshape(heads, Q_BLOCK, span)


def _win_attn(qc, kc, vc, sink, bias):
    b, s, _ = qc.shape
    tq = Q_BLOCK
    nb = s // tq
    prev = lambda bi, i: (bi, jnp.maximum(i - 1, 0), 0)
    cur = lambda bi, i: (bi, i, 0)
    nxt = lambda bi, i: (bi, jnp.minimum(i + 1, nb - 1), 0)
    kv_spec = lambda im: pl.BlockSpec((1, tq, 4 * LANES), im)
    return pl.pallas_call(
        functools.partial(_win_attn_kernel, seq_len=s),
        grid=(b, nb),
        in_specs=[pl.BlockSpec(memory_space=pltpu.SMEM),
                  pl.BlockSpec((1, tq, 8 * LANES), cur),
                  kv_spec(prev), kv_spec(cur), kv_spec(nxt),
                  kv_spec(prev), kv_spec(cur), kv_spec(nxt),
                  _full(bias.shape)],
        out_specs=pl.BlockSpec((1, tq, 8 * LANES), cur),
        out_shape=jax.ShapeDtypeStruct((b, s, 1024), BF16),
        compiler_params=_params("parallel", "parallel"),
        name="win_attn",
    )(sink, qc, kc, kc, kc, vc, vc, vc, bias)


def _post(parts, ws, x2, gffn, w_group, b_group, w_router, b_router, tm=256):
    n, d = x2.shape
    wr = jnp.concatenate([w_group.T, w_router.T, jnp.zeros((ROUTER_ROWS - N_GROUPS - N_EXPERTS, d), F32)], axis=0)
    wrh = wr.astype(BF16)
    wrl = (wr - wrh.astype(F32)).astype(BF16)
    rb = jnp.concatenate([b_group, b_router, jnp.zeros((ROUTER_ROWS - N_GROUPS - N_EXPERTS,), F32)])[:, None]
    row_spec = lambda w: pl.BlockSpec((tm, w), lambda i: (i, 0))
    lane_spec = lambda r: pl.BlockSpec((r, tm), lambda i: (0, i))
    ws = [w.astype(BF16) for w in ws]
    tri = jnp.asarray(np.triu(np.ones((tm, tm), np.float32), 1), BF16)
    return pl.pallas_call(
        functools.partial(_post_kernel, n_parts=len(parts)),
        grid=(n // tm,),
        in_specs=[row_spec(p.shape[1]) for p in parts] + [_full(w.shape) for w in ws]
                 + [row_spec(d), _full((1, d)), _full(wrh.shape), _full(wrl.shape), _full(rb.shape),
                    _full(tri.shape)],
        out_specs=[pl.BlockSpec((tm, TOKEN_TILE_ROWS, LANES), lambda i: (i, 0, 0)),
                   lane_spec(1), lane_spec(1), lane_spec(2), _full((ROUTER_ROWS, LANES))],
        out_shape=[jax.ShapeDtypeStruct((n, TOKEN_TILE_ROWS, LANES), F32),
                   jax.ShapeDtypeStruct((1, n), jnp.int32), jax.ShapeDtypeStruct((1, n), jnp.int32),
                   jax.ShapeDtypeStruct((2, n), F32), jax.ShapeDtypeStruct((ROUTER_ROWS, LANES), F32)],
        scratch_shapes=[pltpu.VMEM((ROUTER_ROWS, LANES), F32)],
        compiler_params=_params("arbitrary"),
        name="post",
    )(*parts, *ws, x2, gffn[None, :], wrh, wrl, rb, tri)


def _route(bucket, rank, wab, counts, tmm):
    n = bucket.shape[1]
    n_tiles_max = n // tmm + N_BUCKETS
    n_slots = n_tiles_max * tmm
    bucket, rank = bucket[0], rank[0]
    cnt = counts[:N_BUCKETS, 0].astype(jnp.int32)
    padded = (cnt + tmm - 1) // tmm * tmm
    ends = jnp.cumsum(padded)
    starts = ends - padded
    pos = starts[bucket] + rank
    n_tiles = ends[-1] // tmm
    tile = jnp.minimum(jnp.arange(n_tiles_max, dtype=jnp.int32), n_tiles - 1)
    tile_bucket = jnp.searchsorted(ends, tile * tmm, side="right").astype(jnp.int32)
    grp, pair = tile_bucket // PAIRS_PER_GROUP, tile_bucket % PAIRS_PER_GROUP
    tile_ea = grp * EXPERTS_PER_GROUP + jnp.asarray(PAIR_LO, jnp.int32)[pair]
    tile_eb = grp * EXPERTS_PER_GROUP + jnp.asarray(PAIR_HI, jnp.int32)[pair]
    token = jnp.arange(n, dtype=jnp.int32)
    src = (jnp.arange(n_slots + tmm, dtype=jnp.int32) % n).at[pos].set(token)
    w_sorted = jnp.zeros((n_slots, 2), F32).at[pos].set(wab.T)
    return pos, src.reshape(n_tiles_max + 1, 1, tmm), w_sorted, tile_ea, tile_eb, n_tiles.reshape(1)


def _moe_sparse(x1t, gffn, src, w_sorted, tile_ea, tile_eb, n_tiles, w_gate, w_up, w_down, tmm):
    n_tiles_max = src.shape[0] - 1
    ne, d, de = w_gate.shape
    idx_spec = lambda off: pl.BlockSpec((1, 1, tmm), lambda t, ea, eb, nt: (t + off, 0, 0),
                                        memory_space=pltpu.SMEM)
    up_spec = lambda which: pl.BlockSpec((1, d, de), lambda t, ea, eb, nt: ((ea, eb)[which][t], 0, 0))
    down_spec = lambda which: pl.BlockSpec((1, de, d), lambda t, ea, eb, nt: ((ea, eb)[which][t], 0, 0))
    wg, wu, wd = w_gate.astype(BF16), w_up.astype(BF16), w_down.astype(BF16)
    return pl.pallas_call(
        _moe_sparse_kernel,
        grid_spec=pltpu.PrefetchScalarGridSpec(
            num_scalar_prefetch=3,
            grid=(n_tiles_max,),
            in_specs=[idx_spec(0), idx_spec(1),
                      pl.BlockSpec((tmm, 2), lambda t, ea, eb, nt: (t, 0)),
                      pl.BlockSpec((1, d), lambda t, ea, eb, nt: (0, 0)),
                      pl.BlockSpec(memory_space=pl.ANY),
                      up_spec(0), up_spec(0), down_spec(0), up_spec(1), up_spec(1), down_spec(1)],
            out_specs=pl.BlockSpec((tmm, TOKEN_TILE_ROWS, LANES), lambda t, ea, eb, nt: (t, 0, 0)),
            scratch_shapes=[pltpu.VMEM((2, tmm, TOKEN_TILE_ROWS, LANES), F32), pltpu.SemaphoreType.DMA((2,))]),
        out_shape=jax.ShapeDtypeStruct((n_tiles_max * tmm, TOKEN_TILE_ROWS, LANES), F32),
        compiler_params=_params("arbitrary"),
        name="moe_sparse",
    )(tile_ea, tile_eb, n_tiles, src, src, w_sorted, gffn[None, :], x1t, wg, wu, wd, wg, wu, wd)


def _unpermute(sorted_rows, pos, tm=256):
    n = pos.shape[0]
    d = TOKEN_TILE_ROWS * LANES
    pos2 = jnp.concatenate([pos, jnp.zeros((tm,), jnp.int32)]).reshape(n // tm + 1, 1, tm)
    idx_spec = lambda off: pl.BlockSpec((1, 1, tm), lambda t: (t + off, 0, 0), memory_space=pltpu.SMEM)
    return pl.pallas_call(
        _unpermute_kernel,
        grid=(n // tm,),
        in_specs=[idx_spec(0), idx_spec(1), pl.BlockSpec(memory_space=pl.ANY)],
        out_specs=pl.BlockSpec((tm, d), lambda t: (t, 0)),
        out_shape=jax.ShapeDtypeStruct((n, d), F32),
        scratch_shapes=[pltpu.VMEM((2, tm, TOKEN_TILE_ROWS, LANES), F32), pltpu.SemaphoreType.DMA((2,))],
        compiler_params=_params("arbitrary"),
        name="unpermute",
    )(pos2, pos2, sorted_rows)


def kernel(x, mix_norm, ffn_norm, w_in_ab, mla_q_a_norm, mla_w_q_up, mla_kv_a_norm, mla_w_kv_up, mla_qn_gain, mla_kn_gain, mla_qr_gain, mla_kr_gain, gqa_q_gain, gqa_k_gain, w_out_ab, w_in_c, win_q_gain, win_k_gain, win_sink, w_out_c, rel_bias, moe_w_group, moe_b_group, moe_w_router, moe_b_router, moe_w_gate, moe_w_up, moe_w_down):
    b, s, d = x.shape
    n = b * s
    depth = mix_norm.shape[0]
    x2 = x.reshape(n, d)
    bias = None
    for layer in range(depth):
        i = layer // 2
        if layer % 2 == 0:
            qa, ka, va, qb, kb, vb = _prep_ab(
                x2, s, mix_norm[layer], w_in_ab[i], mla_q_a_norm[i], mla_w_q_up[i], mla_kv_a_norm[i],
                mla_w_kv_up[i], mla_qn_gain[i], mla_kn_gain[i], mla_qr_gain[i], mla_kr_gain[i],
                gqa_q_gain[i], gqa_k_gain[i])
            r3 = lambda t: t.reshape(b, s, t.shape[1])
            out_a = _mla_attn(r3(qa), r3(ka), r3(va)).reshape(n, 512)
            out_b = _gqa_attn(r3(qb), r3(kb), r3(vb)).reshape(n, 512)
            parts, ws = [out_a, out_b], [w_out_ab[i][:512], w_out_ab[i][512:]]
        else:
            if bias is None:
                bias = _rel_bias_table(rel_bias)
            qc, kc, vc = _prep_c(x2, mix_norm[layer], w_in_c[i], win_q_gain[i], win_k_gain[i])
            r3 = lambda t: t.reshape(b, s, t.shape[1])
            out_c = _win_attn(r3(qc), r3(kc), r3(vc), win_sink[i], bias).reshape(n, 1024)
            parts, ws = [out_c], [w_out_c[i]]
        x1t, bucket, rank, wab, counts = _post(parts, ws, x2, ffn_norm[layer], moe_w_group[layer],
                                               moe_b_group[layer], moe_w_router[layer], moe_b_router[layer])
        pos, src, w_sorted, tile_ea, tile_eb, n_tiles = _route(bucket, rank, wab, counts, MOE_TILE)
        y_sorted = _moe_sparse(x1t, ffn_norm[layer], src, w_sorted, tile_ea, tile_eb, n_tiles,
                               moe_w_gate[layer], moe_w_up[layer], moe_w_down[layer], MOE_TILE)
        x2 = _unpermute(y_sorted, pos)
    return x2.reshape(b, s, d)
```

```python
import functools
import math

import numpy as np
import jax
import jax.numpy as jnp
from jax import lax
from jax.experimental import pallas as pl
from jax.experimental.pallas import tpu as pltpu

F32 = jnp.float32
BF16 = jnp.bfloat16

EPS = 1e-6
ROPE_THETA = 10000.0
LANES = 128
HEAD = 64
ROPE_HALF = 16
GRID_W = 64
WINDOW = 128
Q_BLOCK = 128
REL_BUCKETS = 32
REL_MAX_DIST = 128
N_GROUPS = 4
EXPERTS_PER_GROUP = 4
N_EXPERTS = 16
PAIRS_PER_GROUP = 6
N_BUCKETS = N_GROUPS * PAIRS_PER_GROUP
PAIR_LO = (0, 0, 0, 1, 1, 2)
PAIR_HI = (1, 2, 3, 2, 3, 3)
TOKEN_TILE_ROWS = 8
KEY_CHUNK = 512
LOG2E = math.log2(math.e)
GATHER_UNROLL = 8
MOE_TILE = 256
ROUTER_ROWS = 32
VMEM_LIMIT = 56 * 1024 * 1024

_NT = (((1,), (1,)), ((), ()))


def _dot(a, b):
    return jnp.dot(a, b, preferred_element_type=F32)


def _dot_nt(a, b):
    return lax.dot_general(a, b, _NT, preferred_element_type=F32)


def _split_bf16(a):
    hi = a.astype(BF16)
    lo = (a - hi.astype(F32)).astype(BF16)
    return hi, lo


def _row_rmsnorm(t, gain):
    return t * lax.rsqrt(jnp.mean(t * t, axis=-1, keepdims=True) + EPS) * gain


def _seg_rmsnorm(t, mseg, msegt, invlen, gain):
    hi, lo = _split_bf16(t * t)
    sums = _dot(hi, mseg) + _dot(lo, mseg)
    inv = lax.rsqrt(sums * invlen + EPS)
    ihi, ilo = _split_bf16(inv)
    scale = _dot(ihi, msegt) + _dot(ilo, msegt)
    return t * scale * gain


def _rope128(t, cos, sin_signed, first_half):
    up = pltpu.roll(t, LANES - ROPE_HALF, 1)
    dn = pltpu.roll(t, ROPE_HALF, 1)
    return t * cos + jnp.where(first_half, up, dn) * sin_signed


def _dup_halves(blk, lo):
    sw = pltpu.roll(blk, HEAD, 1)
    return jnp.where(lo, blk, sw), jnp.where(lo, sw, blk)


def _lane_masks():
    lane = lax.broadcasted_iota(jnp.int32, (1, LANES), 1)
    return lane < HEAD, (lane % (2 * ROPE_HALF)) < ROPE_HALF


def _prep_ab_kernel(x_ref, gmix_ref, w0_ref, qag_ref, wq_ref, kvag_ref, wkv_ref,
                    mq_ref, mqt_ref, ilq_ref, gq_ref,
                    mk_ref, mkt_ref, ilk_ref, gk_ref,
                    mr_ref, mrt_ref, ilr_ref, gr_ref,
                    mbq_ref, mbqt_ref, ilbq_ref, gbq_ref,
                    mbk_ref, mbkt_ref, ilbk_ref, gbk_ref,
                    cosa_ref, sina_ref, cosb_ref, sinb_ref,
                    qa_ref, ka_ref, va_ref, qb_ref, kb_ref, vb_ref):
    lo, first_half = _lane_masks()
    h = _row_rmsnorm(x_ref[...], gmix_ref[...]).astype(BF16)
    proj = _dot(h, w0_ref[...])
    cosa, sina = cosa_ref[...], sina_ref[...]
    cosb, sinb = cosb_ref[...], sinb_ref[...]

    qn = _row_rmsnorm(proj[:, 0:256], qag_ref[...]).astype(BF16)
    q = _seg_rmsnorm(_dot(qn, wq_ref[...]), mq_ref[...], mqt_ref[...], ilq_ref[...], gq_ref[...])
    for hd in range(8):
        sl = slice(LANES * hd, LANES * (hd + 1))
        qa_ref[:, sl] = _rope128(q[:, sl], cosa, sina, first_half).astype(BF16)

    kvn = _row_rmsnorm(proj[:, 256:384], kvag_ref[...]).astype(BF16)
    kv = _dot(kvn, wkv_ref[...])
    kn = _seg_rmsnorm(kv[:, 0:1024], mk_ref[...], mkt_ref[...], ilk_ref[...], gk_ref[...])
    kr = _seg_rmsnorm(proj[:, 384:512], mr_ref[...], mrt_ref[...], ilr_ref[...], gr_ref[...])
    kr = _rope128(kr, cosa, sina, first_half)
    for hd in range(8):
        sl = slice(LANES * hd, LANES * (hd + 1))
        ka_ref[:, sl] = (kn[:, sl] + kr).astype(BF16)
    va_ref[...] = kv[:, 1024:1536].astype(BF16)

    bq = _seg_rmsnorm(proj[:, 512:1024], mbq_ref[...], mbqt_ref[...], ilbq_ref[...], gbq_ref[...])
    for blk in range(4):
        sl = slice(LANES * blk, LANES * (blk + 1))
        qb_ref[:, sl] = _rope128(bq[:, sl], cosb, sinb, first_half).astype(BF16)
    bk = _seg_rmsnorm(proj[:, 1024:1152], mbk_ref[...], mbkt_ref[...], ilbk_ref[...], gbk_ref[...])
    bk = _rope128(bk, cosb, sinb, first_half)
    k0, k1 = _dup_halves(bk, lo)
    kb_ref[:, 0:LANES] = k0.astype(BF16)
    kb_ref[:, LANES:2 * LANES] = k1.astype(BF16)
    v0, v1 = _dup_halves(proj[:, 1152:1280], lo)
    vb_ref[:, 0:LANES] = v0.astype(BF16)
    vb_ref[:, LANES:2 * LANES] = v1.astype(BF16)


def _prep_c_kernel(x_ref, gmix_ref, w_ref,
                   mq_ref, mqt_ref, ilq_ref, gq_ref,
                   mk_ref, mkt_ref, ilk_ref, gk_ref,
                   qc_ref, kc_ref, vc_ref):
    lo, _ = _lane_masks()
    h = _row_rmsnorm(x_ref[...], gmix_ref[...]).astype(BF16)
    proj = _dot(h, w_ref[...])
    q = _seg_rmsnorm(proj[:, 0:1024], mq_ref[...], mqt_ref[...], ilq_ref[...], gq_ref[...])
    qc_ref[...] = q.astype(BF16)
    k = _seg_rmsnorm(proj[:, 1024:1280], mk_ref[...], mkt_ref[...], ilk_ref[...], gk_ref[...])
    for blk in range(2):
        sl = slice(LANES * blk, LANES * (blk + 1))
        k0, k1 = _dup_halves(k[:, sl], lo)
        kc_ref[:, 2 * blk * LANES:(2 * blk + 1) * LANES] = k0.astype(BF16)
        kc_ref[:, (2 * blk + 1) * LANES:(2 * blk + 2) * LANES] = k1.astype(BF16)
        v0, v1 = _dup_halves(proj[:, 1280 + LANES * blk:1280 + LANES * (blk + 1)], lo)
        vc_ref[:, 2 * blk * LANES:(2 * blk + 1) * LANES] = v0.astype(BF16)
        vc_ref[:, (2 * blk + 1) * LANES:(2 * blk + 2) * LANES] = v1.astype(BF16)


def _softmax_pv(s, v):
    m = jnp.max(s, axis=-1, keepdims=True)
    p = jnp.exp(s - m)
    l = jnp.sum(p, axis=-1, keepdims=True)
    return _dot(p.astype(BF16), v) / l


def _lane_chunk_reduce(op, t):
    out = t[:, 0:LANES]
    for j in range(1, t.shape[1] // LANES):
        out = op(out, t[:, LANES * j:LANES * (j + 1)])
    return out


def _scores_phase(q, k_ref, lanes, s_buf):
    m_part = None
    for c in range(k_ref.shape[1] // KEY_CHUNK):
        ks = slice(KEY_CHUNK * c, KEY_CHUNK * (c + 1))
        s_c = _dot_nt(q, k_ref[0, ks, lanes])
        s_buf[:, ks] = s_c
        mc = _lane_chunk_reduce(jnp.maximum, s_c)
        m_part = mc if m_part is None else jnp.maximum(m_part, mc)
    return jnp.max(m_part, axis=-1, keepdims=True)


def _pv_phase(s_buf, m, v_ref):
    l_part = None
    acc = None
    for c in range(v_ref.shape[1] // KEY_CHUNK):
        ks = slice(KEY_CHUNK * c, KEY_CHUNK * (c + 1))
        p = jnp.exp2(s_buf[:, ks] - m)
        lc = _lane_chunk_reduce(jnp.add, p)
        pv = _dot(p.astype(BF16), v_ref[0, ks, :])
        l_part = lc if l_part is None else l_part + lc
        acc = pv if acc is None else acc + pv
    return acc / jnp.sum(l_part, axis=-1, keepdims=True)


def _mla_attn_kernel(q_ref, k_ref, v_ref, o_ref, s0_ref, s1_ref):
    lo, _ = _lane_masks()
    m0 = _scores_phase(q_ref[0, :, 0:LANES], k_ref, slice(0, LANES), s0_ref)
    m1 = _scores_phase(q_ref[0, :, LANES:2 * LANES], k_ref, slice(LANES, 2 * LANES), s1_ref)
    o0 = _pv_phase(s0_ref, m0, v_ref)
    o1 = _pv_phase(s1_ref, m1, v_ref)
    o_ref[0] = jnp.where(lo, o0, o1).astype(o_ref.dtype)


def _stack_heads(q_ref, first_block, lo):
    parts = []
    for a in range(4):
        blk = q_ref[0, :, LANES * (first_block + a // 2):LANES * (first_block + a // 2 + 1)]
        keep = lo if a % 2 == 0 else jnp.logical_not(lo)
        parts.append(jnp.where(keep, blk, jnp.zeros_like(blk)))
    return jnp.concatenate(parts, axis=0)


def _gqa_attn_kernel(q_ref, k_ref, v_ref, o_ref, s0_ref, s1_ref):
    lo, _ = _lane_masks()
    tq = q_ref.shape[1]
    hi = jnp.logical_not(lo)
    ms = []
    for j, s_ref in enumerate((s0_ref, s1_ref)):
        blk = q_ref[0, :, LANES * j:LANES * (j + 1)]
        zero = jnp.zeros_like(blk)
        qs = jnp.concatenate([jnp.where(lo, blk, zero), jnp.where(hi, blk, zero)], axis=0)
        ms.append(_scores_phase(qs, k_ref, slice(0, LANES), s_ref))
    for j, s_ref in enumerate((s0_ref, s1_ref)):
        pv = _pv_phase(s_ref, ms[j], v_ref)
        o_ref[0, :, LANES * j:LANES * (j + 1)] = jnp.where(lo, pv[0:tq], pv[tq:2 * tq]).astype(o_ref.dtype)


def _win_attn_kernel(sink_ref, q_ref, kp_ref, kc_ref, kn_ref, vp_ref, vc_ref, vn_ref, bias_ref, o_ref,
                     *, seq_len):
    lo, _ = _lane_masks()
    tq = Q_BLOCK
    nq = q_ref.shape[1] // tq
    i = pl.program_id(1)
    kj = lax.broadcasted_iota(jnp.int32, (1, 3 * tq), 1)
    staged = []
    for g in range(4):
        sl = slice(LANES * g, LANES * (g + 1))
        kcat = jnp.concatenate([kp_ref[0, :, sl], kc_ref[0, :, sl], kn_ref[0, :, sl]], axis=0)
        vcat = jnp.concatenate([vp_ref[0, :, sl], vc_ref[0, :, sl], vn_ref[0, :, sl]], axis=0)
        bias = bias_ref[4 * g:4 * g + 4].reshape(4 * tq, 3 * tq)
        sink = jnp.concatenate([jnp.full((tq, 1), sink_ref[4 * g + a] * LOG2E, F32) for a in range(4)], axis=0)
        for u in range(nq):
            rows = slice(tq * u, tq * (u + 1))
            parts = []
            for a in range(4):
                blk = q_ref[0, rows, LANES * (2 * g + a // 2):LANES * (2 * g + a // 2 + 1)]
                keep = lo if a % 2 == 0 else jnp.logical_not(lo)
                parts.append(jnp.where(keep, blk, jnp.zeros_like(blk)))
            qs = jnp.concatenate(parts, axis=0)
            key_pos = (i * nq + u - 1) * tq + kj
            valid = jnp.logical_and(key_pos >= 0, key_pos < seq_len)
            s = _dot_nt(qs, kcat[tq * u:tq * (u + 3)]) + bias
            s = jnp.where(valid, s, -jnp.inf)
            m = jnp.maximum(jnp.max(_lane_chunk_reduce(jnp.maximum, s), axis=-1, keepdims=True), sink)
            staged.append((g, u, s, m, sink, vcat[tq * u:tq * (u + 3)]))
    for g, u, s, m, sink, v in staged:
        p = jnp.exp2(s - m)
        l = jnp.sum(_lane_chunk_reduce(jnp.add, p), axis=-1, keepdims=True) + jnp.exp2(sink - m)
        pv = _dot(p.astype(BF16), v) / l
        for j in range(2):
            blk = 2 * g + j
            o_ref[0, tq * u:tq * (u + 1), LANES * blk:LANES * (blk + 1)] = jnp.where(
                lo, pv[2 * j * tq:(2 * j + 1) * tq], pv[(2 * j + 1) * tq:(2 * j + 2) * tq]).astype(o_ref.dtype)


def _rel_bias_kernel(bucket_ref, band_ref, relt_ref, o_ref):
    bucket = bucket_ref[...]
    acc = jnp.zeros(o_ref.shape, F32)
    for r in range(REL_BUCKETS):
        acc = acc + jnp.where(bucket == r, relt_ref[:, r:r + 1], 0.0)
    o_ref[...] = jnp.where(band_ref[...] > 0, acc * LOG2E, -jnp.inf)


def _to_tiles(ref, val):
    rows = ref.shape[0] // TOKEN_TILE_ROWS
    for c in range(TOKEN_TILE_ROWS):
        ref[pl.ds(c, rows, stride=TOKEN_TILE_ROWS), :] = val[:, LANES * c:LANES * (c + 1)]


def _from_tiles(ref):
    rows = ref.shape[0] // TOKEN_TILE_ROWS
    return jnp.concatenate([ref[pl.ds(c, rows, stride=TOKEN_TILE_ROWS), :] for c in range(TOKEN_TILE_ROWS)],
                           axis=1)


def _post_kernel(*refs, n_parts):
    parts = refs[:n_parts]
    ws = refs[n_parts:2 * n_parts]
    (x_ref, gffn_ref, wrh_ref, wrl_ref, rb_ref, tri_ref,
     x1t_ref, bucket_ref, rank_ref, wab_ref, counts_ref, carry_ref) = refs[2 * n_parts:]
    acc = x_ref[...]
    for p_ref, w_ref in zip(parts, ws):
        acc = acc + _dot(p_ref[...], w_ref[...])
    _to_tiles(x1t_ref, acc)
    xn = _row_rmsnorm(acc, gffn_ref[...])

    xh, xl = _split_bf16(xn)
    wh, wl = wrh_ref[...], wrl_ref[...]
    logit = _dot_nt(wh, xh) + _dot_nt(wh, xl) + _dot_nt(wl, xh) + rb_ref[...]
    g = [logit[r:r + 1, :] for r in range(N_GROUPS)]
    gmax = jnp.maximum(jnp.maximum(g[0], g[1]), jnp.maximum(g[2], g[3]))
    gsum = sum(jnp.exp(gi - gmax) for gi in g)
    g_p = 1.0 / gsum
    gidx = jnp.where(g[0] == gmax, 0, jnp.where(g[1] == gmax, 1, jnp.where(g[2] == gmax, 2, 3)))
    e = []
    for j in range(EXPERTS_PER_GROUP):
        rows = [logit[N_GROUPS + EXPERTS_PER_GROUP * gg + j:N_GROUPS + EXPERTS_PER_GROUP * gg + j + 1, :]
                for gg in range(N_GROUPS)]
        e.append(jnp.where(gidx == 0, rows[0], jnp.where(gidx == 1, rows[1],
                                                         jnp.where(gidx == 2, rows[2], rows[3]))))
    emax = jnp.maximum(jnp.maximum(e[0], e[1]), jnp.maximum(e[2], e[3]))
    ex = [jnp.exp(ej - emax) for ej in e]
    esum = ex[0] + ex[1] + ex[2] + ex[3]
    pr = [exj / esum for exj in ex]
    p1 = jnp.maximum(jnp.maximum(pr[0], pr[1]), jnp.maximum(pr[2], pr[3]))
    i1 = jnp.where(pr[0] == p1, 0, jnp.where(pr[1] == p1, 1, jnp.where(pr[2] == p1, 2, 3)))
    rest = [jnp.where(i1 == j, -1.0, pr[j]) for j in range(EXPERTS_PER_GROUP)]
    p2 = jnp.maximum(jnp.maximum(rest[0], rest[1]), jnp.maximum(rest[2], rest[3]))
    i2 = jnp.where(rest[0] == p2, 0, jnp.where(rest[1] == p2, 1, jnp.where(rest[2] == p2, 2, 3)))
    den = p1 + p2
    w1 = g_p * p1 / den
    w2 = g_p * p2 / den
    lo_e = jnp.minimum(i1, i2)
    hi_e = jnp.maximum(i1, i2)
    pair = jnp.where(lo_e == 0, hi_e - 1, jnp.where(lo_e == 1, hi_e + 1, 5))
    bucket = gidx * PAIRS_PER_GROUP + pair
    bucket_ref[...] = bucket
    first_is_lo = i1 < i2
    wab_ref[...] = jnp.concatenate([jnp.where(first_is_lo, w1, w2), jnp.where(first_is_lo, w2, w1)], axis=0)

    @pl.when(pl.program_id(0) == 0)
    def _():
        carry_ref[...] = jnp.zeros_like(carry_ref)

    onehot = (lax.broadcasted_iota(jnp.int32, (ROUTER_ROWS, bucket.shape[1]), 0) == bucket).astype(F32)
    before = _dot(onehot.astype(BF16), tri_ref[...]) + carry_ref[:, 0:1]
    rank_ref[...] = jnp.sum(onehot * before, axis=0, keepdims=True).astype(jnp.int32)
    carry_ref[...] = carry_ref[...] + jnp.sum(onehot, axis=1, keepdims=True)
    counts_ref[...] = carry_ref[...]


def _start_row_gather(idx_ref, table_hbm, dst, sem):
    def body(blk, carry):
        for j in range(GATHER_UNROLL):
            r = blk * GATHER_UNROLL + j
            src_row = pl.multiple_of(idx_ref[0, 0, r] * TOKEN_TILE_ROWS, TOKEN_TILE_ROWS)
            dst_row = pl.multiple_of(r * TOKEN_TILE_ROWS, TOKEN_TILE_ROWS)
            pltpu.make_async_copy(table_hbm.at[pl.ds(src_row, TOKEN_TILE_ROWS)],
                                  dst.at[pl.ds(dst_row, TOKEN_TILE_ROWS)], sem).start()
        return carry
    lax.fori_loop(0, dst.shape[0] // (TOKEN_TILE_ROWS * GATHER_UNROLL), body, 0)


def _wait_row_gather(table_hbm, dst, sem):
    pltpu.make_async_copy(table_hbm.at[pl.ds(0, dst.shape[0])], dst, sem).wait()


def _moe_sparse_kernel(ea_ref, eb_ref, nt_ref, src_cur_ref, src_nxt_ref, wab_ref, gffn_ref, x1t_hbm,
                       wga_ref, wua_ref, wda_ref, wgb_ref, wub_ref, wdb_ref, o_ref, buf, sem):
    t = pl.program_id(0)
    nt = nt_ref[0]
    slot = t % 2

    @pl.when(t == 0)
    def _():
        _start_row_gather(src_cur_ref, x1t_hbm, buf.at[0], sem.at[0])

    @pl.when(t + 1 < nt)
    def _():
        _start_row_gather(src_nxt_ref, x1t_hbm, buf.at[1 - slot], sem.at[1 - slot])

    @pl.when(t < nt)
    def _():
        _wait_row_gather(x1t_hbm, buf.at[slot], sem.at[slot])
        x = _from_tiles(buf.at[slot])
        xn = _row_rmsnorm(x, gffn_ref[...]).astype(BF16)
        wab = wab_ref[...]
        out = x
        for col, (wg, wu, wd) in enumerate(((wga_ref, wua_ref, wda_ref), (wgb_ref, wub_ref, wdb_ref))):
            hid = jax.nn.silu(_dot(xn, wg[0])) * _dot(xn, wu[0])
            out = out + wab[:, col:col + 1] * _dot(hid.astype(BF16), wd[0])
        _to_tiles(o_ref, out)

    @pl.when(t >= nt)
    def _():
        o_ref[...] = jnp.zeros_like(o_ref)


def _unpermute_kernel(pos_cur_ref, pos_nxt_ref, sorted_hbm, o_ref, buf, sem):
    t = pl.program_id(0)
    slot = t % 2

    @pl.when(t == 0)
    def _():
        _start_row_gather(pos_cur_ref, sorted_hbm, buf.at[0], sem.at[0])

    @pl.when(t + 1 < pl.num_programs(0))
    def _():
        _start_row_gather(pos_nxt_ref, sorted_hbm, buf.at[1 - slot], sem.at[1 - slot])

    _wait_row_gather(sorted_hbm, buf.at[slot], sem.at[slot])
    o_ref[...] = _from_tiles(buf.at[slot])


def _seg_mats(width, segments):
    m = np.zeros((width, LANES), np.float32)
    invlen = np.ones((1, LANES), np.float32)
    for c, (start, length) in enumerate(segments):
        m[start:start + length, c] = 1.0
        invlen[0, c] = 1.0 / length
    return jnp.asarray(m, BF16), jnp.asarray(m.T, BF16), jnp.asarray(invlen)


def _rope_tables(pos_list, seq_len):
    inv = ROPE_THETA ** (-jnp.arange(0, 2 * ROPE_HALF, 2, dtype=F32) / (2 * ROPE_HALF))
    cos_cols, sin_cols = [], []
    for pos in pos_list:
        if pos is None:
            cos_cols.append(jnp.ones((seq_len, 2 * ROPE_HALF), F32))
            sin_cols.append(jnp.zeros((seq_len, 2 * ROPE_HALF), F32))
        else:
            ang = pos.astype(F32)[:, None] * inv[None, :]
            c, s = jnp.cos(ang), jnp.sin(ang)
            cos_cols.append(jnp.concatenate([c, c], axis=1))
            sin_cols.append(jnp.concatenate([-s, s], axis=1))
    return jnp.concatenate(cos_cols, axis=1), jnp.concatenate(sin_cols, axis=1)


def _t5_bucket_np(rel):
    nb = REL_BUCKETS // 2
    max_exact = nb // 2
    ret = np.where(rel > 0, nb, 0)
    n = np.abs(rel)
    nf = np.maximum(n, 1).astype(np.float32)
    large = max_exact + (np.log(nf / np.float32(max_exact)) / np.float32(math.log(REL_MAX_DIST / max_exact))
                         * np.float32(nb - max_exact)).astype(np.int32)
    large = np.minimum(large, nb - 1)
    return ret + np.where(n < max_exact, n, large)


def _full(shape):
    nd = len(shape)
    return pl.BlockSpec(shape, lambda *_: (0,) * nd)


def _params(*sem):
    return pltpu.CompilerParams(dimension_semantics=sem, vmem_limit_bytes=VMEM_LIMIT)


def _prep_ab(x2, seq_len, gmix, w_in, qag, wq, kvag, wkv, qn_g, kn_g, qr_g, kr_g, bq_g, bk_g, tm=256):
    n = x2.shape[0]
    scale_a = (HEAD + 2 * ROPE_HALF) ** -0.5
    scale_b = HEAD ** -0.5
    zeros = lambda r, c: jnp.zeros((r, c), F32)
    d = w_in.shape[0]
    w0 = jnp.concatenate([w_in[:, 0:384], zeros(d, HEAD), w_in[:, 384:416], zeros(d, 32), w_in[:, 416:1184]],
                         axis=1).astype(BF16)
    wq_p = jnp.concatenate([wq.reshape(-1, 8, 96), jnp.zeros((wq.shape[0], 8, 32), F32)], axis=2)
    wq_p = wq_p.reshape(-1, 1024).astype(BF16)
    wkv_r = wkv.reshape(-1, 8, 128)
    wk_p = jnp.concatenate([wkv_r[:, :, :HEAD], jnp.zeros_like(wkv_r[:, :, :HEAD])], axis=2).reshape(-1, 1024)
    wkv_p = jnp.concatenate([wk_p, wkv_r[:, :, HEAD:].reshape(-1, 512)], axis=1).astype(BF16)

    seg_q = [(LANES * h, HEAD) for h in range(8)] + [(LANES * h + HEAD, 32) for h in range(8)]
    gq = jnp.tile(jnp.concatenate([qn_g, qr_g, jnp.zeros((32,), F32)]), 8)[None, :] * (scale_a * LOG2E)
    seg_k = [(LANES * h, HEAD) for h in range(8)]
    gk = jnp.tile(jnp.concatenate([kn_g, jnp.zeros((HEAD,), F32)]), 8)[None, :]
    seg_r = [(HEAD, 32)]
    gr = jnp.concatenate([jnp.zeros((HEAD,), F32), kr_g, jnp.zeros((32,), F32)])[None, :]
    seg_bq = [(HEAD * h, HEAD) for h in range(8)]
    gbq = jnp.tile(bq_g, 8)[None, :] * (scale_b * LOG2E)
    seg_bk = [(HEAD * h, HEAD) for h in range(2)]
    gbk = jnp.tile(bk_g, 2)[None, :]

    pos = jnp.arange(seq_len)
    cosa, sina = _rope_tables([None, None, pos, None], seq_len)
    row, col = pos // GRID_W, pos % GRID_W
    cosb, sinb = _rope_tables([row, col, row, col], seq_len)

    consts = [gmix[None, :], w0, qag[None, :], wq_p, kvag[None, :], wkv_p,
              *_seg_mats(1024, seg_q), gq, *_seg_mats(1024, seg_k), gk, *_seg_mats(LANES, seg_r), gr,
              *_seg_mats(512, seg_bq), gbq, *_seg_mats(LANES, seg_bk), gbk]
    nsb = seq_len // tm
    rope_spec = pl.BlockSpec((tm, LANES), lambda i: (i % nsb, 0))
    row_spec = lambda w: pl.BlockSpec((tm, w), lambda i: (i, 0))
    out_widths = (1024, 1024, 512, 512, 256, 256)
    return pl.pallas_call(
        _prep_ab_kernel,
        grid=(n // tm,),
        in_specs=[row_spec(x2.shape[1])] + [_full(c.shape) for c in consts] + [rope_spec] * 4,
        out_specs=[row_spec(w) for w in out_widths],
        out_shape=[jax.ShapeDtypeStruct((n, w), BF16) for w in out_widths],
        compiler_params=_params("parallel"),
        name="prep_ab",
    )(x2, *consts, cosa, sina, cosb, sinb)


def _prep_c(x2, gmix, w_in, q_g, k_g, tm=256):
    n = x2.shape[0]
    seg_q = [(HEAD * h, HEAD) for h in range(16)]
    gq = jnp.tile(q_g, 16)[None, :] * (HEAD ** -0.5 * LOG2E)
    seg_k = [(HEAD * h, HEAD) for h in range(4)]
    gk = jnp.tile(k_g, 4)[None, :]
    consts = [gmix[None, :], w_in.astype(BF16), *_seg_mats(1024, seg_q), gq, *_seg_mats(256, seg_k), gk]
    row_spec = lambda w: pl.BlockSpec((tm, w), lambda i: (i, 0))
    out_widths = (1024, 512, 512)
    return pl.pallas_call(
        _prep_c_kernel,
        grid=(n // tm,),
        in_specs=[row_spec(x2.shape[1])] + [_full(c.shape) for c in consts],
        out_specs=[row_spec(w) for w in out_widths],
        out_shape=[jax.ShapeDtypeStruct((n, w), BF16) for w in out_widths],
        compiler_params=_params("parallel"),
        name="prep_c",
    )(x2, *consts)


def _mla_attn(qa, ka, va, tq=512):
    b, s, _ = qa.shape
    return pl.pallas_call(
        _mla_attn_kernel,
        grid=(b, 4, s // tq),
        in_specs=[pl.BlockSpec((1, tq, 2 * LANES), lambda bi, hp, qi: (bi, qi, hp)),
                  pl.BlockSpec((1, s, 2 * LANES), lambda bi, hp, qi: (bi, 0, hp)),
                  pl.BlockSpec((1, s, LANES), lambda bi, hp, qi: (bi, 0, hp))],
        out_specs=pl.BlockSpec((1, tq, LANES), lambda bi, hp, qi: (bi, qi, hp)),
        out_shape=jax.ShapeDtypeStruct((b, s, 512), BF16),
        scratch_shapes=[pltpu.VMEM((tq, s), F32), pltpu.VMEM((tq, s), F32)],
        compiler_params=_params("parallel", "parallel", "parallel"),
        name="mla_attn",
    )(qa, ka, va)


def _gqa_attn(qb, kb, vb, tq=256):
    b, s, _ = qb.shape
    return pl.pallas_call(
        _gqa_attn_kernel,
        grid=(b, 2, s // tq),
        in_specs=[pl.BlockSpec((1, tq, 2 * LANES), lambda bi, g, qi: (bi, qi, g)),
                  pl.BlockSpec((1, s, LANES), lambda bi, g, qi: (bi, 0, g)),
                  pl.BlockSpec((1, s, LANES), lambda bi, g, qi: (bi, 0, g))],
        out_specs=pl.BlockSpec((1, tq, 2 * LANES), lambda bi, g, qi: (bi, qi, g)),
        out_shape=jax.ShapeDtypeStruct((b, s, 512), BF16),
        scratch_shapes=[pltpu.VMEM((2 * tq, s), F32), pltpu.VMEM((2 * tq, s), F32)],
        compiler_params=_params("parallel", "parallel", "parallel"),
        name="gqa_attn",
    )(qb, kb, vb)


def _rel_bias_table(rel_bias):
    span = Q_BLOCK + 2 * WINDOW
    rel = np.arange(span)[None, :] - WINDOW - np.arange(Q_BLOCK)[:, None]
    bucket = _t5_bucket_np(rel).astype(np.int32).reshape(1, -1)
    band = (np.abs(rel) <= WINDOW).astype(np.int32).reshape(1, -1)
    heads = rel_bias.shape[1]
    cols = bucket.shape[1]
    chunk = cols // 8
    table = pl.pallas_call(
        _rel_bias_kernel,
        grid=(8,),
        in_specs=[pl.BlockSpec((1, chunk), lambda i: (0, i)), pl.BlockSpec((1, chunk), lambda i: (0, i)),
                  _full((heads, REL_BUCKETS))],
        out_specs=pl.BlockSpec((heads, chunk), lambda i: (0, i)),
        out_shape=jax.ShapeDtypeStruct((heads, cols), F32),
        compiler_params=_params("parallel"),
        name="rel_bias",
    )(jnp.asarray(bucket), jnp.asarray(band), rel_bias.T)
    return table.reshape(heads, Q_BLOCK, span)


def _win_attn(qc, kc, vc, sink, bias, nq=2):
    b, s, _ = qc.shape
    tq = Q_BLOCK
    nb = s // tq
    prev = lambda bi, i: (bi, jnp.maximum(i * nq - 1, 0), 0)
    cur = lambda bi, i: (bi, i, 0)
    nxt = lambda bi, i: (bi, jnp.minimum((i + 1) * nq, nb - 1), 0)
    edge_spec = lambda im: pl.BlockSpec((1, tq, 4 * LANES), im)
    cur_spec = pl.BlockSpec((1, nq * tq, 4 * LANES), cur)
    return pl.pallas_call(
        functools.partial(_win_attn_kernel, seq_len=s),
        grid=(b, nb // nq),
        in_specs=[pl.BlockSpec(memory_space=pltpu.SMEM),
                  pl.BlockSpec((1, nq * tq, 8 * LANES), cur),
                  edge_spec(prev), cur_spec, edge_spec(nxt),
                  edge_spec(prev), cur_spec, edge_spec(nxt),
                  _full(bias.shape)],
        out_specs=pl.BlockSpec((1, nq * tq, 8 * LANES), cur),
        out_shape=jax.ShapeDtypeStruct((b, s, 1024), BF16),
        compiler_params=_params("parallel", "parallel"),
        name="win_attn",
    )(sink, qc, kc, kc, kc, vc, vc, vc, bias)


def _post(parts, ws, x2, gffn, w_group, b_group, w_router, b_router, tm=256):
    n, d = x2.shape
    wr = jnp.concatenate([w_group.T, w_router.T, jnp.zeros((ROUTER_ROWS - N_GROUPS - N_EXPERTS, d), F32)], axis=0)
    wrh = wr.astype(BF16)
    wrl = (wr - wrh.astype(F32)).astype(BF16)
    rb = jnp.concatenate([b_group, b_router, jnp.zeros((ROUTER_ROWS - N_GROUPS - N_EXPERTS,), F32)])[:, None]
    row_spec = lambda w: pl.BlockSpec((tm, w), lambda i: (i, 0))
    lane_spec = lambda r: pl.BlockSpec((r, tm), lambda i: (0, i))
    ws = [w.astype(BF16) for w in ws]
    tri = jnp.asarray(np.triu(np.ones((tm, tm), np.float32), 1), BF16)
    return pl.pallas_call(
        functools.partial(_post_kernel, n_parts=len(parts)),
        grid=(n // tm,),
        in_specs=[row_spec(p.shape[1]) for p in parts] + [_full(w.shape) for w in ws]
                 + [row_spec(d), _full((1, d)), _full(wrh.shape), _full(wrl.shape), _full(rb.shape),
                    _full(tri.shape)],
        out_specs=[pl.BlockSpec((tm * TOKEN_TILE_ROWS, LANES), lambda i: (i, 0)),
                   lane_spec(1), lane_spec(1), lane_spec(2), _full((ROUTER_ROWS, LANES))],
        out_shape=[jax.ShapeDtypeStruct((n * TOKEN_TILE_ROWS, LANES), F32),
                   jax.ShapeDtypeStruct((1, n), jnp.int32), jax.ShapeDtypeStruct((1, n), jnp.int32),
                   jax.ShapeDtypeStruct((2, n), F32), jax.ShapeDtypeStruct((ROUTER_ROWS, LANES), F32)],
        scratch_shapes=[pltpu.VMEM((ROUTER_ROWS, LANES), F32)],
        compiler_params=_params("arbitrary"),
        name="post",
    )(*parts, *ws, x2, gffn[None, :], wrh, wrl, rb, tri)


def _route(bucket, rank, wab, counts, tmm):
    n = bucket.shape[1]
    n_tiles_max = n // tmm + N_BUCKETS
    n_slots = n_tiles_max * tmm
    bucket, rank = bucket[0], rank[0]
    cnt = counts[:N_BUCKETS, 0].astype(jnp.int32)
    padded = (cnt + tmm - 1) // tmm * tmm
    ends = jnp.cumsum(padded)
    starts = ends - padded
    pos = starts[bucket] + rank
    n_tiles = ends[-1] // tmm
    tile = jnp.minimum(jnp.arange(n_tiles_max, dtype=jnp.int32), n_tiles - 1)
    tile_bucket = jnp.sum((tile * tmm)[:, None] >= ends[None, :], axis=1).astype(jnp.int32)
    grp, pair = tile_bucket // PAIRS_PER_GROUP, tile_bucket % PAIRS_PER_GROUP
    tile_ea = grp * EXPERTS_PER_GROUP + jnp.asarray(PAIR_LO, jnp.int32)[pair]
    tile_eb = grp * EXPERTS_PER_GROUP + jnp.asarray(PAIR_HI, jnp.int32)[pair]
    token = jnp.arange(n, dtype=jnp.int32)
    src = (jnp.arange(n_slots + tmm, dtype=jnp.int32) % n).at[pos].set(token)
    w_sorted = jnp.zeros((n_slots, 2), F32).at[pos].set(wab.T)
    return pos, src.reshape(n_tiles_max + 1, 1, tmm), w_sorted, tile_ea, tile_eb, n_tiles.reshape(1)


def _moe_sparse(x1t, gffn, src, w_sorted, tile_ea, tile_eb, n_tiles, w_gate, w_up, w_down, tmm):
    n_tiles_max = src.shape[0] - 1
    ne, d, de = w_gate.shape
    idx_spec = lambda off: pl.BlockSpec((1, 1, tmm), lambda t, ea, eb, nt: (t + off, 0, 0),
                                        memory_space=pltpu.SMEM)
    up_spec = lambda which: pl.BlockSpec((1, d, de), lambda t, ea, eb, nt: ((ea, eb)[which][t], 0, 0))
    down_spec = lambda which: pl.BlockSpec((1, de, d), lambda t, ea, eb, nt: ((ea, eb)[which][t], 0, 0))
    wg, wu, wd = w_gate.astype(BF16), w_up.astype(BF16), w_down.astype(BF16)
    return pl.pallas_call(
        _moe_sparse_kernel,
        grid_spec=pltpu.PrefetchScalarGridSpec(
            num_scalar_prefetch=3,
            grid=(n_tiles_max,),
            in_specs=[idx_spec(0), idx_spec(1),
                      pl.BlockSpec((tmm, 2), lambda t, ea, eb, nt: (t, 0)),
                      pl.BlockSpec((1, d), lambda t, ea, eb, nt: (0, 0)),
                      pl.BlockSpec(memory_space=pl.ANY),
                      up_spec(0), up_spec(0), down_spec(0), up_spec(1), up_spec(1), down_spec(1)],
            out_specs=pl.BlockSpec((tmm * TOKEN_TILE_ROWS, LANES), lambda t, ea, eb, nt: (t, 0)),
            scratch_shapes=[pltpu.VMEM((2, tmm * TOKEN_TILE_ROWS, LANES), F32), pltpu.SemaphoreType.DMA((2,))]),
        out_shape=jax.ShapeDtypeStruct((n_tiles_max * tmm * TOKEN_TILE_ROWS, LANES), F32),
        compiler_params=_params("arbitrary"),
        name="moe_sparse",
    )(tile_ea, tile_eb, n_tiles, src, src, w_sorted, gffn[None, :], x1t, wg, wu, wd, wg, wu, wd)


def _unpermute(sorted_rows, pos, tm=256):
    n = pos.shape[0]
    d = TOKEN_TILE_ROWS * LANES
    pos2 = jnp.concatenate([pos, jnp.zeros((tm,), jnp.int32)]).reshape(n // tm + 1, 1, tm)
    idx_spec = lambda off: pl.BlockSpec((1, 1, tm), lambda t: (t + off, 0, 0), memory_space=pltpu.SMEM)
    return pl.pallas_call(
        _unpermute_kernel,
        grid=(n // tm,),
        in_specs=[idx_spec(0), idx_spec(1), pl.BlockSpec(memory_space=pl.ANY)],
        out_specs=pl.BlockSpec((tm, d), lambda t: (t, 0)),
        out_shape=jax.ShapeDtypeStruct((n, d), F32),
        scratch_shapes=[pltpu.VMEM((2, tm * TOKEN_TILE_ROWS, LANES), F32), pltpu.SemaphoreType.DMA((2,))],
        compiler_params=_params("arbitrary"),
        name="unpermute",
    )(pos2, pos2, sorted_rows)


def kernel(x, mix_norm, ffn_norm, w_in_ab, mla_q_a_norm, mla_w_q_up, mla_kv_a_norm, mla_w_kv_up, mla_qn_gain, mla_kn_gain, mla_qr_gain, mla_kr_gain, gqa_q_gain, gqa_k_gain, w_out_ab, w_in_c, win_q_gain, win_k_gain, win_sink, w_out_c, rel_bias, moe_w_group, moe_b_group, moe_w_router, moe_b_router, moe_w_gate, moe_w_up, moe_w_down):
    b, s, d = x.shape
    n = b * s
    depth = mix_norm.shape[0]
    x2 = x.reshape(n, d)
    bias = None
    for layer in range(depth):
        i = layer // 2
        if layer % 2 == 0:
            qa, ka, va, qb, kb, vb = _prep_ab(
                x2, s, mix_norm[layer], w_in_ab[i], mla_q_a_norm[i], mla_w_q_up[i], mla_kv_a_norm[i],
                mla_w_kv_up[i], mla_qn_gain[i], mla_kn_gain[i], mla_qr_gain[i], mla_kr_gain[i],
                gqa_q_gain[i], gqa_k_gain[i])
            r3 = lambda t: t.reshape(b, s, t.shape[1])
            out_a = _mla_attn(r3(qa), r3(ka), r3(va)).reshape(n, 512)
            out_b = _gqa_attn(r3(qb), r3(kb), r3(vb)).reshape(n, 512)
            parts, ws = [out_a, out_b], [w_out_ab[i][:512], w_out_ab[i][512:]]
        else:
            if bias is None:
                bias = _rel_bias_table(rel_bias)
            qc, kc, vc = _prep_c(x2, mix_norm[layer], w_in_c[i], win_q_gain[i], win_k_gain[i])
            r3 = lambda t: t.reshape(b, s, t.shape[1])
            out_c = _win_attn(r3(qc), r3(kc), r3(vc), win_sink[i], bias).reshape(n, 1024)
            parts, ws = [out_c], [w_out_c[i]]
        x1t, bucket, rank, wab, counts = _post(parts, ws, x2, ffn_norm[layer], moe_w_group[layer],
                                               moe_b_group[layer], moe_w_router[layer], moe_b_router[layer])
        pos, src, w_sorted, tile_ea, tile_eb, n_tiles = _route(bucket, rank, wab, counts, MOE_TILE)
        y_sorted = _moe_sparse(x1t, ffn_norm[layer], src, w_sorted, tile_ea, tile_eb, n_tiles,
                               moe_w_gate[layer], moe_w_up[layer], moe_w_down[layer], MOE_TILE)
        x2 = _unpermute(y_sorted, pos)
    return x2.reshape(b, s, d)
```

```python
import functools
import math

import numpy as np
import jax
import jax.numpy as jnp
from jax import lax
from jax.experimental import pallas as pl
from jax.experimental.pallas import tpu as pltpu

F32 = jnp.float32
BF16 = jnp.bfloat16

EPS = 1e-6
ROPE_THETA = 10000.0
LANES = 128
HEAD = 64
ROPE_HALF = 16
GRID_W = 64
WINDOW = 128
Q_BLOCK = 128
REL_BUCKETS = 32
REL_MAX_DIST = 128
N_GROUPS = 4
EXPERTS_PER_GROUP = 4
N_EXPERTS = 16
PAIRS_PER_GROUP = 6
N_BUCKETS = N_GROUPS * PAIRS_PER_GROUP
PAIR_LO = (0, 0, 0, 1, 1, 2)
PAIR_HI = (1, 2, 3, 2, 3, 3)
TOKEN_TILE_ROWS = 8
KEY_CHUNK = 512
LOG2E = math.log2(math.e)
GATHER_UNROLL = 8
MOE_TILE = 256
ROUTER_ROWS = 32
VMEM_LIMIT = 56 * 1024 * 1024

_NT = (((1,), (1,)), ((), ()))


def _dot(a, b):
    return jnp.dot(a, b, preferred_element_type=F32)


def _dot_nt(a, b):
    return lax.dot_general(a, b, _NT, preferred_element_type=F32)


def _split_bf16(a):
    hi = a.astype(BF16)
    lo = (a - hi.astype(F32)).astype(BF16)
    return hi, lo


def _row_rmsnorm(t, gain):
    return t * lax.rsqrt(jnp.mean(t * t, axis=-1, keepdims=True) + EPS) * gain


def _seg_rmsnorm(t, mseg, msegt, invlen, gain):
    sums = _dot((t * t).astype(BF16), mseg)
    inv = lax.rsqrt(sums * invlen + EPS)
    ihi, ilo = _split_bf16(inv)
    scale = _dot(jnp.concatenate([ihi, ilo], axis=1), msegt)
    return t * scale * gain


def _rope128(t, cos, sin_signed, first_half):
    up = pltpu.roll(t, LANES - ROPE_HALF, 1)
    dn = pltpu.roll(t, ROPE_HALF, 1)
    return t * cos + jnp.where(first_half, up, dn) * sin_signed


def _dup_halves(blk, lo):
    sw = pltpu.roll(blk, HEAD, 1)
    return jnp.where(lo, blk, sw), jnp.where(lo, sw, blk)


def _lane_masks():
    lane = lax.broadcasted_iota(jnp.int32, (1, LANES), 1)
    return lane < HEAD, (lane % (2 * ROPE_HALF)) < ROPE_HALF


def _prep_ab_kernel(x_ref, gmix_ref, w0_ref, qag_ref, wq_ref, kvag_ref, wkv_ref,
                    mq_ref, mqt_ref, ilq_ref, gq_ref,
                    mk_ref, mkt_ref, ilk_ref, gk_ref,
                    mr_ref, mrt_ref, ilr_ref, gr_ref,
                    mbq_ref, mbqt_ref, ilbq_ref, gbq_ref,
                    mbk_ref, mbkt_ref, ilbk_ref, gbk_ref,
                    cosa_ref, sina_ref, cosb_ref, sinb_ref,
                    qa_ref, ka_ref, va_ref, qb_ref, kb_ref, vb_ref):
    lo, first_half = _lane_masks()
    h = _row_rmsnorm(x_ref[...], gmix_ref[...]).astype(BF16)
    proj = _dot(h, w0_ref[...])
    cosa, sina = cosa_ref[...], sina_ref[...]
    cosb, sinb = cosb_ref[...], sinb_ref[...]

    qn = _row_rmsnorm(proj[:, 0:256], qag_ref[...]).astype(BF16)
    q = _seg_rmsnorm(_dot(qn, wq_ref[...]), mq_ref[...], mqt_ref[...], ilq_ref[...], gq_ref[...])
    for hd in range(8):
        sl = slice(LANES * hd, LANES * (hd + 1))
        qa_ref[:, sl] = _rope128(q[:, sl], cosa, sina, first_half).astype(BF16)

    kvn = _row_rmsnorm(proj[:, 256:384], kvag_ref[...]).astype(BF16)
    kv = _dot(kvn, wkv_ref[...])
    ones_hi = jnp.where(lo, 0.0, 1.0)
    kn = _seg_rmsnorm(kv[:, 0:1024], mk_ref[...], mkt_ref[...], ilk_ref[...], gk_ref[...])
    kr = _seg_rmsnorm(proj[:, 384:512], mr_ref[...], mrt_ref[...], ilr_ref[...], gr_ref[...])
    kr = _rope128(kr, cosa, sina, first_half)
    for hd in range(8):
        sl = slice(LANES * hd, LANES * (hd + 1))
        ka_ref[:, sl] = (kn[:, sl] + kr).astype(BF16)
        va_ref[:, sl] = (kv[:, 1024 + LANES * hd:1024 + LANES * (hd + 1)] + ones_hi).astype(BF16)

    bq = _seg_rmsnorm(proj[:, 512:1024], mbq_ref[...], mbqt_ref[...], ilbq_ref[...], gbq_ref[...])
    for blk in range(4):
        sl = slice(LANES * blk, LANES * (blk + 1))
        qb_ref[:, sl] = _rope128(bq[:, sl], cosb, sinb, first_half).astype(BF16)
    bk = _seg_rmsnorm(proj[:, 1024:1152], mbk_ref[...], mbkt_ref[...], ilbk_ref[...], gbk_ref[...])
    bk = _rope128(bk, cosb, sinb, first_half)
    k0, k1 = _dup_halves(bk, lo)
    kb_ref[:, 0:LANES] = k0.astype(BF16)
    kb_ref[:, LANES:2 * LANES] = k1.astype(BF16)
    bv = proj[:, 1152:1280]
    vb_ref[:, 0:LANES] = jnp.where(lo, bv, 1.0).astype(BF16)
    vb_ref[:, LANES:2 * LANES] = jnp.where(lo, pltpu.roll(bv, HEAD, 1), 1.0).astype(BF16)


def _prep_c_kernel(x_ref, gmix_ref, w_ref,
                   mq_ref, mqt_ref, ilq_ref, gq_ref,
                   mk_ref, mkt_ref, ilk_ref, gk_ref,
                   qc_ref, kc_ref, vc_ref):
    lo, _ = _lane_masks()
    h = _row_rmsnorm(x_ref[...], gmix_ref[...]).astype(BF16)
    proj = _dot(h, w_ref[...])
    q = _seg_rmsnorm(proj[:, 0:1024], mq_ref[...], mqt_ref[...], ilq_ref[...], gq_ref[...])
    qc_ref[...] = q.astype(BF16)
    k = _seg_rmsnorm(proj[:, 1024:1280], mk_ref[...], mkt_ref[...], ilk_ref[...], gk_ref[...])
    for blk in range(2):
        sl = slice(LANES * blk, LANES * (blk + 1))
        k0, k1 = _dup_halves(k[:, sl], lo)
        kc_ref[:, 2 * blk * LANES:(2 * blk + 1) * LANES] = k0.astype(BF16)
        kc_ref[:, (2 * blk + 1) * LANES:(2 * blk + 2) * LANES] = k1.astype(BF16)
        v0, v1 = _dup_halves(proj[:, 1280 + LANES * blk:1280 + LANES * (blk + 1)], lo)
        vc_ref[:, 2 * blk * LANES:(2 * blk + 1) * LANES] = v0.astype(BF16)
        vc_ref[:, (2 * blk + 1) * LANES:(2 * blk + 2) * LANES] = v1.astype(BF16)


def _softmax_pv(s, v):
    m = jnp.max(s, axis=-1, keepdims=True)
    p = jnp.exp(s - m)
    l = jnp.sum(p, axis=-1, keepdims=True)
    return _dot(p.astype(BF16), v) / l


def _lane_chunk_reduce(op, t):
    out = t[:, 0:LANES]
    for j in range(1, t.shape[1] // LANES):
        out = op(out, t[:, LANES * j:LANES * (j + 1)])
    return out


def _scores_phase(q, k_ref, lanes, s_buf):
    m_part = None
    for c in range(k_ref.shape[1] // KEY_CHUNK):
        ks = slice(KEY_CHUNK * c, KEY_CHUNK * (c + 1))
        s_c = _dot_nt(q, k_ref[0, ks, lanes])
        s_buf[:, ks] = s_c
        mc = _lane_chunk_reduce(jnp.maximum, s_c)
        m_part = mc if m_part is None else jnp.maximum(m_part, mc)
    return jnp.max(m_part, axis=-1, keepdims=True)


def _pv_phase(s_buf, m, v_ref, lanes):
    lo, _ = _lane_masks()
    acc = None
    for c in range(v_ref.shape[1] // KEY_CHUNK):
        ks = slice(KEY_CHUNK * c, KEY_CHUNK * (c + 1))
        p = jnp.exp2(s_buf[:, ks] - m)
        pv = _dot(p.astype(BF16), v_ref[0, ks, lanes])
        acc = pv if acc is None else acc + pv
    return acc / jnp.where(lo, pltpu.roll(acc, HEAD, 1), 1.0)


def _pair_heads(o_even, o_odd):
    lo, _ = _lane_masks()
    return jnp.where(lo, o_even, pltpu.roll(o_odd, HEAD, 1))


def _mla_attn_kernel(q_ref, k_ref, v_ref, o_ref, s0_ref, s1_ref):
    first, second = slice(0, LANES), slice(LANES, 2 * LANES)
    m0 = _scores_phase(q_ref[0, :, first], k_ref, first, s0_ref)
    m1 = _scores_phase(q_ref[0, :, second], k_ref, second, s1_ref)
    o0 = _pv_phase(s0_ref, m0, v_ref, first)
    o1 = _pv_phase(s1_ref, m1, v_ref, second)
    o_ref[0] = _pair_heads(o0, o1).astype(o_ref.dtype)


def _gqa_attn_kernel(q_ref, k_ref, v_ref, o_ref, s0_ref, s1_ref):
    lo, _ = _lane_masks()
    tq = q_ref.shape[1]
    hi = jnp.logical_not(lo)
    ms = []
    for j, s_ref in enumerate((s0_ref, s1_ref)):
        blk = q_ref[0, :, LANES * j:LANES * (j + 1)]
        zero = jnp.zeros_like(blk)
        qs = jnp.concatenate([jnp.where(lo, blk, zero), jnp.where(hi, blk, zero)], axis=0)
        ms.append(_scores_phase(qs, k_ref, slice(0, LANES), s_ref))
    for j, s_ref in enumerate((s0_ref, s1_ref)):
        o = _pv_phase(s_ref, ms[j], v_ref, slice(0, LANES))
        o_ref[0, :, LANES * j:LANES * (j + 1)] = _pair_heads(o[0:tq], o[tq:2 * tq]).astype(o_ref.dtype)


def _win_attn_kernel(sink_ref, q_ref, kp_ref, kc_ref, kn_ref, vp_ref, vc_ref, vn_ref, bias_ref, o_ref,
                     *, seq_len):
    lo, _ = _lane_masks()
    tq = Q_BLOCK
    nq = q_ref.shape[1] // tq
    i = pl.program_id(1)
    kj = lax.broadcasted_iota(jnp.int32, (1, 3 * tq), 1)
    staged = []
    for g in range(4):
        sl = slice(LANES * g, LANES * (g + 1))
        kcat = jnp.concatenate([kp_ref[0, :, sl], kc_ref[0, :, sl], kn_ref[0, :, sl]], axis=0)
        vcat = jnp.concatenate([vp_ref[0, :, sl], vc_ref[0, :, sl], vn_ref[0, :, sl]], axis=0)
        bias = bias_ref[4 * g:4 * g + 4].reshape(4 * tq, 3 * tq)
        sink = jnp.concatenate([jnp.full((tq, 1), sink_ref[4 * g + a] * LOG2E, F32) for a in range(4)], axis=0)
        for u in range(nq):
            rows = slice(tq * u, tq * (u + 1))
            parts = []
            for a in range(4):
                blk = q_ref[0, rows, LANES * (2 * g + a // 2):LANES * (2 * g + a // 2 + 1)]
                keep = lo if a % 2 == 0 else jnp.logical_not(lo)
                parts.append(jnp.where(keep, blk, jnp.zeros_like(blk)))
            qs = jnp.concatenate(parts, axis=0)
            key_pos = (i * nq + u - 1) * tq + kj
            valid = jnp.logical_and(key_pos >= 0, key_pos < seq_len)
            s = _dot_nt(qs, kcat[tq * u:tq * (u + 3)]) + bias
            s = jnp.where(valid, s, -jnp.inf)
            m = jnp.maximum(jnp.max(_lane_chunk_reduce(jnp.maximum, s), axis=-1, keepdims=True), sink)
            staged.append((g, u, s, m, sink, vcat[tq * u:tq * (u + 3)]))
    for g, u, s, m, sink, v in staged:
        p = jnp.exp2(s - m)
        l = jnp.sum(_lane_chunk_reduce(jnp.add, p), axis=-1, keepdims=True) + jnp.exp2(sink - m)
        pv = _dot(p.astype(BF16), v) / l
        for j in range(2):
            blk = 2 * g + j
            o_ref[0, tq * u:tq * (u + 1), LANES * blk:LANES * (blk + 1)] = jnp.where(
                lo, pv[2 * j * tq:(2 * j + 1) * tq], pv[(2 * j + 1) * tq:(2 * j + 2) * tq]).astype(o_ref.dtype)


def _rel_bias_kernel(bucket_ref, band_ref, relt_ref, o_ref):
    bucket = bucket_ref[...]
    acc = jnp.zeros(o_ref.shape, F32)
    for r in range(REL_BUCKETS):
        acc = acc + jnp.where(bucket == r, relt_ref[:, r:r + 1], 0.0)
    o_ref[...] = jnp.where(band_ref[...] > 0, acc * LOG2E, -jnp.inf)


def _to_tiles(ref, val):
    rows = ref.shape[0] // TOKEN_TILE_ROWS
    for c in range(TOKEN_TILE_ROWS):
        ref[pl.ds(c, rows, stride=TOKEN_TILE_ROWS), :] = val[:, LANES * c:LANES * (c + 1)]


def _from_tiles(ref):
    rows = ref.shape[0] // TOKEN_TILE_ROWS
    return jnp.concatenate([ref[pl.ds(c, rows, stride=TOKEN_TILE_ROWS), :] for c in range(TOKEN_TILE_ROWS)],
                           axis=1)


def _post_kernel(*refs, n_parts):
    parts = refs[:n_parts]
    ws = refs[n_parts:2 * n_parts]
    (x_ref, gffn_ref, wrh_ref, wrl_ref, rb_ref, tri_ref,
     x1t_ref, bucket_ref, rank_ref, counts_ref, carry_ref) = refs[2 * n_parts:]
    acc = x_ref[...]
    for p_ref, w_ref in zip(parts, ws):
        acc = acc + _dot(p_ref[...], w_ref[...])
    _to_tiles(x1t_ref, acc)
    xn = _row_rmsnorm(acc, gffn_ref[...])

    xh, xl = _split_bf16(xn)
    wh, wl = wrh_ref[...], wrl_ref[...]
    logit = _dot_nt(wh, xh) + _dot_nt(wh, xl) + _dot_nt(wl, xh) + rb_ref[...]
    g = [logit[r:r + 1, :] for r in range(N_GROUPS)]
    gmax = jnp.maximum(jnp.maximum(g[0], g[1]), jnp.maximum(g[2], g[3]))
    gsum = sum(jnp.exp(gi - gmax) for gi in g)
    g_p = 1.0 / gsum
    gidx = jnp.where(g[0] == gmax, 0, jnp.where(g[1] == gmax, 1, jnp.where(g[2] == gmax, 2, 3)))
    e = []
    for j in range(EXPERTS_PER_GROUP):
        rows = [logit[N_GROUPS + EXPERTS_PER_GROUP * gg + j:N_GROUPS + EXPERTS_PER_GROUP * gg + j + 1, :]
                for gg in range(N_GROUPS)]
        e.append(jnp.where(gidx == 0, rows[0], jnp.where(gidx == 1, rows[1],
                                                         jnp.where(gidx == 2, rows[2], rows[3]))))
    emax = jnp.maximum(jnp.maximum(e[0], e[1]), jnp.maximum(e[2], e[3]))
    ex = [jnp.exp(ej - emax) for ej in e]
    esum = ex[0] + ex[1] + ex[2] + ex[3]
    pr = [exj / esum for exj in ex]
    p1 = jnp.maximum(jnp.maximum(pr[0], pr[1]), jnp.maximum(pr[2], pr[3]))
    i1 = jnp.where(pr[0] == p1, 0, jnp.where(pr[1] == p1, 1, jnp.where(pr[2] == p1, 2, 3)))
    rest = [jnp.where(i1 == j, -1.0, pr[j]) for j in range(EXPERTS_PER_GROUP)]
    p2 = jnp.maximum(jnp.maximum(rest[0], rest[1]), jnp.maximum(rest[2], rest[3]))
    i2 = jnp.where(rest[0] == p2, 0, jnp.where(rest[1] == p2, 1, jnp.where(rest[2] == p2, 2, 3)))
    lo_e = jnp.minimum(i1, i2)
    hi_e = jnp.maximum(i1, i2)
    pair = jnp.where(lo_e == 0, hi_e - 1, jnp.where(lo_e == 1, hi_e + 1, 5))
    bucket = gidx * PAIRS_PER_GROUP + pair
    bucket_ref[...] = bucket

    @pl.when(pl.program_id(0) == 0)
    def _():
        carry_ref[...] = jnp.zeros_like(carry_ref)

    onehot = (lax.broadcasted_iota(jnp.int32, (ROUTER_ROWS, bucket.shape[1]), 0) == bucket).astype(F32)
    before = _dot(onehot.astype(BF16), tri_ref[...]) + carry_ref[:, 0:1]
    rank_ref[...] = jnp.sum(onehot * before, axis=0, keepdims=True).astype(jnp.int32)
    carry_ref[...] = carry_ref[...] + jnp.sum(onehot, axis=1, keepdims=True)
    counts_ref[...] = carry_ref[...]


def _start_row_gather(idx_ref, table_hbm, dst, sem):
    def body(blk, carry):
        for j in range(GATHER_UNROLL):
            r = blk * GATHER_UNROLL + j
            src_row = pl.multiple_of(idx_ref[0, 0, r] * TOKEN_TILE_ROWS, TOKEN_TILE_ROWS)
            dst_row = pl.multiple_of(r * TOKEN_TILE_ROWS, TOKEN_TILE_ROWS)
            pltpu.make_async_copy(table_hbm.at[pl.ds(src_row, TOKEN_TILE_ROWS)],
                                  dst.at[pl.ds(dst_row, TOKEN_TILE_ROWS)], sem).start()
        return carry
    lax.fori_loop(0, dst.shape[0] // (TOKEN_TILE_ROWS * GATHER_UNROLL), body, 0)


def _wait_row_gather(table_hbm, dst, sem):
    pltpu.make_async_copy(table_hbm.at[pl.ds(0, dst.shape[0])], dst, sem).wait()


def _tile_gates(logit, ea, eb):
    lane = lax.broadcasted_iota(jnp.int32, (1, LANES), 1)
    pick = lambda idx: jnp.sum(jnp.where(lane == idx, logit, 0.0), axis=-1, keepdims=True)
    glog = jnp.where(lane < N_GROUPS, logit, -jnp.inf)
    gmax = jnp.max(glog, axis=-1, keepdims=True)
    gsum = jnp.sum(jnp.exp(glog - gmax), axis=-1, keepdims=True)
    g_p = jnp.exp(pick(ea // EXPERTS_PER_GROUP) - gmax) / gsum
    la, lb = pick(N_GROUPS + ea), pick(N_GROUPS + eb)
    top = jnp.maximum(la, lb)
    pa, pb = jnp.exp(la - top), jnp.exp(lb - top)
    return g_p * pa / (pa + pb), g_p * pb / (pa + pb)


def _moe_sparse_kernel(ea_ref, eb_ref, nt_ref, src_cur_ref, src_nxt_ref, gffn_ref, wr_ref, rb_ref, x1t_hbm,
                       wga_ref, wua_ref, wda_ref, wgb_ref, wub_ref, wdb_ref, o_ref, buf, sem):
    t = pl.program_id(0)
    nt = nt_ref[0]
    slot = t % 2

    @pl.when(t == 0)
    def _():
        _start_row_gather(src_cur_ref, x1t_hbm, buf.at[0], sem.at[0])

    @pl.when(t + 1 < nt)
    def _():
        _start_row_gather(src_nxt_ref, x1t_hbm, buf.at[1 - slot], sem.at[1 - slot])

    @pl.when(t < nt)
    def _():
        _wait_row_gather(x1t_hbm, buf.at[slot], sem.at[slot])
        x = _from_tiles(buf.at[slot])
        xn = _row_rmsnorm(x, gffn_ref[...]).astype(BF16)
        gate_a, gate_b = _tile_gates(_dot(xn, wr_ref[...]) + rb_ref[...], ea_ref[t], eb_ref[t])
        out = x
        for gate, (wg, wu, wd) in ((gate_a, (wga_ref, wua_ref, wda_ref)), (gate_b, (wgb_ref, wub_ref, wdb_ref))):
            hid = jax.nn.silu(_dot(xn, wg[0])) * _dot(xn, wu[0])
            out = out + gate * _dot(hid.astype(BF16), wd[0])
        _to_tiles(o_ref, out)

    @pl.when(t >= nt)
    def _():
        o_ref[...] = jnp.zeros_like(o_ref)


def _unpermute_kernel(pos_cur_ref, pos_nxt_ref, sorted_hbm, o_ref, buf, sem):
    t = pl.program_id(0)
    slot = t % 2

    @pl.when(t == 0)
    def _():
        _start_row_gather(pos_cur_ref, sorted_hbm, buf.at[0], sem.at[0])

    @pl.when(t + 1 < pl.num_programs(0))
    def _():
        _start_row_gather(pos_nxt_ref, sorted_hbm, buf.at[1 - slot], sem.at[1 - slot])

    _wait_row_gather(sorted_hbm, buf.at[slot], sem.at[slot])
    o_ref[...] = _from_tiles(buf.at[slot])


def _seg_mats(width, segments):
    m = np.zeros((width, LANES), np.float32)
    invlen = np.ones((1, LANES), np.float32)
    for c, (start, length) in enumerate(segments):
        m[start:start + length, c] = 1.0
        invlen[0, c] = 1.0 / length
    return jnp.asarray(m, BF16), jnp.asarray(np.concatenate([m.T, m.T], axis=0), BF16), jnp.asarray(invlen)


def _rope_tables(pos_list, seq_len):
    inv = ROPE_THETA ** (-jnp.arange(0, 2 * ROPE_HALF, 2, dtype=F32) / (2 * ROPE_HALF))
    cos_cols, sin_cols = [], []
    for pos in pos_list:
        if pos is None:
            cos_cols.append(jnp.ones((seq_len, 2 * ROPE_HALF), F32))
            sin_cols.append(jnp.zeros((seq_len, 2 * ROPE_HALF), F32))
        else:
            ang = pos.astype(F32)[:, None] * inv[None, :]
            c, s = jnp.cos(ang), jnp.sin(ang)
            cos_cols.append(jnp.concatenate([c, c], axis=1))
            sin_cols.append(jnp.concatenate([-s, s], axis=1))
    return jnp.concatenate(cos_cols, axis=1), jnp.concatenate(sin_cols, axis=1)


def _t5_bucket_np(rel):
    nb = REL_BUCKETS // 2
    max_exact = nb // 2
    ret = np.where(rel > 0, nb, 0)
    n = np.abs(rel)
    nf = np.maximum(n, 1).astype(np.float32)
    large = max_exact + (np.log(nf / np.float32(max_exact)) / np.float32(math.log(REL_MAX_DIST / max_exact))
                         * np.float32(nb - max_exact)).astype(np.int32)
    large = np.minimum(large, nb - 1)
    return ret + np.where(n < max_exact, n, large)


def _full(shape):
    nd = len(shape)
    return pl.BlockSpec(shape, lambda *_: (0,) * nd)


def _params(*sem):
    return pltpu.CompilerParams(dimension_semantics=sem, vmem_limit_bytes=VMEM_LIMIT)


def _prep_ab(x2, seq_len, gmix, w_in, qag, wq, kvag, wkv, qn_g, kn_g, qr_g, kr_g, bq_g, bk_g, tm=256):
    n = x2.shape[0]
    scale_a = (HEAD + 2 * ROPE_HALF) ** -0.5
    scale_b = HEAD ** -0.5
    zeros = lambda r, c: jnp.zeros((r, c), F32)
    d = w_in.shape[0]
    w0 = jnp.concatenate([w_in[:, 0:384], zeros(d, HEAD), w_in[:, 384:416], zeros(d, 32), w_in[:, 416:1184]],
                         axis=1).astype(BF16)
    wq_p = jnp.concatenate([wq.reshape(-1, 8, 96), jnp.zeros((wq.shape[0], 8, 32), F32)], axis=2)
    wq_p = wq_p.reshape(-1, 1024).astype(BF16)
    wkv_r = wkv.reshape(-1, 8, 128)
    wk_p = jnp.concatenate([wkv_r[:, :, :HEAD], jnp.zeros_like(wkv_r[:, :, :HEAD])], axis=2).reshape(-1, 1024)
    wv_p = jnp.concatenate([wkv_r[:, :, HEAD:], jnp.zeros_like(wkv_r[:, :, HEAD:])], axis=2).reshape(-1, 1024)
    wkv_p = jnp.concatenate([wk_p, wv_p], axis=1).astype(BF16)

    seg_q = [(LANES * h, HEAD) for h in range(8)] + [(LANES * h + HEAD, 32) for h in range(8)]
    gq = jnp.tile(jnp.concatenate([qn_g, qr_g, jnp.zeros((32,), F32)]), 8)[None, :] * (scale_a * LOG2E)
    seg_k = [(LANES * h, HEAD) for h in range(8)]
    gk = jnp.tile(jnp.concatenate([kn_g, jnp.zeros((HEAD,), F32)]), 8)[None, :]
    seg_r = [(HEAD, 32)]
    gr = jnp.concatenate([jnp.zeros((HEAD,), F32), kr_g, jnp.zeros((32,), F32)])[None, :]
    seg_bq = [(HEAD * h, HEAD) for h in range(8)]
    gbq = jnp.tile(bq_g, 8)[None, :] * (scale_b * LOG2E)
    seg_bk = [(HEAD * h, HEAD) for h in range(2)]
    gbk = jnp.tile(bk_g, 2)[None, :]

    pos = jnp.arange(seq_len)
    cosa, sina = _rope_tables([None, None, pos, None], seq_len)
    row, col = pos // GRID_W, pos % GRID_W
    cosb, sinb = _rope_tables([row, col, row, col], seq_len)

    consts = [gmix[None, :], w0, qag[None, :], wq_p, kvag[None, :], wkv_p,
              *_seg_mats(1024, seg_q), gq, *_seg_mats(1024, seg_k), gk, *_seg_mats(LANES, seg_r), gr,
              *_seg_mats(512, seg_bq), gbq, *_seg_mats(LANES, seg_bk), gbk]
    nsb = seq_len // tm
    rope_spec = pl.BlockSpec((tm, LANES), lambda i: (i % nsb, 0))
    row_spec = lambda w: pl.BlockSpec((tm, w), lambda i: (i, 0))
    out_widths = (1024, 1024, 1024, 512, 256, 256)
    return pl.pallas_call(
        _prep_ab_kernel,
        grid=(n // tm,),
        in_specs=[row_spec(x2.shape[1])] + [_full(c.shape) for c in consts] + [rope_spec] * 4,
        out_specs=[row_spec(w) for w in out_widths],
        out_shape=[jax.ShapeDtypeStruct((n, w), BF16) for w in out_widths],
        compiler_params=_params("parallel"),
        name="prep_ab",
    )(x2, *consts, cosa, sina, cosb, sinb)


def _prep_c(x2, gmix, w_in, q_g, k_g, tm=256):
    n = x2.shape[0]
    seg_q = [(HEAD * h, HEAD) for h in range(16)]
    gq = jnp.tile(q_g, 16)[None, :] * (HEAD ** -0.5 * LOG2E)
    seg_k = [(HEAD * h, HEAD) for h in range(4)]
    gk = jnp.tile(k_g, 4)[None, :]
    consts = [gmix[None, :], w_in.astype(BF16), *_seg_mats(1024, seg_q), gq, *_seg_mats(256, seg_k), gk]
    row_spec = lambda w: pl.BlockSpec((tm, w), lambda i: (i, 0))
    out_widths = (1024, 512, 512)
    return pl.pallas_call(
        _prep_c_kernel,
        grid=(n // tm,),
        in_specs=[row_spec(x2.shape[1])] + [_full(c.shape) for c in consts],
        out_specs=[row_spec(w) for w in out_widths],
        out_shape=[jax.ShapeDtypeStruct((n, w), BF16) for w in out_widths],
        compiler_params=_params("parallel"),
        name="prep_c",
    )(x2, *consts)


def _mla_attn(qa, ka, va, tq=512):
    b, s, _ = qa.shape
    return pl.pallas_call(
        _mla_attn_kernel,
        grid=(b, 4, s // tq),
        in_specs=[pl.BlockSpec((1, tq, 2 * LANES), lambda bi, hp, qi: (bi, qi, hp)),
                  pl.BlockSpec((1, s, 2 * LANES), lambda bi, hp, qi: (bi, 0, hp)),
                  pl.BlockSpec((1, s, 2 * LANES), lambda bi, hp, qi: (bi, 0, hp))],
        out_specs=pl.BlockSpec((1, tq, LANES), lambda bi, hp, qi: (bi, qi, hp)),
        out_shape=jax.ShapeDtypeStruct((b, s, 512), BF16),
        scratch_shapes=[pltpu.VMEM((tq, s), F32), pltpu.VMEM((tq, s), F32)],
        compiler_params=_params("parallel", "parallel", "parallel"),
        name="mla_attn",
    )(qa, ka, va)


def _gqa_attn(qb, kb, vb, tq=256):
    b, s, _ = qb.shape
    return pl.pallas_call(
        _gqa_attn_kernel,
        grid=(b, 2, s // tq),
        in_specs=[pl.BlockSpec((1, tq, 2 * LANES), lambda bi, g, qi: (bi, qi, g)),
                  pl.BlockSpec((1, s, LANES), lambda bi, g, qi: (bi, 0, g)),
                  pl.BlockSpec((1, s, LANES), lambda bi, g, qi: (bi, 0, g))],
        out_specs=pl.BlockSpec((1, tq, 2 * LANES), lambda bi, g, qi: (bi, qi, g)),
        out_shape=jax.ShapeDtypeStruct((b, s, 512), BF16),
        scratch_shapes=[pltpu.VMEM((2 * tq, s), F32), pltpu.VMEM((2 * tq, s), F32)],
        compiler_params=_params("parallel", "parallel", "parallel"),
        name="gqa_attn",
    )(qb, kb, vb)


def _rel_bias_table(rel_bias):
    span = Q_BLOCK + 2 * WINDOW
    rel = np.arange(span)[None, :] - WINDOW - np.arange(Q_BLOCK)[:, None]
    bucket = _t5_bucket_np(rel).astype(np.int32).reshape(1, -1)
    band = (np.abs(rel) <= WINDOW).astype(np.int32).reshape(1, -1)
    heads = rel_bias.shape[1]
    cols = bucket.shape[1]
    chunk = cols // 8
    table = pl.pallas_call(
        _rel_bias_kernel,
        grid=(8,),
        in_specs=[pl.BlockSpec((1, chunk), lambda i: (0, i)), pl.BlockSpec((1, chunk), lambda i: (0, i)),
                  _full((heads, REL_BUCKETS))],
        out_specs=pl.BlockSpec((heads, chunk), lambda i: (0, i)),
        out_shape=jax.ShapeDtypeStruct((heads, cols), F32),
        compiler_params=_params("parallel"),
        name="rel_bias",
    )(jnp.asarray(bucket), jnp.asarray(band), rel_bias.T)
    return table.reshape(heads, Q_BLOCK, span)


def _win_attn(qc, kc, vc, sink, bias, nq=2):
    b, s, _ = qc.shape
    tq = Q_BLOCK
    nb = s // tq
    prev = lambda bi, i: (bi, jnp.maximum(i * nq - 1, 0), 0)
    cur = lambda bi, i: (bi, i, 0)
    nxt = lambda bi, i: (bi, jnp.minimum((i + 1) * nq, nb - 1), 0)
    edge_spec = lambda im: pl.BlockSpec((1, tq, 4 * LANES), im)
    cur_spec = pl.BlockSpec((1, nq * tq, 4 * LANES), cur)
    return pl.pallas_call(
        functools.partial(_win_attn_kernel, seq_len=s),
        grid=(b, nb // nq),
        in_specs=[pl.BlockSpec(memory_space=pltpu.SMEM),
                  pl.BlockSpec((1, nq * tq, 8 * LANES), cur),
                  edge_spec(prev), cur_spec, edge_spec(nxt),
                  edge_spec(prev), cur_spec, edge_spec(nxt),
                  _full(bias.shape)],
        out_specs=pl.BlockSpec((1, nq * tq, 8 * LANES), cur),
        out_shape=jax.ShapeDtypeStruct((b, s, 1024), BF16),
        compiler_params=_params("parallel", "parallel"),
        name="win_attn",
    )(sink, qc, kc, kc, kc, vc, vc, vc, bias)


def _post(parts, ws, x2, gffn, w_group, b_group, w_router, b_router, tm=256):
    n, d = x2.shape
    wr = jnp.concatenate([w_group.T, w_router.T, jnp.zeros((ROUTER_ROWS - N_GROUPS - N_EXPERTS, d), F32)], axis=0)
    wrh = wr.astype(BF16)
    wrl = (wr - wrh.astype(F32)).astype(BF16)
    rb = jnp.concatenate([b_group, b_router, jnp.zeros((ROUTER_ROWS - N_GROUPS - N_EXPERTS,), F32)])[:, None]
    row_spec = lambda w: pl.BlockSpec((tm, w), lambda i: (i, 0))
    lane_spec = lambda r: pl.BlockSpec((r, tm), lambda i: (0, i))
    ws = [w.astype(BF16) for w in ws]
    tri = jnp.asarray(np.triu(np.ones((tm, tm), np.float32), 1), BF16)
    return pl.pallas_call(
        functools.partial(_post_kernel, n_parts=len(parts)),
        grid=(n // tm,),
        in_specs=[row_spec(p.shape[1]) for p in parts] + [_full(w.shape) for w in ws]
                 + [row_spec(d), _full((1, d)), _full(wrh.shape), _full(wrl.shape), _full(rb.shape),
                    _full(tri.shape)],
        out_specs=[pl.BlockSpec((tm * TOKEN_TILE_ROWS, LANES), lambda i: (i, 0)),
                   lane_spec(1), lane_spec(1), _full((ROUTER_ROWS, LANES))],
        out_shape=[jax.ShapeDtypeStruct((n * TOKEN_TILE_ROWS, LANES), F32),
                   jax.ShapeDtypeStruct((1, n), jnp.int32), jax.ShapeDtypeStruct((1, n), jnp.int32),
                   jax.ShapeDtypeStruct((ROUTER_ROWS, LANES), F32)],
        scratch_shapes=[pltpu.VMEM((ROUTER_ROWS, LANES), F32)],
        compiler_params=_params("arbitrary"),
        name="post",
    )(*parts, *ws, x2, gffn[None, :], wrh, wrl, rb, tri)


def _route(bucket, rank, counts, tmm):
    n = bucket.shape[1]
    n_tiles_max = n // tmm + N_BUCKETS
    n_slots = n_tiles_max * tmm
    bucket, rank = bucket[0], rank[0]
    cnt = counts[:N_BUCKETS, 0].astype(jnp.int32)
    padded = (cnt + tmm - 1) // tmm * tmm
    ends = jnp.cumsum(padded)
    starts = ends - padded
    pos = starts[bucket] + rank
    n_tiles = ends[-1] // tmm
    tile = jnp.minimum(jnp.arange(n_tiles_max, dtype=jnp.int32), n_tiles - 1)
    tile_bucket = jnp.sum((tile * tmm)[:, None] >= ends[None, :], axis=1).astype(jnp.int32)
    grp, pair = tile_bucket // PAIRS_PER_GROUP, tile_bucket % PAIRS_PER_GROUP
    tile_ea = grp * EXPERTS_PER_GROUP + jnp.asarray(PAIR_LO, jnp.int32)[pair]
    tile_eb = grp * EXPERTS_PER_GROUP + jnp.asarray(PAIR_HI, jnp.int32)[pair]
    token = jnp.arange(n, dtype=jnp.int32)
    src = (jnp.arange(n_slots + tmm, dtype=jnp.int32) % n).at[pos].set(token)
    return pos, src.reshape(n_tiles_max + 1, 1, tmm), tile_ea, tile_eb, n_tiles.reshape(1)


def _moe_sparse(x1t, gffn, w_group, b_group, w_router, b_router, src, tile_ea, tile_eb, n_tiles,
                w_gate, w_up, w_down, tmm):
    n_tiles_max = src.shape[0] - 1
    ne, d, de = w_gate.shape
    idx_spec = lambda off: pl.BlockSpec((1, 1, tmm), lambda t, ea, eb, nt: (t + off, 0, 0),
                                        memory_space=pltpu.SMEM)
    up_spec = lambda which: pl.BlockSpec((1, d, de), lambda t, ea, eb, nt: ((ea, eb)[which][t], 0, 0))
    down_spec = lambda which: pl.BlockSpec((1, de, d), lambda t, ea, eb, nt: ((ea, eb)[which][t], 0, 0))
    wg, wu, wd = w_gate.astype(BF16), w_up.astype(BF16), w_down.astype(BF16)
    pad = LANES - N_GROUPS - N_EXPERTS
    wr = jnp.concatenate([w_group, w_router, jnp.zeros((d, pad), F32)], axis=1).astype(BF16)
    rb = jnp.concatenate([b_group, b_router, jnp.zeros((pad,), F32)])[None, :]
    return pl.pallas_call(
        _moe_sparse_kernel,
        grid_spec=pltpu.PrefetchScalarGridSpec(
            num_scalar_prefetch=3,
            grid=(n_tiles_max,),
            in_specs=[idx_spec(0), idx_spec(1),
                      pl.BlockSpec((1, d), lambda t, ea, eb, nt: (0, 0)),
                      pl.BlockSpec((d, LANES), lambda t, ea, eb, nt: (0, 0)),
                      pl.BlockSpec((1, LANES), lambda t, ea, eb, nt: (0, 0)),
                      pl.BlockSpec(memory_space=pl.ANY),
                      up_spec(0), up_spec(0), down_spec(0), up_spec(1), up_spec(1), down_spec(1)],
            out_specs=pl.BlockSpec((tmm * TOKEN_TILE_ROWS, LANES), lambda t, ea, eb, nt: (t, 0)),
            scratch_shapes=[pltpu.VMEM((2, tmm * TOKEN_TILE_ROWS, LANES), F32), pltpu.SemaphoreType.DMA((2,))]),
        out_shape=jax.ShapeDtypeStruct((n_tiles_max * tmm * TOKEN_TILE_ROWS, LANES), F32),
        compiler_params=_params("arbitrary"),
        name="moe_sparse",
    )(tile_ea, tile_eb, n_tiles, src, src, gffn[None, :], wr, rb, x1t, wg, wu, wd, wg, wu, wd)


def _unpermute(sorted_rows, pos, tm=256):
    n = pos.shape[0]
    d = TOKEN_TILE_ROWS * LANES
    pos2 = jnp.concatenate([pos, jnp.zeros((tm,), jnp.int32)]).reshape(n // tm + 1, 1, tm)
    idx_spec = lambda off: pl.BlockSpec((1, 1, tm), lambda t: (t + off, 0, 0), memory_space=pltpu.SMEM)
    return pl.pallas_call(
        _unpermute_kernel,
        grid=(n // tm,),
        in_specs=[idx_spec(0), idx_spec(1), pl.BlockSpec(memory_space=pl.ANY)],
        out_specs=pl.BlockSpec((tm, d), lambda t: (t, 0)),
        out_shape=jax.ShapeDtypeStruct((n, d), F32),
        scratch_shapes=[pltpu.VMEM((2, tm * TOKEN_TILE_ROWS, LANES), F32), pltpu.SemaphoreType.DMA((2,))],
        compiler_params=_params("arbitrary"),
        name="unpermute",
    )(pos2, pos2, sorted_rows)


def kernel(x, mix_norm, ffn_norm, w_in_ab, mla_q_a_norm, mla_w_q_up, mla_kv_a_norm, mla_w_kv_up, mla_qn_gain, mla_kn_gain, mla_qr_gain, mla_kr_gain, gqa_q_gain, gqa_k_gain, w_out_ab, w_in_c, win_q_gain, win_k_gain, win_sink, w_out_c, rel_bias, moe_w_group, moe_b_group, moe_w_router, moe_b_router, moe_w_gate, moe_w_up, moe_w_down):
    b, s, d = x.shape
    n = b * s
    depth = mix_norm.shape[0]
    x2 = x.reshape(n, d)
    bias = None
    for layer in range(depth):
        i = layer // 2
        if layer % 2 == 0:
            qa, ka, va, qb, kb, vb = _prep_ab(
                x2, s, mix_norm[layer], w_in_ab[i], mla_q_a_norm[i], mla_w_q_up[i], mla_kv_a_norm[i],
                mla_w_kv_up[i], mla_qn_gain[i], mla_kn_gain[i], mla_qr_gain[i], mla_kr_gain[i],
                gqa_q_gain[i], gqa_k_gain[i])
            r3 = lambda t: t.reshape(b, s, t.shape[1])
            out_a = _mla_attn(r3(qa), r3(ka), r3(va)).reshape(n, 512)
            out_b = _gqa_attn(r3(qb), r3(kb), r3(vb)).reshape(n, 512)
            parts, ws = [out_a, out_b], [w_out_ab[i][:512], w_out_ab[i][512:]]
        else:
            if bias is None:
                bias = _rel_bias_table(rel_bias)
            qc, kc, vc = _prep_c(x2, mix_norm[layer], w_in_c[i], win_q_gain[i], win_k_gain[i])
            r3 = lambda t: t.reshape(b, s, t.shape[1])
            out_c = _win_attn(r3(qc), r3(kc), r3(vc), win_sink[i], bias).reshape(n, 1024)
            parts, ws = [out_c], [w_out_c[i]]
        router = (moe_w_group[layer], moe_b_group[layer], moe_w_router[layer], moe_b_router[layer])
        x1t, bucket, rank, counts = _post(parts, ws, x2, ffn_norm[layer], *router)
        pos, src, tile_ea, tile_eb, n_tiles = _route(bucket, rank, counts, MOE_TILE)
        y_sorted = _moe_sparse(x1t, ffn_norm[layer], *router, src, tile_ea, tile_eb, n_tiles,
                               moe_w_gate[layer], moe_w_up[layer], moe_w_down[layer], MOE_TILE)
        x2 = _unpermute(y_sorted, pos)
    return x2.reshape(b, s, d)
```

```python
import functools
import math

import numpy as np
import jax
import jax.numpy as jnp
from jax import lax
from jax.experimental import pallas as pl
from jax.experimental.pallas import tpu as pltpu

F32 = jnp.float32
BF16 = jnp.bfloat16

EPS = 1e-6
ROPE_THETA = 10000.0
LANES = 128
HEAD = 64
ROPE_HALF = 16
GRID_W = 64
WINDOW = 128
Q_BLOCK = 128
REL_BUCKETS = 32
REL_MAX_DIST = 128
N_GROUPS = 4
EXPERTS_PER_GROUP = 4
N_EXPERTS = 16
PAIRS_PER_GROUP = 6
N_BUCKETS = N_GROUPS * PAIRS_PER_GROUP
PAIR_LO = (0, 0, 0, 1, 1, 2)
PAIR_HI = (1, 2, 3, 2, 3, 3)
TOKEN_TILE_ROWS = 8
KEY_CHUNK = 512
LOG2E = math.log2(math.e)
PREP_SUB = 256
WIN_ROW_TILE = 32
GATHER_UNROLL = 8
MOE_TILE = 256
ROUTER_ROWS = 32
VMEM_LIMIT = 56 * 1024 * 1024

_NT = (((1,), (1,)), ((), ()))


def _dot(a, b):
    return jnp.dot(a, b, preferred_element_type=F32)


def _dot_nt(a, b):
    return lax.dot_general(a, b, _NT, preferred_element_type=F32)


def _split_bf16(a):
    hi = a.astype(BF16)
    lo = (a - hi.astype(F32)).astype(BF16)
    return hi, lo


def _row_rmsnorm(t, gain):
    return t * lax.rsqrt(jnp.mean(t * t, axis=-1, keepdims=True) + EPS) * gain


def _seg_rmsnorm(t, mseg, msegt, invlen, gain):
    sums = _dot((t * t).astype(BF16), mseg)
    inv = lax.rsqrt(sums * invlen + EPS)
    ihi, ilo = _split_bf16(inv)
    scale = _dot(jnp.concatenate([ihi, ilo], axis=1), msegt)
    return t * scale * gain


def _rope128(t, cos, sin_signed, first_half):
    up = pltpu.roll(t, LANES - ROPE_HALF, 1)
    dn = pltpu.roll(t, ROPE_HALF, 1)
    return t * cos + jnp.where(first_half, up, dn) * sin_signed


def _dup_halves(blk, lo):
    sw = pltpu.roll(blk, HEAD, 1)
    return jnp.where(lo, blk, sw), jnp.where(lo, sw, blk)


def _lane_masks():
    lane = lax.broadcasted_iota(jnp.int32, (1, LANES), 1)
    return lane < HEAD, (lane % (2 * ROPE_HALF)) < ROPE_HALF


def _by_sub_tiles(rows_fn, refs, n_const):
    row_refs = (refs[0],) + tuple(refs[1 + n_const:])
    for h in range(refs[0].shape[0] // PREP_SUB):
        view = [r.at[pl.ds(PREP_SUB * h, PREP_SUB)] for r in row_refs]
        rows_fn(view[0], *refs[1:1 + n_const], *view[1:])


def _prep_ab_kernel(*refs):
    _by_sub_tiles(_prep_ab_rows, refs, 26)


def _prep_c_kernel(*refs):
    _by_sub_tiles(_prep_c_rows, refs, 10)


def _prep_ab_rows(x_ref, gmix_ref, w0_ref, qag_ref, wq_ref, kvag_ref, wkv_ref,
                    mq_ref, mqt_ref, ilq_ref, gq_ref,
                    mk_ref, mkt_ref, ilk_ref, gk_ref,
                    mr_ref, mrt_ref, ilr_ref, gr_ref,
                    mbq_ref, mbqt_ref, ilbq_ref, gbq_ref,
                    mbk_ref, mbkt_ref, ilbk_ref, gbk_ref,
                    cosa_ref, sina_ref, cosb_ref, sinb_ref,
                    qa_ref, ka_ref, va_ref, qb_ref, kb_ref, vb_ref):
    lo, first_half = _lane_masks()
    h = _row_rmsnorm(x_ref[...], gmix_ref[...]).astype(BF16)
    proj = _dot(h, w0_ref[...])
    cosa, sina = cosa_ref[...], sina_ref[...]
    cosb, sinb = cosb_ref[...], sinb_ref[...]

    qn = _row_rmsnorm(proj[:, 0:256], qag_ref[...]).astype(BF16)
    q = _seg_rmsnorm(_dot(qn, wq_ref[...]), mq_ref[...], mqt_ref[...], ilq_ref[...], gq_ref[...])
    for hd in range(8):
        sl = slice(LANES * hd, LANES * (hd + 1))
        qa_ref[:, sl] = _rope128(q[:, sl], cosa, sina, first_half).astype(BF16)

    kvn = _row_rmsnorm(proj[:, 256:384], kvag_ref[...]).astype(BF16)
    kv = _dot(kvn, wkv_ref[...])
    ones_hi = jnp.where(lo, 0.0, 1.0)
    kn = _seg_rmsnorm(kv[:, 0:1024], mk_ref[...], mkt_ref[...], ilk_ref[...], gk_ref[...])
    kr = _seg_rmsnorm(proj[:, 384:512], mr_ref[...], mrt_ref[...], ilr_ref[...], gr_ref[...])
    kr = _rope128(kr, cosa, sina, first_half)
    for hd in range(8):
        sl = slice(LANES * hd, LANES * (hd + 1))
        ka_ref[:, sl] = (kn[:, sl] + kr).astype(BF16)
        va_ref[:, sl] = (kv[:, 1024 + LANES * hd:1024 + LANES * (hd + 1)] + ones_hi).astype(BF16)

    bq = _seg_rmsnorm(proj[:, 512:1024], mbq_ref[...], mbqt_ref[...], ilbq_ref[...], gbq_ref[...])
    for blk in range(4):
        sl = slice(LANES * blk, LANES * (blk + 1))
        qb_ref[:, sl] = _rope128(bq[:, sl], cosb, sinb, first_half).astype(BF16)
    bk = _seg_rmsnorm(proj[:, 1024:1152], mbk_ref[...], mbkt_ref[...], ilbk_ref[...], gbk_ref[...])
    bk = _rope128(bk, cosb, sinb, first_half)
    k0, k1 = _dup_halves(bk, lo)
    kb_ref[:, 0:LANES] = k0.astype(BF16)
    kb_ref[:, LANES:2 * LANES] = k1.astype(BF16)
    bv = proj[:, 1152:1280]
    vb_ref[:, 0:LANES] = jnp.where(lo, bv, 1.0).astype(BF16)
    vb_ref[:, LANES:2 * LANES] = jnp.where(lo, pltpu.roll(bv, HEAD, 1), 1.0).astype(BF16)


def _prep_c_rows(x_ref, gmix_ref, w_ref,
                   mq_ref, mqt_ref, ilq_ref, gq_ref,
                   mk_ref, mkt_ref, ilk_ref, gk_ref,
                   qc_ref, kc_ref, vc_ref):
    lo, _ = _lane_masks()
    h = _row_rmsnorm(x_ref[...], gmix_ref[...]).astype(BF16)
    proj = _dot(h, w_ref[...])
    q = _seg_rmsnorm(proj[:, 0:1024], mq_ref[...], mqt_ref[...], ilq_ref[...], gq_ref[...])
    qc_ref[...] = q.astype(BF16)
    k = _seg_rmsnorm(proj[:, 1024:1280], mk_ref[...], mkt_ref[...], ilk_ref[...], gk_ref[...])
    for blk in range(2):
        sl = slice(LANES * blk, LANES * (blk + 1))
        k0, k1 = _dup_halves(k[:, sl], lo)
        kc_ref[:, 2 * blk * LANES:(2 * blk + 1) * LANES] = k0.astype(BF16)
        kc_ref[:, (2 * blk + 1) * LANES:(2 * blk + 2) * LANES] = k1.astype(BF16)
        v = proj[:, 1280 + LANES * blk:1280 + LANES * (blk + 1)]
        vc_ref[:, 2 * blk * LANES:(2 * blk + 1) * LANES] = jnp.where(lo, v, 1.0).astype(BF16)
        vc_ref[:, (2 * blk + 1) * LANES:(2 * blk + 2) * LANES] = jnp.where(
            lo, pltpu.roll(v, HEAD, 1), 1.0).astype(BF16)


def _softmax_pv(s, v):
    m = jnp.max(s, axis=-1, keepdims=True)
    p = jnp.exp(s - m)
    l = jnp.sum(p, axis=-1, keepdims=True)
    return _dot(p.astype(BF16), v) / l


def _lane_chunk_reduce(op, t):
    out = t[:, 0:LANES]
    for j in range(1, t.shape[1] // LANES):
        out = op(out, t[:, LANES * j:LANES * (j + 1)])
    return out


def _scores_phase(q, k_ref, lanes, s_buf):
    m_part = None
    for c in range(k_ref.shape[1] // KEY_CHUNK):
        ks = slice(KEY_CHUNK * c, KEY_CHUNK * (c + 1))
        s_c = _dot_nt(q, k_ref[0, ks, lanes])
        s_buf[:, ks] = s_c
        mc = _lane_chunk_reduce(jnp.maximum, s_c)
        m_part = mc if m_part is None else jnp.maximum(m_part, mc)
    return jnp.max(m_part, axis=-1, keepdims=True)


def _pv_phase(s_buf, m, v_ref, lanes):
    lo, _ = _lane_masks()
    acc = None
    for c in range(v_ref.shape[1] // KEY_CHUNK):
        ks = slice(KEY_CHUNK * c, KEY_CHUNK * (c + 1))
        p = jnp.exp2(s_buf[:, ks] - m)
        pv = _dot(p.astype(BF16), v_ref[0, ks, lanes])
        acc = pv if acc is None else acc + pv
    return acc / jnp.where(lo, pltpu.roll(acc, HEAD, 1), 1.0)


def _pair_heads(o_even, o_odd):
    lo, _ = _lane_masks()
    return jnp.where(lo, o_even, pltpu.roll(o_odd, HEAD, 1))


def _mla_attn_kernel(q_ref, k_ref, v_ref, o_ref, s0_ref, s1_ref):
    first, second = slice(0, LANES), slice(LANES, 2 * LANES)
    m0 = _scores_phase(q_ref[0, :, first], k_ref, first, s0_ref)
    m1 = _scores_phase(q_ref[0, :, second], k_ref, second, s1_ref)
    o0 = _pv_phase(s0_ref, m0, v_ref, first)
    o1 = _pv_phase(s1_ref, m1, v_ref, second)
    o_ref[0] = _pair_heads(o0, o1).astype(o_ref.dtype)


def _gqa_attn_kernel(q_ref, k_ref, v_ref, o_ref, s0_ref, s1_ref):
    lo, _ = _lane_masks()
    tq = q_ref.shape[1]
    hi = jnp.logical_not(lo)
    ms = []
    for j, s_ref in enumerate((s0_ref, s1_ref)):
        blk = q_ref[0, :, LANES * j:LANES * (j + 1)]
        zero = jnp.zeros_like(blk)
        qs = jnp.concatenate([jnp.where(lo, blk, zero), jnp.where(hi, blk, zero)], axis=0)
        ms.append(_scores_phase(qs, k_ref, slice(0, LANES), s_ref))
    for j, s_ref in enumerate((s0_ref, s1_ref)):
        o = _pv_phase(s_ref, ms[j], v_ref, slice(0, LANES))
        o_ref[0, :, LANES * j:LANES * (j + 1)] = _pair_heads(o[0:tq], o[tq:2 * tq]).astype(o_ref.dtype)


def _win_attn_kernel(sink_ref, q_ref, kp_ref, kc_ref, kn_ref, vp_ref, vc_ref, vn_ref, bias_ref, o_ref,
                     s_scr, p_scr, m_scr, *, seq_len):
    lo, _ = _lane_masks()
    tq = Q_BLOCK
    nq = q_ref.shape[1] // tq
    i = pl.program_id(1)
    kj = lax.broadcasted_iota(jnp.int32, (1, 3 * tq), 1)
    units = [(g, u) for g in range(4) for u in range(nq)]
    for idx, (g, u) in enumerate(units):
        sl = slice(LANES * g, LANES * (g + 1))
        kcat = jnp.concatenate([kp_ref[0, :, sl], kc_ref[0, :, sl], kn_ref[0, :, sl]], axis=0)
        parts = []
        for a in range(4):
            blk = q_ref[0, tq * u:tq * (u + 1), LANES * (2 * g + a // 2):LANES * (2 * g + a // 2 + 1)]
            keep = lo if a % 2 == 0 else jnp.logical_not(lo)
            parts.append(jnp.where(keep, blk, jnp.zeros_like(blk)))
        s_scr[idx] = _dot_nt(jnp.concatenate(parts, axis=0), kcat[tq * u:tq * (u + 3)])
    for idx, (g, u) in enumerate(units):
        key_pos = (i * nq + u - 1) * tq + kj
        valid = jnp.logical_and(key_pos >= 0, key_pos < seq_len)
        for rt in range(4 * tq // WIN_ROW_TILE):
            a, r0 = divmod(rt * WIN_ROW_TILE, tq)
            rows = slice(rt * WIN_ROW_TILE, (rt + 1) * WIN_ROW_TILE)
            s = s_scr[idx, rows, :] + bias_ref[4 * g + a, r0:r0 + WIN_ROW_TILE, :]
            s = jnp.where(valid, s, -jnp.inf)
            m = jnp.maximum(jnp.max(_lane_chunk_reduce(jnp.maximum, s), axis=-1, keepdims=True),
                            sink_ref[4 * g + a] * LOG2E)
            p_scr[idx, rows, :] = jnp.exp2(s - m).astype(BF16)
            m_scr[idx, rows, :] = jnp.broadcast_to(m, (WIN_ROW_TILE, LANES))
    for idx, (g, u) in enumerate(units):
        sl = slice(LANES * g, LANES * (g + 1))
        vcat = jnp.concatenate([vp_ref[0, :, sl], vc_ref[0, :, sl], vn_ref[0, :, sl]], axis=0)
        acc = _dot(p_scr[idx], vcat[tq * u:tq * (u + 3)])
        outs = []
        for a in range(4):
            rows = slice(tq * a, tq * (a + 1))
            sink_term = jnp.exp2(sink_ref[4 * g + a] * LOG2E - m_scr[idx, rows, :])
            den = jnp.where(lo, pltpu.roll(acc[rows], HEAD, 1) + sink_term, 1.0)
            outs.append(acc[rows] / den)
        for j in range(2):
            blk = 2 * g + j
            o_ref[0, tq * u:tq * (u + 1), LANES * blk:LANES * (blk + 1)] = _pair_heads(
                outs[2 * j], outs[2 * j + 1]).astype(o_ref.dtype)


def _rel_bias_kernel(bucket_ref, band_ref, relt_ref, o_ref):
    bucket = bucket_ref[...]
    acc = jnp.zeros(o_ref.shape, F32)
    for r in range(REL_BUCKETS):
        acc = acc + jnp.where(bucket == r, relt_ref[:, r:r + 1], 0.0)
    o_ref[...] = jnp.where(band_ref[...] > 0, acc * LOG2E, -jnp.inf)


def _to_tiles(ref, val, first_token=0):
    for c in range(TOKEN_TILE_ROWS):
        ref[pl.ds(first_token * TOKEN_TILE_ROWS + c, val.shape[0], stride=TOKEN_TILE_ROWS), :] = (
            val[:, LANES * c:LANES * (c + 1)])


def _from_tiles(ref):
    rows = ref.shape[0] // TOKEN_TILE_ROWS
    return jnp.concatenate([ref[pl.ds(c, rows, stride=TOKEN_TILE_ROWS), :] for c in range(TOKEN_TILE_ROWS)],
                           axis=1)


def _route_bucket(xn, wh, wl, rb):
    xh, xl = _split_bf16(xn)
    logit = _dot_nt(wh, xh) + _dot_nt(wh, xl) + _dot_nt(wl, xh) + rb
    g = [logit[r:r + 1, :] for r in range(N_GROUPS)]
    gmax = jnp.maximum(jnp.maximum(g[0], g[1]), jnp.maximum(g[2], g[3]))
    gidx = jnp.where(g[0] == gmax, 0, jnp.where(g[1] == gmax, 1, jnp.where(g[2] == gmax, 2, 3)))
    e = []
    for j in range(EXPERTS_PER_GROUP):
        rows = [logit[N_GROUPS + EXPERTS_PER_GROUP * gg + j:N_GROUPS + EXPERTS_PER_GROUP * gg + j + 1, :]
                for gg in range(N_GROUPS)]
        e.append(jnp.where(gidx == 0, rows[0], jnp.where(gidx == 1, rows[1],
                                                         jnp.where(gidx == 2, rows[2], rows[3]))))
    emax = jnp.maximum(jnp.maximum(e[0], e[1]), jnp.maximum(e[2], e[3]))
    ex = [jnp.exp(ej - emax) for ej in e]
    esum = ex[0] + ex[1] + ex[2] + ex[3]
    pr = [exj / esum for exj in ex]
    p1 = jnp.maximum(jnp.maximum(pr[0], pr[1]), jnp.maximum(pr[2], pr[3]))
    i1 = jnp.where(pr[0] == p1, 0, jnp.where(pr[1] == p1, 1, jnp.where(pr[2] == p1, 2, 3)))
    rest = [jnp.where(i1 == j, -1.0, pr[j]) for j in range(EXPERTS_PER_GROUP)]
    p2 = jnp.maximum(jnp.maximum(rest[0], rest[1]), jnp.maximum(rest[2], rest[3]))
    i2 = jnp.where(rest[0] == p2, 0, jnp.where(rest[1] == p2, 1, jnp.where(rest[2] == p2, 2, 3)))
    lo_e = jnp.minimum(i1, i2)
    hi_e = jnp.maximum(i1, i2)
    pair = jnp.where(lo_e == 0, hi_e - 1, jnp.where(lo_e == 1, hi_e + 1, 5))
    return gidx * PAIRS_PER_GROUP + pair


def _post_kernel(*refs, n_parts):
    parts = refs[:n_parts]
    ws = refs[n_parts:2 * n_parts]
    (x_ref, gffn_ref, wrh_ref, wrl_ref, rb_ref, tri_ref,
     x1t_ref, bucket_ref, rank_ref, counts_ref, carry_ref) = refs[2 * n_parts:]

    @pl.when(pl.program_id(0) == 0)
    def _():
        carry_ref[...] = jnp.zeros_like(carry_ref)

    sub = tri_ref.shape[0]
    buckets = []
    for h in range(x_ref.shape[0] // sub):
        rows = slice(sub * h, sub * (h + 1))
        acc = x_ref[rows, :]
        for p_ref, w_ref in zip(parts, ws):
            acc = acc + _dot(p_ref[rows, :], w_ref[...])
        _to_tiles(x1t_ref, acc, sub * h)
        bucket = _route_bucket(_row_rmsnorm(acc, gffn_ref[...]), wrh_ref[...], wrl_ref[...], rb_ref[...])
        bucket_ref[:, rows] = bucket
        buckets.append(bucket)
    for h, bucket in enumerate(buckets):
        rows = slice(sub * h, sub * (h + 1))
        onehot = (lax.broadcasted_iota(jnp.int32, (ROUTER_ROWS, sub), 0) == bucket).astype(F32)
        before = _dot(onehot.astype(BF16), tri_ref[...]) + carry_ref[:, 0:1]
        rank_ref[:, rows] = jnp.sum(onehot * before, axis=0, keepdims=True).astype(jnp.int32)
        carry_ref[...] = carry_ref[...] + jnp.sum(onehot, axis=1, keepdims=True)
    counts_ref[...] = carry_ref[...]


def _start_row_gather(idx_ref, table_hbm, dst, sem):
    def body(blk, carry):
        for j in range(GATHER_UNROLL):
            r = blk * GATHER_UNROLL + j
            src_row = pl.multiple_of(idx_ref[0, 0, r] * TOKEN_TILE_ROWS, TOKEN_TILE_ROWS)
            dst_row = pl.multiple_of(r * TOKEN_TILE_ROWS, TOKEN_TILE_ROWS)
            pltpu.make_async_copy(table_hbm.at[pl.ds(src_row, TOKEN_TILE_ROWS)],
                                  dst.at[pl.ds(dst_row, TOKEN_TILE_ROWS)], sem).start()
        return carry
    lax.fori_loop(0, dst.shape[0] // (TOKEN_TILE_ROWS * GATHER_UNROLL), body, 0)


def _wait_row_gather(table_hbm, dst, sem):
    pltpu.make_async_copy(table_hbm.at[pl.ds(0, dst.shape[0])], dst, sem).wait()


def _tile_gates(logit, ea, eb):
    lane = lax.broadcasted_iota(jnp.int32, (1, LANES), 1)
    pick = lambda idx: jnp.sum(jnp.where(lane == idx, logit, 0.0), axis=-1, keepdims=True)
    glog = jnp.where(lane < N_GROUPS, logit, -jnp.inf)
    gmax = jnp.max(glog, axis=-1, keepdims=True)
    gsum = jnp.sum(jnp.exp(glog - gmax), axis=-1, keepdims=True)
    g_p = jnp.exp(pick(ea // EXPERTS_PER_GROUP) - gmax) / gsum
    la, lb = pick(N_GROUPS + ea), pick(N_GROUPS + eb)
    top = jnp.maximum(la, lb)
    pa, pb = jnp.exp(la - top), jnp.exp(lb - top)
    return g_p * pa / (pa + pb), g_p * pb / (pa + pb)


def _moe_sparse_kernel(ea_ref, eb_ref, nt_ref, src_cur_ref, src_nxt_ref, gffn_ref, wr_ref, rb_ref, x1t_hbm,
                       wga_ref, wua_ref, wda_ref, wgb_ref, wub_ref, wdb_ref, o_ref, buf, sem):
    t = pl.program_id(0)
    nt = nt_ref[0]
    slot = t % 2

    @pl.when(t == 0)
    def _():
        _start_row_gather(src_cur_ref, x1t_hbm, buf.at[0], sem.at[0])

    @pl.when(t + 1 < nt)
    def _():
        _start_row_gather(src_nxt_ref, x1t_hbm, buf.at[1 - slot], sem.at[1 - slot])

    @pl.when(t < nt)
    def _():
        _wait_row_gather(x1t_hbm, buf.at[slot], sem.at[slot])
        x = _from_tiles(buf.at[slot])
        xn = _row_rmsnorm(x, gffn_ref[...]).astype(BF16)
        gate_a, gate_b = _tile_gates(_dot(xn, wr_ref[...]) + rb_ref[...], ea_ref[t], eb_ref[t])
        out = x
        for gate, (wg, wu, wd) in ((gate_a, (wga_ref, wua_ref, wda_ref)), (gate_b, (wgb_ref, wub_ref, wdb_ref))):
            hid = jax.nn.silu(_dot(xn, wg[0])) * _dot(xn, wu[0])
            out = out + gate * _dot(hid.astype(BF16), wd[0])
        _to_tiles(o_ref, out)

    @pl.when(t >= nt)
    def _():
        o_ref[...] = jnp.zeros_like(o_ref)


def _unpermute_kernel(pos_cur_ref, pos_nxt_ref, sorted_hbm, o_ref, buf, sem):
    t = pl.program_id(0)
    slot = t % 2

    @pl.when(t == 0)
    def _():
        _start_row_gather(pos_cur_ref, sorted_hbm, buf.at[0], sem.at[0])

    @pl.when(t + 1 < pl.num_programs(0))
    def _():
        _start_row_gather(pos_nxt_ref, sorted_hbm, buf.at[1 - slot], sem.at[1 - slot])

    _wait_row_gather(sorted_hbm, buf.at[slot], sem.at[slot])
    o_ref[...] = _from_tiles(buf.at[slot])


def _seg_mats(width, segments):
    m = np.zeros((width, LANES), np.float32)
    invlen = np.ones((1, LANES), np.float32)
    for c, (start, length) in enumerate(segments):
        m[start:start + length, c] = 1.0
        invlen[0, c] = 1.0 / length
    return jnp.asarray(m, BF16), jnp.asarray(np.concatenate([m.T, m.T], axis=0), BF16), jnp.asarray(invlen)


def _rope_tables(pos_list, seq_len):
    inv = ROPE_THETA ** (-jnp.arange(0, 2 * ROPE_HALF, 2, dtype=F32) / (2 * ROPE_HALF))
    cos_cols, sin_cols = [], []
    for pos in pos_list:
        if pos is None:
            cos_cols.append(jnp.ones((seq_len, 2 * ROPE_HALF), F32))
            sin_cols.append(jnp.zeros((seq_len, 2 * ROPE_HALF), F32))
        else:
            ang = pos.astype(F32)[:, None] * inv[None, :]
            c, s = jnp.cos(ang), jnp.sin(ang)
            cos_cols.append(jnp.concatenate([c, c], axis=1))
            sin_cols.append(jnp.concatenate([-s, s], axis=1))
    return jnp.concatenate(cos_cols, axis=1), jnp.concatenate(sin_cols, axis=1)


def _t5_bucket_np(rel):
    nb = REL_BUCKETS // 2
    max_exact = nb // 2
    ret = np.where(rel > 0, nb, 0)
    n = np.abs(rel)
    nf = np.maximum(n, 1).astype(np.float32)
    large = max_exact + (np.log(nf / np.float32(max_exact)) / np.float32(math.log(REL_MAX_DIST / max_exact))
                         * np.float32(nb - max_exact)).astype(np.int32)
    large = np.minimum(large, nb - 1)
    return ret + np.where(n < max_exact, n, large)


def _full(shape):
    nd = len(shape)
    return pl.BlockSpec(shape, lambda *_: (0,) * nd)


def _params(*sem):
    return pltpu.CompilerParams(dimension_semantics=sem, vmem_limit_bytes=VMEM_LIMIT)


def _prep_ab(x2, seq_len, gmix, w_in, qag, wq, kvag, wkv, qn_g, kn_g, qr_g, kr_g, bq_g, bk_g, tm=512):
    n = x2.shape[0]
    scale_a = (HEAD + 2 * ROPE_HALF) ** -0.5
    scale_b = HEAD ** -0.5
    zeros = lambda r, c: jnp.zeros((r, c), F32)
    d = w_in.shape[0]
    w0 = jnp.concatenate([w_in[:, 0:384], zeros(d, HEAD), w_in[:, 384:416], zeros(d, 32), w_in[:, 416:1184]],
                         axis=1).astype(BF16)
    wq_p = jnp.concatenate([wq.reshape(-1, 8, 96), jnp.zeros((wq.shape[0], 8, 32), F32)], axis=2)
    wq_p = wq_p.reshape(-1, 1024).astype(BF16)
    wkv_r = wkv.reshape(-1, 8, 128)
    wk_p = jnp.concatenate([wkv_r[:, :, :HEAD], jnp.zeros_like(wkv_r[:, :, :HEAD])], axis=2).reshape(-1, 1024)
    wv_p = jnp.concatenate([wkv_r[:, :, HEAD:], jnp.zeros_like(wkv_r[:, :, HEAD:])], axis=2).reshape(-1, 1024)
    wkv_p = jnp.concatenate([wk_p, wv_p], axis=1).astype(BF16)

    seg_q = [(LANES * h, HEAD) for h in range(8)] + [(LANES * h + HEAD, 32) for h in range(8)]
    gq = jnp.tile(jnp.concatenate([qn_g, qr_g, jnp.zeros((32,), F32)]), 8)[None, :] * (scale_a * LOG2E)
    seg_k = [(LANES * h, HEAD) for h in range(8)]
    gk = jnp.tile(jnp.concatenate([kn_g, jnp.zeros((HEAD,), F32)]), 8)[None, :]
    seg_r = [(HEAD, 32)]
    gr = jnp.concatenate([jnp.zeros((HEAD,), F32), kr_g, jnp.zeros((32,), F32)])[None, :]
    seg_bq = [(HEAD * h, HEAD) for h in range(8)]
    gbq = jnp.tile(bq_g, 8)[None, :] * (scale_b * LOG2E)
    seg_bk = [(HEAD * h, HEAD) for h in range(2)]
    gbk = jnp.tile(bk_g, 2)[None, :]

    pos = jnp.arange(seq_len)
    cosa, sina = _rope_tables([None, None, pos, None], seq_len)
    row, col = pos // GRID_W, pos % GRID_W
    cosb, sinb = _rope_tables([row, col, row, col], seq_len)

    consts = [gmix[None, :], w0, qag[None, :], wq_p, kvag[None, :], wkv_p,
              *_seg_mats(1024, seg_q), gq, *_seg_mats(1024, seg_k), gk, *_seg_mats(LANES, seg_r), gr,
              *_seg_mats(512, seg_bq), gbq, *_seg_mats(LANES, seg_bk), gbk]
    nsb = seq_len // tm
    rope_spec = pl.BlockSpec((tm, LANES), lambda i: (i % nsb, 0))
    row_spec = lambda w: pl.BlockSpec((tm, w), lambda i: (i, 0))
    out_widths = (1024, 1024, 1024, 512, 256, 256)
    return pl.pallas_call(
        _prep_ab_kernel,
        grid=(n // tm,),
        in_specs=[row_spec(x2.shape[1])] + [_full(c.shape) for c in consts] + [rope_spec] * 4,
        out_specs=[row_spec(w) for w in out_widths],
        out_shape=[jax.ShapeDtypeStruct((n, w), BF16) for w in out_widths],
        compiler_params=_params("parallel"),
        name="prep_ab",
    )(x2, *consts, cosa, sina, cosb, sinb)


def _prep_c(x2, gmix, w_in, q_g, k_g, tm=512):
    n = x2.shape[0]
    seg_q = [(HEAD * h, HEAD) for h in range(16)]
    gq = jnp.tile(q_g, 16)[None, :] * (HEAD ** -0.5 * LOG2E)
    seg_k = [(HEAD * h, HEAD) for h in range(4)]
    gk = jnp.tile(k_g, 4)[None, :]
    consts = [gmix[None, :], w_in.astype(BF16), *_seg_mats(1024, seg_q), gq, *_seg_mats(256, seg_k), gk]
    row_spec = lambda w: pl.BlockSpec((tm, w), lambda i: (i, 0))
    out_widths = (1024, 512, 512)
    return pl.pallas_call(
        _prep_c_kernel,
        grid=(n // tm,),
        in_specs=[row_spec(x2.shape[1])] + [_full(c.shape) for c in consts],
        out_specs=[row_spec(w) for w in out_widths],
        out_shape=[jax.ShapeDtypeStruct((n, w), BF16) for w in out_widths],
        compiler_params=_params("parallel"),
        name="prep_c",
    )(x2, *consts)


def _mla_attn(qa, ka, va, tq=512):
    b, s, _ = qa.shape
    return pl.pallas_call(
        _mla_attn_kernel,
        grid=(b, 4, s // tq),
        in_specs=[pl.BlockSpec((1, tq, 2 * LANES), lambda bi, hp, qi: (bi, qi, hp)),
                  pl.BlockSpec((1, s, 2 * LANES), lambda bi, hp, qi: (bi, 0, hp)),
                  pl.BlockSpec((1, s, 2 * LANES), lambda bi, hp, qi: (bi, 0, hp))],
        out_specs=pl.BlockSpec((1, tq, LANES), lambda bi, hp, qi: (bi, qi, hp)),
        out_shape=jax.ShapeDtypeStruct((b, s, 512), BF16),
        scratch_shapes=[pltpu.VMEM((tq, s), F32), pltpu.VMEM((tq, s), F32)],
        compiler_params=_params("parallel", "parallel", "parallel"),
        name="mla_attn",
    )(qa, ka, va)


def _gqa_attn(qb, kb, vb, tq=256):
    b, s, _ = qb.shape
    return pl.pallas_call(
        _gqa_attn_kernel,
        grid=(b, 2, s // tq),
        in_specs=[pl.BlockSpec((1, tq, 2 * LANES), lambda bi, g, qi: (bi, qi, g)),
                  pl.BlockSpec((1, s, LANES), lambda bi, g, qi: (bi, 0, g)),
                  pl.BlockSpec((1, s, LANES), lambda bi, g, qi: (bi, 0, g))],
        out_specs=pl.BlockSpec((1, tq, 2 * LANES), lambda bi, g, qi: (bi, qi, g)),
        out_shape=jax.ShapeDtypeStruct((b, s, 512), BF16),
        scratch_shapes=[pltpu.VMEM((2 * tq, s), F32), pltpu.VMEM((2 * tq, s), F32)],
        compiler_params=_params("parallel", "parallel", "parallel"),
        name="gqa_attn",
    )(qb, kb, vb)


def _rel_bias_table(rel_bias):
    span = Q_BLOCK + 2 * WINDOW
    rel = np.arange(span)[None, :] - WINDOW - np.arange(Q_BLOCK)[:, None]
    bucket = _t5_bucket_np(rel).astype(np.int32).reshape(1, -1)
    band = (np.abs(rel) <= WINDOW).astype(np.int32).reshape(1, -1)
    heads = rel_bias.shape[1]
    cols = bucket.shape[1]
    chunk = cols // 8
    table = pl.pallas_call(
        _rel_bias_kernel,
        grid=(8,),
        in_specs=[pl.BlockSpec((1, chunk), lambda i: (0, i)), pl.BlockSpec((1, chunk), lambda i: (0, i)),
                  _full((heads, REL_BUCKETS))],
        out_specs=pl.BlockSpec((heads, chunk), lambda i: (0, i)),
        out_shape=jax.ShapeDtypeStruct((heads, cols), F32),
        compiler_params=_params("parallel"),
        name="rel_bias",
    )(jnp.asarray(bucket), jnp.asarray(band), rel_bias.T)
    return table.reshape(heads, Q_BLOCK, span)


def _win_attn(qc, kc, vc, sink, bias, nq=2):
    b, s, _ = qc.shape
    tq = Q_BLOCK
    nb = s // tq
    prev = lambda bi, i: (bi, jnp.maximum(i * nq - 1, 0), 0)
    cur = lambda bi, i: (bi, i, 0)
    nxt = lambda bi, i: (bi, jnp.minimum((i + 1) * nq, nb - 1), 0)
    edge_spec = lambda im: pl.BlockSpec((1, tq, 4 * LANES), im)
    cur_spec = pl.BlockSpec((1, nq * tq, 4 * LANES), cur)
    return pl.pallas_call(
        functools.partial(_win_attn_kernel, seq_len=s),
        grid=(b, nb // nq),
        in_specs=[pl.BlockSpec(memory_space=pltpu.SMEM),
                  pl.BlockSpec((1, nq * tq, 8 * LANES), cur),
                  edge_spec(prev), cur_spec, edge_spec(nxt),
                  edge_spec(prev), cur_spec, edge_spec(nxt),
                  _full(bias.shape)],
        out_specs=pl.BlockSpec((1, nq * tq, 8 * LANES), cur),
        scratch_shapes=[pltpu.VMEM((4 * nq, 4 * tq, 3 * tq), F32), pltpu.VMEM((4 * nq, 4 * tq, 3 * tq), BF16),
                        pltpu.VMEM((4 * nq, 4 * tq, LANES), F32)],
        out_shape=jax.ShapeDtypeStruct((b, s, 1024), BF16),
        compiler_params=_params("parallel", "parallel"),
        name="win_attn",
    )(sink, qc, kc, kc, kc, vc, vc, vc, bias)


def _post(parts, ws, x2, gffn, w_group, b_group, w_router, b_router, tm=1024, sub=256):
    n, d = x2.shape
    wr = jnp.concatenate([w_group.T, w_router.T, jnp.zeros((ROUTER_ROWS - N_GROUPS - N_EXPERTS, d), F32)], axis=0)
    wrh = wr.astype(BF16)
    wrl = (wr - wrh.astype(F32)).astype(BF16)
    rb = jnp.concatenate([b_group, b_router, jnp.zeros((ROUTER_ROWS - N_GROUPS - N_EXPERTS,), F32)])[:, None]
    row_spec = lambda w: pl.BlockSpec((tm, w), lambda i: (i, 0))
    lane_spec = lambda r: pl.BlockSpec((r, tm), lambda i: (0, i))
    ws = [w.astype(BF16) for w in ws]
    tri = jnp.asarray(np.triu(np.ones((sub, sub), np.float32), 1), BF16)
    return pl.pallas_call(
        functools.partial(_post_kernel, n_parts=len(parts)),
        grid=(n // tm,),
        in_specs=[row_spec(p.shape[1]) for p in parts] + [_full(w.shape) for w in ws]
                 + [row_spec(d), _full((1, d)), _full(wrh.shape), _full(wrl.shape), _full(rb.shape),
                    _full(tri.shape)],
        out_specs=[pl.BlockSpec((tm * TOKEN_TILE_ROWS, LANES), lambda i: (i, 0)),
                   lane_spec(1), lane_spec(1), _full((ROUTER_ROWS, LANES))],
        out_shape=[jax.ShapeDtypeStruct((n * TOKEN_TILE_ROWS, LANES), F32),
                   jax.ShapeDtypeStruct((1, n), jnp.int32), jax.ShapeDtypeStruct((1, n), jnp.int32),
                   jax.ShapeDtypeStruct((ROUTER_ROWS, LANES), F32)],
        scratch_shapes=[pltpu.VMEM((ROUTER_ROWS, LANES), F32)],
        compiler_params=_params("arbitrary"),
        name="post",
    )(*parts, *ws, x2, gffn[None, :], wrh, wrl, rb, tri)


def _route(bucket, rank, counts, tmm):
    n = bucket.shape[1]
    n_tiles_max = n // tmm + N_BUCKETS
    n_slots = n_tiles_max * tmm
    bucket, rank = bucket[0], rank[0]
    cnt = counts[:N_BUCKETS, 0].astype(jnp.int32)
    padded = (cnt + tmm - 1) // tmm * tmm
    ends = jnp.cumsum(padded)
    starts = ends - padded
    pos = starts[bucket] + rank
    n_tiles = ends[-1] // tmm
    tile = jnp.minimum(jnp.arange(n_tiles_max, dtype=jnp.int32), n_tiles - 1)
    tile_bucket = jnp.sum((tile * tmm)[:, None] >= ends[None, :], axis=1).astype(jnp.int32)
    grp, pair = tile_bucket // PAIRS_PER_GROUP, tile_bucket % PAIRS_PER_GROUP
    tile_ea = grp * EXPERTS_PER_GROUP + jnp.asarray(PAIR_LO, jnp.int32)[pair]
    tile_eb = grp * EXPERTS_PER_GROUP + jnp.asarray(PAIR_HI, jnp.int32)[pair]
    token = jnp.arange(n, dtype=jnp.int32)
    src = (jnp.arange(n_slots + tmm, dtype=jnp.int32) % n).at[pos].set(token)
    return pos, src.reshape(n_tiles_max + 1, 1, tmm), tile_ea, tile_eb, n_tiles.reshape(1)


def _moe_sparse(x1t, gffn, w_group, b_group, w_router, b_router, src, tile_ea, tile_eb, n_tiles,
                w_gate, w_up, w_down, tmm):
    n_tiles_max = src.shape[0] - 1
    ne, d, de = w_gate.shape
    idx_spec = lambda off: pl.BlockSpec((1, 1, tmm), lambda t, ea, eb, nt: (t + off, 0, 0),
                                        memory_space=pltpu.SMEM)
    up_spec = lambda which: pl.BlockSpec((1, d, de), lambda t, ea, eb, nt: ((ea, eb)[which][t], 0, 0))
    down_spec = lambda which: pl.BlockSpec((1, de, d), lambda t, ea, eb, nt: ((ea, eb)[which][t], 0, 0))
    wg, wu, wd = w_gate.astype(BF16), w_up.astype(BF16), w_down.astype(BF16)
    pad = LANES - N_GROUPS - N_EXPERTS
    wr = jnp.concatenate([w_group, w_router, jnp.zeros((d, pad), F32)], axis=1).astype(BF16)
    rb = jnp.concatenate([b_group, b_router, jnp.zeros((pad,), F32)])[None, :]
    return pl.pallas_call(
        _moe_sparse_kernel,
        grid_spec=pltpu.PrefetchScalarGridSpec(
            num_scalar_prefetch=3,
            grid=(n_tiles_max,),
            in_specs=[idx_spec(0), idx_spec(1),
                      pl.BlockSpec((1, d), lambda t, ea, eb, nt: (0, 0)),
                      pl.BlockSpec((d, LANES), lambda t, ea, eb, nt: (0, 0)),
                      pl.BlockSpec((1, LANES), lambda t, ea, eb, nt: (0, 0)),
                      pl.BlockSpec(memory_space=pl.ANY),
                      up_spec(0), up_spec(0), down_spec(0), up_spec(1), up_spec(1), down_spec(1)],
            out_specs=pl.BlockSpec((tmm * TOKEN_TILE_ROWS, LANES), lambda t, ea, eb, nt: (t, 0)),
            scratch_shapes=[pltpu.VMEM((2, tmm * TOKEN_TILE_ROWS, LANES), F32), pltpu.SemaphoreType.DMA((2,))]),
        out_shape=jax.ShapeDtypeStruct((n_tiles_max * tmm * TOKEN_TILE_ROWS, LANES), F32),
        compiler_params=_params("arbitrary"),
        name="moe_sparse",
    )(tile_ea, tile_eb, n_tiles, src, src, gffn[None, :], wr, rb, x1t, wg, wu, wd, wg, wu, wd)


def _unpermute(sorted_rows, pos, tm=256):
    n = pos.shape[0]
    d = TOKEN_TILE_ROWS * LANES
    pos2 = jnp.concatenate([pos, jnp.zeros((tm,), jnp.int32)]).reshape(n // tm + 1, 1, tm)
    idx_spec = lambda off: pl.BlockSpec((1, 1, tm), lambda t: (t + off, 0, 0), memory_space=pltpu.SMEM)
    return pl.pallas_call(
        _unpermute_kernel,
        grid=(n // tm,),
        in_specs=[idx_spec(0), idx_spec(1), pl.BlockSpec(memory_space=pl.ANY)],
        out_specs=pl.BlockSpec((tm, d), lambda t: (t, 0)),
        out_shape=jax.ShapeDtypeStruct((n, d), F32),
        scratch_shapes=[pltpu.VMEM((2, tm * TOKEN_TILE_ROWS, LANES), F32), pltpu.SemaphoreType.DMA((2,))],
        compiler_params=_params("arbitrary"),
        name="unpermute",
    )(pos2, pos2, sorted_rows)


def kernel(x, mix_norm, ffn_norm, w_in_ab, mla_q_a_norm, mla_w_q_up, mla_kv_a_norm, mla_w_kv_up, mla_qn_gain, mla_kn_gain, mla_qr_gain, mla_kr_gain, gqa_q_gain, gqa_k_gain, w_out_ab, w_in_c, win_q_gain, win_k_gain, win_sink, w_out_c, rel_bias, moe_w_group, moe_b_group, moe_w_router, moe_b_router, moe_w_gate, moe_w_up, moe_w_down):
    b, s, d = x.shape
    n = b * s
    depth = mix_norm.shape[0]
    x2 = x.reshape(n, d)
    bias = None
    for layer in range(depth):
        i = layer // 2
        if layer % 2 == 0:
            qa, ka, va, qb, kb, vb = _prep_ab(
                x2, s, mix_norm[layer], w_in_ab[i], mla_q_a_norm[i], mla_w_q_up[i], mla_kv_a_norm[i],
                mla_w_kv_up[i], mla_qn_gain[i], mla_kn_gain[i], mla_qr_gain[i], mla_kr_gain[i],
                gqa_q_gain[i], gqa_k_gain[i])
            r3 = lambda t: t.reshape(b, s, t.shape[1])
            out_a = _mla_attn(r3(qa), r3(ka), r3(va)).reshape(n, 512)
            out_b = _gqa_attn(r3(qb), r3(kb), r3(vb)).reshape(n, 512)
            parts, ws = [out_a, out_b], [w_out_ab[i][:512], w_out_ab[i][512:]]
        else:
            if bias is None:
                bias = _rel_bias_table(rel_bias)
            qc, kc, vc = _prep_c(x2, mix_norm[layer], w_in_c[i], win_q_gain[i], win_k_gain[i])
            r3 = lambda t: t.reshape(b, s, t.shape[1])
            out_c = _win_attn(r3(qc), r3(kc), r3(vc), win_sink[i], bias).reshape(n, 1024)
            parts, ws = [out_c], [w_out_c[i]]
        router = (moe_w_group[layer], moe_b_group[layer], moe_w_router[layer], moe_b_router[layer])
        x1t, bucket, rank, counts = _post(parts, ws, x2, ffn_norm[layer], *router)
        pos, src, tile_ea, tile_eb, n_tiles = _route(bucket, rank, counts, MOE_TILE)
        y_sorted = _moe_sparse(x1t, ffn_norm[layer], *router, src, tile_ea, tile_eb, n_tiles,
                               moe_w_gate[layer], moe_w_up[layer], moe_w_down[layer], MOE_TILE)
        x2 = _unpermute(y_sorted, pos)
    return x2.reshape(b, s, d)
```

```python
import functools
import math

import numpy as np
import jax
import jax.numpy as jnp
from jax import lax
from jax.experimental import pallas as pl
from jax.experimental.pallas import tpu as pltpu

F32 = jnp.float32
BF16 = jnp.bfloat16

EPS = 1e-6
ROPE_THETA = 10000.0
LANES = 128
HEAD = 64
ROPE_HALF = 16
GRID_W = 64
WINDOW = 128
Q_BLOCK = 128
REL_BUCKETS = 32
REL_MAX_DIST = 128
N_GROUPS = 4
EXPERTS_PER_GROUP = 4
N_EXPERTS = 16
PAIRS_PER_GROUP = 6
N_BUCKETS = N_GROUPS * PAIRS_PER_GROUP
PAIR_LO = (0, 0, 0, 1, 1, 2)
PAIR_HI = (1, 2, 3, 2, 3, 3)
TOKEN_TILE_ROWS = 8
KEY_CHUNK = 512
LOG2E = math.log2(math.e)
PREP_SUB = 256
WIN_ROW_TILE = 32
GATHER_UNROLL = 8
MOE_TILE = 256
ROUTER_ROWS = 32
VMEM_LIMIT = 56 * 1024 * 1024

_NT = (((1,), (1,)), ((), ()))


def _dot(a, b):
    return jnp.dot(a, b, preferred_element_type=F32)


def _dot_nt(a, b):
    return lax.dot_general(a, b, _NT, preferred_element_type=F32)


def _split_bf16(a):
    hi = a.astype(BF16)
    lo = (a - hi.astype(F32)).astype(BF16)
    return hi, lo


def _row_rmsnorm(t, gain):
    return t * lax.rsqrt(jnp.mean(t * t, axis=-1, keepdims=True) + EPS) * gain


def _seg_rmsnorm(t, mseg, msegt, invlen, gain):
    sums = _dot((t * t).astype(BF16), mseg)
    inv = lax.rsqrt(sums * invlen + EPS)
    ihi, ilo = _split_bf16(inv)
    scale = _dot(jnp.concatenate([ihi, ilo], axis=1), msegt)
    return t * scale * gain


def _rope128(t, cos, sin_signed, first_half):
    up = pltpu.roll(t, LANES - ROPE_HALF, 1)
    dn = pltpu.roll(t, ROPE_HALF, 1)
    return t * cos + jnp.where(first_half, up, dn) * sin_signed


def _dup_halves(blk, lo):
    sw = pltpu.roll(blk, HEAD, 1)
    return jnp.where(lo, blk, sw), jnp.where(lo, sw, blk)


def _lane_masks():
    lane = lax.broadcasted_iota(jnp.int32, (1, LANES), 1)
    return lane < HEAD, (lane % (2 * ROPE_HALF)) < ROPE_HALF


def _by_sub_tiles(rows_fn, refs, n_const):
    row_refs = (refs[0],) + tuple(refs[1 + n_const:])
    for h in range(refs[0].shape[0] // PREP_SUB):
        view = [r.at[pl.ds(PREP_SUB * h, PREP_SUB)] for r in row_refs]
        rows_fn(view[0], *refs[1:1 + n_const], *view[1:])


def _prep_ab_kernel(*refs):
    _by_sub_tiles(_prep_ab_rows, refs, 26)


def _prep_c_kernel(*refs):
    _by_sub_tiles(_prep_c_rows, refs, 10)


def _prep_ab_rows(x_ref, gmix_ref, w0_ref, qag_ref, wq_ref, kvag_ref, wkv_ref,
                    mq_ref, mqt_ref, ilq_ref, gq_ref,
                    mk_ref, mkt_ref, ilk_ref, gk_ref,
                    mr_ref, mrt_ref, ilr_ref, gr_ref,
                    mbq_ref, mbqt_ref, ilbq_ref, gbq_ref,
                    mbk_ref, mbkt_ref, ilbk_ref, gbk_ref,
                    cosa_ref, sina_ref, cosb_ref, sinb_ref,
                    qa_ref, ka_ref, va_ref, qb_ref, kb_ref, vb_ref):
    lo, first_half = _lane_masks()
    h = _row_rmsnorm(x_ref[...], gmix_ref[...]).astype(BF16)
    proj = _dot(h, w0_ref[...])
    cosa, sina = cosa_ref[...], sina_ref[...]
    cosb, sinb = cosb_ref[...], sinb_ref[...]

    qn = _row_rmsnorm(proj[:, 0:256], qag_ref[...]).astype(BF16)
    q = _seg_rmsnorm(_dot(qn, wq_ref[...]), mq_ref[...], mqt_ref[...], ilq_ref[...], gq_ref[...])
    for hd in range(8):
        sl = slice(LANES * hd, LANES * (hd + 1))
        qa_ref[:, sl] = _rope128(q[:, sl], cosa, sina, first_half).astype(BF16)

    kvn = _row_rmsnorm(proj[:, 256:384], kvag_ref[...]).astype(BF16)
    kv = _dot(kvn, wkv_ref[...])
    ones_hi = jnp.where(lo, 0.0, 1.0)
    kn = _seg_rmsnorm(kv[:, 0:1024], mk_ref[...], mkt_ref[...], ilk_ref[...], gk_ref[...])
    kr = _seg_rmsnorm(proj[:, 384:512], mr_ref[...], mrt_ref[...], ilr_ref[...], gr_ref[...])
    kr = _rope128(kr, cosa, sina, first_half)
    for hd in range(8):
        sl = slice(LANES * hd, LANES * (hd + 1))
        ka_ref[:, sl] = (kn[:, sl] + kr).astype(BF16)
        va_ref[:, sl] = (kv[:, 1024 + LANES * hd:1024 + LANES * (hd + 1)] + ones_hi).astype(BF16)

    bq = _seg_rmsnorm(proj[:, 512:1024], mbq_ref[...], mbqt_ref[...], ilbq_ref[...], gbq_ref[...])
    for blk in range(4):
        sl = slice(LANES * blk, LANES * (blk + 1))
        qb_ref[:, sl] = _rope128(bq[:, sl], cosb, sinb, first_half).astype(BF16)
    bk = _seg_rmsnorm(proj[:, 1024:1152], mbk_ref[...], mbkt_ref[...], ilbk_ref[...], gbk_ref[...])
    bk = _rope128(bk, cosb, sinb, first_half)
    k0, k1 = _dup_halves(bk, lo)
    kb_ref[:, 0:LANES] = k0.astype(BF16)
    kb_ref[:, LANES:2 * LANES] = k1.astype(BF16)
    bv = proj[:, 1152:1280]
    vb_ref[:, 0:LANES] = jnp.where(lo, bv, 1.0).astype(BF16)
    vb_ref[:, LANES:2 * LANES] = jnp.where(lo, pltpu.roll(bv, HEAD, 1), 1.0).astype(BF16)


def _prep_c_rows(x_ref, gmix_ref, w_ref,
                   mq_ref, mqt_ref, ilq_ref, gq_ref,
                   mk_ref, mkt_ref, ilk_ref, gk_ref,
                   qc_ref, kc_ref, vc_ref):
    lo, _ = _lane_masks()
    h = _row_rmsnorm(x_ref[...], gmix_ref[...]).astype(BF16)
    proj = _dot(h, w_ref[...])
    q = _seg_rmsnorm(proj[:, 0:1024], mq_ref[...], mqt_ref[...], ilq_ref[...], gq_ref[...])
    qc_ref[...] = q.astype(BF16)
    k = _seg_rmsnorm(proj[:, 1024:1280], mk_ref[...], mkt_ref[...], ilk_ref[...], gk_ref[...])
    for blk in range(2):
        sl = slice(LANES * blk, LANES * (blk + 1))
        k0, k1 = _dup_halves(k[:, sl], lo)
        kc_ref[:, 2 * blk * LANES:(2 * blk + 1) * LANES] = k0.astype(BF16)
        kc_ref[:, (2 * blk + 1) * LANES:(2 * blk + 2) * LANES] = k1.astype(BF16)
        v = proj[:, 1280 + LANES * blk:1280 + LANES * (blk + 1)]
        vc_ref[:, 2 * blk * LANES:(2 * blk + 1) * LANES] = jnp.where(lo, v, 1.0).astype(BF16)
        vc_ref[:, (2 * blk + 1) * LANES:(2 * blk + 2) * LANES] = jnp.where(
            lo, pltpu.roll(v, HEAD, 1), 1.0).astype(BF16)


def _softmax_pv(s, v):
    m = jnp.max(s, axis=-1, keepdims=True)
    p = jnp.exp(s - m)
    l = jnp.sum(p, axis=-1, keepdims=True)
    return _dot(p.astype(BF16), v) / l


def _lane_chunk_reduce(op, t):
    out = t[:, 0:LANES]
    for j in range(1, t.shape[1] // LANES):
        out = op(out, t[:, LANES * j:LANES * (j + 1)])
    return out


def _scores_phase(q, k_ref, lanes, s_buf):
    m_part = None
    for c in range(k_ref.shape[1] // KEY_CHUNK):
        ks = slice(KEY_CHUNK * c, KEY_CHUNK * (c + 1))
        s_c = _dot_nt(q, k_ref[0, ks, lanes])
        s_buf[:, ks] = s_c
        mc = _lane_chunk_reduce(jnp.maximum, s_c)
        m_part = mc if m_part is None else jnp.maximum(m_part, mc)
    return jnp.max(m_part, axis=-1, keepdims=True)


def _pv_phase(s_buf, m, v_ref, lanes):
    lo, _ = _lane_masks()
    acc = None
    for c in range(v_ref.shape[1] // KEY_CHUNK):
        ks = slice(KEY_CHUNK * c, KEY_CHUNK * (c + 1))
        p = jnp.exp2(s_buf[:, ks] - m)
        pv = _dot(p.astype(BF16), v_ref[0, ks, lanes])
        acc = pv if acc is None else acc + pv
    return acc / jnp.where(lo, pltpu.roll(acc, HEAD, 1), 1.0)


def _pair_heads(o_even, o_odd):
    lo, _ = _lane_masks()
    return jnp.where(lo, o_even, pltpu.roll(o_odd, HEAD, 1))


def _mla_attn_kernel(q_ref, k_ref, v_ref, o_ref, s0_ref, s1_ref):
    first, second = slice(0, LANES), slice(LANES, 2 * LANES)
    m0 = _scores_phase(q_ref[0, :, first], k_ref, first, s0_ref)
    m1 = _scores_phase(q_ref[0, :, second], k_ref, second, s1_ref)
    o0 = _pv_phase(s0_ref, m0, v_ref, first)
    o1 = _pv_phase(s1_ref, m1, v_ref, second)
    o_ref[0] = _pair_heads(o0, o1).astype(o_ref.dtype)


def _gqa_attn_kernel(q_ref, k_ref, v_ref, o_ref, s0_ref, s1_ref):
    lo, _ = _lane_masks()
    tq = q_ref.shape[1]
    hi = jnp.logical_not(lo)
    ms = []
    for j, s_ref in enumerate((s0_ref, s1_ref)):
        blk = q_ref[0, :, LANES * j:LANES * (j + 1)]
        zero = jnp.zeros_like(blk)
        qs = jnp.concatenate([jnp.where(lo, blk, zero), jnp.where(hi, blk, zero)], axis=0)
        ms.append(_scores_phase(qs, k_ref, slice(0, LANES), s_ref))
    for j, s_ref in enumerate((s0_ref, s1_ref)):
        o = _pv_phase(s_ref, ms[j], v_ref, slice(0, LANES))
        o_ref[0, :, LANES * j:LANES * (j + 1)] = _pair_heads(o[0:tq], o[tq:2 * tq]).astype(o_ref.dtype)


def _win_attn_kernel(sink_ref, q_ref, kp_ref, kc_ref, kn_ref, vp_ref, vc_ref, vn_ref, bias_ref, o_ref,
                     s_scr, p_scr, m_scr, *, seq_len):
    lo, _ = _lane_masks()
    tq = Q_BLOCK
    nq = q_ref.shape[1] // tq
    i = pl.program_id(1)
    kj = lax.broadcasted_iota(jnp.int32, (1, 3 * tq), 1)
    units = [(g, u) for g in range(4) for u in range(nq)]
    for idx, (g, u) in enumerate(units):
        sl = slice(LANES * g, LANES * (g + 1))
        kcat = jnp.concatenate([kp_ref[0, :, sl], kc_ref[0, :, sl], kn_ref[0, :, sl]], axis=0)
        parts = []
        for a in range(4):
            blk = q_ref[0, tq * u:tq * (u + 1), LANES * (2 * g + a // 2):LANES * (2 * g + a // 2 + 1)]
            keep = lo if a % 2 == 0 else jnp.logical_not(lo)
            parts.append(jnp.where(keep, blk, jnp.zeros_like(blk)))
        s_scr[idx] = _dot_nt(jnp.concatenate(parts, axis=0), kcat[tq * u:tq * (u + 3)])
    for idx, (g, u) in enumerate(units):
        key_pos = (i * nq + u - 1) * tq + kj
        valid = jnp.logical_and(key_pos >= 0, key_pos < seq_len)
        for rt in range(4 * tq // WIN_ROW_TILE):
            a, r0 = divmod(rt * WIN_ROW_TILE, tq)
            rows = slice(rt * WIN_ROW_TILE, (rt + 1) * WIN_ROW_TILE)
            s = s_scr[idx, rows, :] + bias_ref[4 * g + a, r0:r0 + WIN_ROW_TILE, :]
            s = jnp.where(valid, s, -jnp.inf)
            m = jnp.maximum(jnp.max(_lane_chunk_reduce(jnp.maximum, s), axis=-1, keepdims=True),
                            sink_ref[4 * g + a] * LOG2E)
            p_scr[idx, rows, :] = jnp.exp2(s - m).astype(BF16)
            m_scr[idx, rows, :] = jnp.broadcast_to(m, (WIN_ROW_TILE, LANES))
    for idx, (g, u) in enumerate(units):
        sl = slice(LANES * g, LANES * (g + 1))
        vcat = jnp.concatenate([vp_ref[0, :, sl], vc_ref[0, :, sl], vn_ref[0, :, sl]], axis=0)
        acc = _dot(p_scr[idx], vcat[tq * u:tq * (u + 3)])
        outs = []
        for a in range(4):
            rows = slice(tq * a, tq * (a + 1))
            sink_term = jnp.exp2(sink_ref[4 * g + a] * LOG2E - m_scr[idx, rows, :])
            den = jnp.where(lo, pltpu.roll(acc[rows], HEAD, 1) + sink_term, 1.0)
            outs.append(acc[rows] / den)
        for j in range(2):
            blk = 2 * g + j
            o_ref[0, tq * u:tq * (u + 1), LANES * blk:LANES * (blk + 1)] = _pair_heads(
                outs[2 * j], outs[2 * j + 1]).astype(o_ref.dtype)


def _rel_bias_kernel(bucket_ref, band_ref, relt_ref, o_ref):
    bucket = bucket_ref[...]
    acc = jnp.zeros(o_ref.shape, F32)
    for r in range(REL_BUCKETS):
        acc = acc + jnp.where(bucket == r, relt_ref[:, r:r + 1], 0.0)
    o_ref[...] = jnp.where(band_ref[...] > 0, acc * LOG2E, -jnp.inf)


def _to_tiles(ref, val, first_token=0):
    for c in range(TOKEN_TILE_ROWS):
        ref[pl.ds(first_token * TOKEN_TILE_ROWS + c, val.shape[0], stride=TOKEN_TILE_ROWS), :] = (
            val[:, LANES * c:LANES * (c + 1)])


def _from_tiles(ref):
    rows = ref.shape[0] // TOKEN_TILE_ROWS
    return jnp.concatenate([ref[pl.ds(c, rows, stride=TOKEN_TILE_ROWS), :] for c in range(TOKEN_TILE_ROWS)],
                           axis=1)


def _route_bucket(xn, wh, wl, rb):
    xh, xl = _split_bf16(xn)
    logit = _dot_nt(wh, xh) + _dot_nt(wh, xl) + _dot_nt(wl, xh) + rb
    g = [logit[r:r + 1, :] for r in range(N_GROUPS)]
    gmax = jnp.maximum(jnp.maximum(g[0], g[1]), jnp.maximum(g[2], g[3]))
    gidx = jnp.where(g[0] == gmax, 0, jnp.where(g[1] == gmax, 1, jnp.where(g[2] == gmax, 2, 3)))
    e = []
    for j in range(EXPERTS_PER_GROUP):
        rows = [logit[N_GROUPS + EXPERTS_PER_GROUP * gg + j:N_GROUPS + EXPERTS_PER_GROUP * gg + j + 1, :]
                for gg in range(N_GROUPS)]
        e.append(jnp.where(gidx == 0, rows[0], jnp.where(gidx == 1, rows[1],
                                                         jnp.where(gidx == 2, rows[2], rows[3]))))
    emax = jnp.maximum(jnp.maximum(e[0], e[1]), jnp.maximum(e[2], e[3]))
    ex = [jnp.exp(ej - emax) for ej in e]
    esum = ex[0] + ex[1] + ex[2] + ex[3]
    pr = [exj / esum for exj in ex]
    p1 = jnp.maximum(jnp.maximum(pr[0], pr[1]), jnp.maximum(pr[2], pr[3]))
    i1 = jnp.where(pr[0] == p1, 0, jnp.where(pr[1] == p1, 1, jnp.where(pr[2] == p1, 2, 3)))
    rest = [jnp.where(i1 == j, -1.0, pr[j]) for j in range(EXPERTS_PER_GROUP)]
    p2 = jnp.maximum(jnp.maximum(rest[0], rest[1]), jnp.maximum(rest[2], rest[3]))
    i2 = jnp.where(rest[0] == p2, 0, jnp.where(rest[1] == p2, 1, jnp.where(rest[2] == p2, 2, 3)))
    lo_e = jnp.minimum(i1, i2)
    hi_e = jnp.maximum(i1, i2)
    pair = jnp.where(lo_e == 0, hi_e - 1, jnp.where(lo_e == 1, hi_e + 1, 5))
    return gidx * PAIRS_PER_GROUP + pair


def _post_kernel(*refs, n_parts):
    parts = refs[:n_parts]
    ws = refs[n_parts:2 * n_parts]
    (x_ref, gffn_ref, wrh_ref, wrl_ref, rb_ref, tri_ref,
     x1t_ref, bucket_ref, rank_ref, counts_ref, carry_ref) = refs[2 * n_parts:]

    @pl.when(pl.program_id(0) == 0)
    def _():
        carry_ref[...] = jnp.zeros_like(carry_ref)

    sub = tri_ref.shape[0]
    buckets = []
    for h in range(x_ref.shape[0] // sub):
        rows = slice(sub * h, sub * (h + 1))
        acc = x_ref[rows, :]
        for p_ref, w_ref in zip(parts, ws):
            acc = acc + _dot(p_ref[rows, :], w_ref[...])
        _to_tiles(x1t_ref, acc, sub * h)
        bucket = _route_bucket(_row_rmsnorm(acc, gffn_ref[...]), wrh_ref[...], wrl_ref[...], rb_ref[...])
        bucket_ref[:, rows] = bucket
        buckets.append(bucket)
    for h, bucket in enumerate(buckets):
        rows = slice(sub * h, sub * (h + 1))
        onehot = (lax.broadcasted_iota(jnp.int32, (ROUTER_ROWS, sub), 0) == bucket).astype(F32)
        before = _dot(onehot.astype(BF16), tri_ref[...]) + carry_ref[:, 0:1]
        rank_ref[:, rows] = jnp.sum(onehot * before, axis=0, keepdims=True).astype(jnp.int32)
        carry_ref[...] = carry_ref[...] + jnp.sum(onehot, axis=1, keepdims=True)
    counts_ref[...] = carry_ref[...]


def _start_row_gather(idx_ref, table_hbm, dst, sem):
    def body(blk, carry):
        for j in range(GATHER_UNROLL):
            r = blk * GATHER_UNROLL + j
            src_row = pl.multiple_of(idx_ref[0, 0, r] * TOKEN_TILE_ROWS, TOKEN_TILE_ROWS)
            dst_row = pl.multiple_of(r * TOKEN_TILE_ROWS, TOKEN_TILE_ROWS)
            pltpu.make_async_copy(table_hbm.at[pl.ds(src_row, TOKEN_TILE_ROWS)],
                                  dst.at[pl.ds(dst_row, TOKEN_TILE_ROWS)], sem).start()
        return carry
    lax.fori_loop(0, dst.shape[0] // (TOKEN_TILE_ROWS * GATHER_UNROLL), body, 0)


def _wait_row_gather(table_hbm, dst, sem):
    pltpu.make_async_copy(table_hbm.at[pl.ds(0, dst.shape[0])], dst, sem).wait()


def _tile_gates(logit, ea, eb):
    lane = lax.broadcasted_iota(jnp.int32, (1, LANES), 1)
    pick = lambda idx: jnp.sum(jnp.where(lane == idx, logit, 0.0), axis=-1, keepdims=True)
    glog = jnp.where(lane < N_GROUPS, logit, -jnp.inf)
    gmax = jnp.max(glog, axis=-1, keepdims=True)
    gsum = jnp.sum(jnp.exp(glog - gmax), axis=-1, keepdims=True)
    g_p = jnp.exp(pick(ea // EXPERTS_PER_GROUP) - gmax) / gsum
    la, lb = pick(N_GROUPS + ea), pick(N_GROUPS + eb)
    top = jnp.maximum(la, lb)
    pa, pb = jnp.exp(la - top), jnp.exp(lb - top)
    return g_p * pa / (pa + pb), g_p * pb / (pa + pb)


def _moe_sparse_kernel(ea_ref, eb_ref, nt_ref, src_cur_ref, src_nxt_ref, gffn_ref, wr_ref, rb_ref, x1t_hbm,
                       wga_ref, wua_ref, wda_ref, wgb_ref, wub_ref, wdb_ref, o_ref, buf, sem):
    t = pl.program_id(0)
    nt = nt_ref[0]
    slot = t % 2

    @pl.when(t == 0)
    def _():
        _start_row_gather(src_cur_ref, x1t_hbm, buf.at[0], sem.at[0])

    @pl.when(t + 1 < nt)
    def _():
        _start_row_gather(src_nxt_ref, x1t_hbm, buf.at[1 - slot], sem.at[1 - slot])

    @pl.when(t < nt)
    def _():
        _wait_row_gather(x1t_hbm, buf.at[slot], sem.at[slot])
        x = _from_tiles(buf.at[slot])
        xn = _row_rmsnorm(x, gffn_ref[...])
        gate_a, gate_b = _tile_gates(_dot(xn, wr_ref[...]) + rb_ref[...], ea_ref[t], eb_ref[t])
        hid_a = jax.nn.silu(_dot(xn, wga_ref[0, 0])) * _dot(xn, wua_ref[0, 0])
        hid_b = jax.nn.silu(_dot(xn, wgb_ref[0, 0])) * _dot(xn, wub_ref[0, 0])
        out = x + gate_a * _dot(hid_a, wda_ref[0, 0]) + gate_b * _dot(hid_b, wdb_ref[0, 0])
        _to_tiles(o_ref, out)

    @pl.when(t >= nt)
    def _():
        o_ref[...] = jnp.zeros_like(o_ref)


def _unpermute_kernel(pos_cur_ref, pos_nxt_ref, sorted_hbm, o_ref, buf, sem):
    t = pl.program_id(0)
    slot = t % 2

    @pl.when(t == 0)
    def _():
        _start_row_gather(pos_cur_ref, sorted_hbm, buf.at[0], sem.at[0])

    @pl.when(t + 1 < pl.num_programs(0))
    def _():
        _start_row_gather(pos_nxt_ref, sorted_hbm, buf.at[1 - slot], sem.at[1 - slot])

    _wait_row_gather(sorted_hbm, buf.at[slot], sem.at[slot])
    o_ref[...] = _from_tiles(buf.at[slot])


def _seg_mats(width, segments):
    m = np.zeros((width, LANES), np.float32)
    invlen = np.ones((1, LANES), np.float32)
    for c, (start, length) in enumerate(segments):
        m[start:start + length, c] = 1.0
        invlen[0, c] = 1.0 / length
    return jnp.asarray(m, BF16), jnp.asarray(np.concatenate([m.T, m.T], axis=0), BF16), jnp.asarray(invlen)


def _rope_tables(pos_list, seq_len):
    inv = np.float32(ROPE_THETA) ** (-np.arange(0, 2 * ROPE_HALF, 2, dtype=np.float32) / np.float32(2 * ROPE_HALF))
    cos_cols, sin_cols = [], []
    for pos in pos_list:
        if pos is None:
            cos_cols.append(np.ones((seq_len, 2 * ROPE_HALF), np.float32))
            sin_cols.append(np.zeros((seq_len, 2 * ROPE_HALF), np.float32))
        else:
            ang = pos.astype(np.float32)[:, None] * inv[None, :]
            c, s = np.cos(ang), np.sin(ang)
            cos_cols.append(np.concatenate([c, c], axis=1))
            sin_cols.append(np.concatenate([-s, s], axis=1))
    return jnp.asarray(np.concatenate(cos_cols, axis=1)), jnp.asarray(np.concatenate(sin_cols, axis=1))


def _t5_bucket_np(rel):
    nb = REL_BUCKETS // 2
    max_exact = nb // 2
    ret = np.where(rel > 0, nb, 0)
    n = np.abs(rel)
    nf = np.maximum(n, 1).astype(np.float32)
    large = max_exact + (np.log(nf / np.float32(max_exact)) / np.float32(math.log(REL_MAX_DIST / max_exact))
                         * np.float32(nb - max_exact)).astype(np.int32)
    large = np.minimum(large, nb - 1)
    return ret + np.where(n < max_exact, n, large)


def _full(shape):
    nd = len(shape)
    return pl.BlockSpec(shape, lambda *_: (0,) * nd)


def _params(*sem):
    return pltpu.CompilerParams(dimension_semantics=sem, vmem_limit_bytes=VMEM_LIMIT)


def _prep_ab(x2, seq_len, gmix, w_in, qag, wq, kvag, wkv, qn_g, kn_g, qr_g, kr_g, bq_g, bk_g, tm=512):
    n = x2.shape[0]
    scale_a = (HEAD + 2 * ROPE_HALF) ** -0.5
    scale_b = HEAD ** -0.5
    zeros = lambda r, c: jnp.zeros((r, c), F32)
    d = w_in.shape[0]
    w0 = jnp.concatenate([w_in[:, 0:384], zeros(d, HEAD), w_in[:, 384:416], zeros(d, 32), w_in[:, 416:1184]],
                         axis=1).astype(BF16)
    wq_p = jnp.concatenate([wq.reshape(-1, 8, 96), jnp.zeros((wq.shape[0], 8, 32), F32)], axis=2)
    wq_p = wq_p.reshape(-1, 1024).astype(BF16)
    wkv_r = wkv.reshape(-1, 8, 128)
    wk_p = jnp.concatenate([wkv_r[:, :, :HEAD], jnp.zeros_like(wkv_r[:, :, :HEAD])], axis=2).reshape(-1, 1024)
    wv_p = jnp.concatenate([wkv_r[:, :, HEAD:], jnp.zeros_like(wkv_r[:, :, HEAD:])], axis=2).reshape(-1, 1024)
    wkv_p = jnp.concatenate([wk_p, wv_p], axis=1).astype(BF16)

    seg_q = [(LANES * h, HEAD) for h in range(8)] + [(LANES * h + HEAD, 32) for h in range(8)]
    gq = jnp.tile(jnp.concatenate([qn_g, qr_g, jnp.zeros((32,), F32)]), 8)[None, :] * (scale_a * LOG2E)
    seg_k = [(LANES * h, HEAD) for h in range(8)]
    gk = jnp.tile(jnp.concatenate([kn_g, jnp.zeros((HEAD,), F32)]), 8)[None, :]
    seg_r = [(HEAD, 32)]
    gr = jnp.concatenate([jnp.zeros((HEAD,), F32), kr_g, jnp.zeros((32,), F32)])[None, :]
    seg_bq = [(HEAD * h, HEAD) for h in range(8)]
    gbq = jnp.tile(bq_g, 8)[None, :] * (scale_b * LOG2E)
    seg_bk = [(HEAD * h, HEAD) for h in range(2)]
    gbk = jnp.tile(bk_g, 2)[None, :]

    pos = np.arange(seq_len)
    cosa, sina = _rope_tables([None, None, pos, None], seq_len)
    row, col = pos // GRID_W, pos % GRID_W
    cosb, sinb = _rope_tables([row, col, row, col], seq_len)

    consts = [gmix[None, :], w0, qag[None, :], wq_p, kvag[None, :], wkv_p,
              *_seg_mats(1024, seg_q), gq, *_seg_mats(1024, seg_k), gk, *_seg_mats(LANES, seg_r), gr,
              *_seg_mats(512, seg_bq), gbq, *_seg_mats(LANES, seg_bk), gbk]
    nsb = seq_len // tm
    rope_spec = pl.BlockSpec((tm, LANES), lambda i: (i % nsb, 0))
    row_spec = lambda w: pl.BlockSpec((tm, w), lambda i: (i, 0))
    out_widths = (1024, 1024, 1024, 512, 256, 256)
    return pl.pallas_call(
        _prep_ab_kernel,
        grid=(n // tm,),
        in_specs=[row_spec(x2.shape[1])] + [_full(c.shape) for c in consts] + [rope_spec] * 4,
        out_specs=[row_spec(w) for w in out_widths],
        out_shape=[jax.ShapeDtypeStruct((n, w), BF16) for w in out_widths],
        compiler_params=_params("parallel"),
        name="prep_ab",
    )(x2, *consts, cosa, sina, cosb, sinb)


def _prep_c(x2, gmix, w_in, q_g, k_g, tm=512):
    n = x2.shape[0]
    seg_q = [(HEAD * h, HEAD) for h in range(16)]
    gq = jnp.tile(q_g, 16)[None, :] * (HEAD ** -0.5 * LOG2E)
    seg_k = [(HEAD * h, HEAD) for h in range(4)]
    gk = jnp.tile(k_g, 4)[None, :]
    consts = [gmix[None, :], w_in.astype(BF16), *_seg_mats(1024, seg_q), gq, *_seg_mats(256, seg_k), gk]
    row_spec = lambda w: pl.BlockSpec((tm, w), lambda i: (i, 0))
    out_widths = (1024, 512, 512)
    return pl.pallas_call(
        _prep_c_kernel,
        grid=(n // tm,),
        in_specs=[row_spec(x2.shape[1])] + [_full(c.shape) for c in consts],
        out_specs=[row_spec(w) for w in out_widths],
        out_shape=[jax.ShapeDtypeStruct((n, w), BF16) for w in out_widths],
        compiler_params=_params("parallel"),
        name="prep_c",
    )(x2, *consts)


def _mla_attn(qa, ka, va, tq=512):
    b, s, _ = qa.shape
    return pl.pallas_call(
        _mla_attn_kernel,
        grid=(b, 4, s // tq),
        in_specs=[pl.BlockSpec((1, tq, 2 * LANES), lambda bi, hp, qi: (bi, qi, hp)),
                  pl.BlockSpec((1, s, 2 * LANES), lambda bi, hp, qi: (bi, 0, hp)),
                  pl.BlockSpec((1, s, 2 * LANES), lambda bi, hp, qi: (bi, 0, hp))],
        out_specs=pl.BlockSpec((1, tq, LANES), lambda bi, hp, qi: (bi, qi, hp)),
        out_shape=jax.ShapeDtypeStruct((b, s, 512), BF16),
        scratch_shapes=[pltpu.VMEM((tq, s), F32), pltpu.VMEM((tq, s), F32)],
        compiler_params=_params("parallel", "parallel", "parallel"),
        name="mla_attn",
    )(qa, ka, va)


def _gqa_attn(qb, kb, vb, tq=256):
    b, s, _ = qb.shape
    return pl.pallas_call(
        _gqa_attn_kernel,
        grid=(b, 2, s // tq),
        in_specs=[pl.BlockSpec((1, tq, 2 * LANES), lambda bi, g, qi: (bi, qi, g)),
                  pl.BlockSpec((1, s, LANES), lambda bi, g, qi: (bi, 0, g)),
                  pl.BlockSpec((1, s, LANES), lambda bi, g, qi: (bi, 0, g))],
        out_specs=pl.BlockSpec((1, tq, 2 * LANES), lambda bi, g, qi: (bi, qi, g)),
        out_shape=jax.ShapeDtypeStruct((b, s, 512), BF16),
        scratch_shapes=[pltpu.VMEM((2 * tq, s), F32), pltpu.VMEM((2 * tq, s), F32)],
        compiler_params=_params("parallel", "parallel", "parallel"),
        name="gqa_attn",
    )(qb, kb, vb)


def _rel_bias_table(rel_bias):
    span = Q_BLOCK + 2 * WINDOW
    rel = np.arange(span)[None, :] - WINDOW - np.arange(Q_BLOCK)[:, None]
    bucket = _t5_bucket_np(rel).astype(np.int32).reshape(1, -1)
    band = (np.abs(rel) <= WINDOW).astype(np.int32).reshape(1, -1)
    heads = rel_bias.shape[1]
    cols = bucket.shape[1]
    chunk = cols // 8
    table = pl.pallas_call(
        _rel_bias_kernel,
        grid=(8,),
        in_specs=[pl.BlockSpec((1, chunk), lambda i: (0, i)), pl.BlockSpec((1, chunk), lambda i: (0, i)),
                  _full((heads, REL_BUCKETS))],
        out_specs=pl.BlockSpec((heads, chunk), lambda i: (0, i)),
        out_shape=jax.ShapeDtypeStruct((heads, cols), F32),
        compiler_params=_params("parallel"),
        name="rel_bias",
    )(jnp.asarray(bucket), jnp.asarray(band), rel_bias.T)
    return table.reshape(heads, Q_BLOCK, span)


def _win_attn(qc, kc, vc, sink, bias, nq=2):
    b, s, _ = qc.shape
    tq = Q_BLOCK
    nb = s // tq
    prev = lambda bi, i: (bi, jnp.maximum(i * nq - 1, 0), 0)
    cur = lambda bi, i: (bi, i, 0)
    nxt = lambda bi, i: (bi, jnp.minimum((i + 1) * nq, nb - 1), 0)
    edge_spec = lambda im: pl.BlockSpec((1, tq, 4 * LANES), im)
    cur_spec = pl.BlockSpec((1, nq * tq, 4 * LANES), cur)
    return pl.pallas_call(
        functools.partial(_win_attn_kernel, seq_len=s),
        grid=(b, nb // nq),
        in_specs=[pl.BlockSpec(memory_space=pltpu.SMEM),
                  pl.BlockSpec((1, nq * tq, 8 * LANES), cur),
                  edge_spec(prev), cur_spec, edge_spec(nxt),
                  edge_spec(prev), cur_spec, edge_spec(nxt),
                  _full(bias.shape)],
        out_specs=pl.BlockSpec((1, nq * tq, 8 * LANES), cur),
        scratch_shapes=[pltpu.VMEM((4 * nq, 4 * tq, 3 * tq), F32), pltpu.VMEM((4 * nq, 4 * tq, 3 * tq), BF16),
                        pltpu.VMEM((4 * nq, 4 * tq, LANES), F32)],
        out_shape=jax.ShapeDtypeStruct((b, s, 1024), BF16),
        compiler_params=_params("parallel", "parallel"),
        name="win_attn",
    )(sink, qc, kc, kc, kc, vc, vc, vc, bias)


def _post(parts, ws, x2, gffn, w_group, b_group, w_router, b_router, tm=1024, sub=256):
    n, d = x2.shape
    wr = jnp.concatenate([w_group.T, w_router.T, jnp.zeros((ROUTER_ROWS - N_GROUPS - N_EXPERTS, d), F32)], axis=0)
    wrh = wr.astype(BF16)
    wrl = (wr - wrh.astype(F32)).astype(BF16)
    rb = jnp.concatenate([b_group, b_router, jnp.zeros((ROUTER_ROWS - N_GROUPS - N_EXPERTS,), F32)])[:, None]
    row_spec = lambda w: pl.BlockSpec((tm, w), lambda i: (i, 0))
    lane_spec = lambda r: pl.BlockSpec((r, tm), lambda i: (0, i))
    ws = [w.astype(BF16) for w in ws]
    tri = jnp.asarray(np.triu(np.ones((sub, sub), np.float32), 1), BF16)
    return pl.pallas_call(
        functools.partial(_post_kernel, n_parts=len(parts)),
        grid=(n // tm,),
        in_specs=[row_spec(p.shape[1]) for p in parts] + [_full(w.shape) for w in ws]
                 + [row_spec(d), _full((1, d)), _full(wrh.shape), _full(wrl.shape), _full(rb.shape),
                    _full(tri.shape)],
        out_specs=[pl.BlockSpec((tm * TOKEN_TILE_ROWS, LANES), lambda i: (i, 0)),
                   lane_spec(1), lane_spec(1), _full((ROUTER_ROWS, LANES))],
        out_shape=[jax.ShapeDtypeStruct((n * TOKEN_TILE_ROWS, LANES), F32),
                   jax.ShapeDtypeStruct((1, n), jnp.int32), jax.ShapeDtypeStruct((1, n), jnp.int32),
                   jax.ShapeDtypeStruct((ROUTER_ROWS, LANES), F32)],
        scratch_shapes=[pltpu.VMEM((ROUTER_ROWS, LANES), F32)],
        compiler_params=_params("arbitrary"),
        name="post",
    )(*parts, *ws, x2, gffn[None, :], wrh, wrl, rb, tri)


def _route(bucket, rank, counts, tmm):
    n = bucket.shape[1]
    n_tiles_max = n // tmm + N_BUCKETS
    n_slots = n_tiles_max * tmm
    bucket, rank = bucket[0], rank[0]
    cnt = counts[:N_BUCKETS, 0].astype(jnp.int32)
    padded = (cnt + tmm - 1) // tmm * tmm
    ends = jnp.cumsum(padded)
    starts = ends - padded
    pos = starts[bucket] + rank
    n_tiles = ends[-1] // tmm
    tile = jnp.minimum(jnp.arange(n_tiles_max, dtype=jnp.int32), n_tiles - 1)
    tile_bucket = jnp.sum((tile * tmm)[:, None] >= ends[None, :], axis=1).astype(jnp.int32)
    grp, pair = tile_bucket // PAIRS_PER_GROUP, tile_bucket % PAIRS_PER_GROUP
    tile_ea = grp * EXPERTS_PER_GROUP + jnp.asarray(PAIR_LO, jnp.int32)[pair]
    tile_eb = grp * EXPERTS_PER_GROUP + jnp.asarray(PAIR_HI, jnp.int32)[pair]
    token = jnp.arange(n, dtype=jnp.int32)
    src = (jnp.arange(n_slots + tmm, dtype=jnp.int32) % n).at[pos].set(token)
    return pos, src.reshape(n_tiles_max + 1, 1, tmm), tile_ea, tile_eb, n_tiles.reshape(1)


def _moe_sparse(x1t, gffn, w_group, b_group, w_router, b_router, src, tile_ea, tile_eb, n_tiles,
                w_gate, w_up, w_down, layer, tmm):
    n_tiles_max = src.shape[0] - 1
    _, ne, d, de = w_gate.shape
    idx_spec = lambda off: pl.BlockSpec((1, 1, tmm), lambda t, ea, eb, nt: (t + off, 0, 0),
                                        memory_space=pltpu.SMEM)
    up_spec = lambda which: pl.BlockSpec((1, 1, d, de), lambda t, ea, eb, nt: (layer, (ea, eb)[which][t], 0, 0))
    down_spec = lambda which: pl.BlockSpec((1, 1, de, d), lambda t, ea, eb, nt: (layer, (ea, eb)[which][t], 0, 0))
    wg, wu, wd = w_gate, w_up, w_down
    pad = LANES - N_GROUPS - N_EXPERTS
    wr = jnp.concatenate([w_group, w_router, jnp.zeros((d, pad), F32)], axis=1)
    rb = jnp.concatenate([b_group, b_router, jnp.zeros((pad,), F32)])[None, :]
    return pl.pallas_call(
        _moe_sparse_kernel,
        grid_spec=pltpu.PrefetchScalarGridSpec(
            num_scalar_prefetch=3,
            grid=(n_tiles_max,),
            in_specs=[idx_spec(0), idx_spec(1),
                      pl.BlockSpec((1, d), lambda t, ea, eb, nt: (0, 0)),
                      pl.BlockSpec((d, LANES), lambda t, ea, eb, nt: (0, 0)),
                      pl.BlockSpec((1, LANES), lambda t, ea, eb, nt: (0, 0)),
                      pl.BlockSpec(memory_space=pl.ANY),
                      up_spec(0), up_spec(0), down_spec(0), up_spec(1), up_spec(1), down_spec(1)],
            out_specs=pl.BlockSpec((tmm * TOKEN_TILE_ROWS, LANES), lambda t, ea, eb, nt: (t, 0)),
            scratch_shapes=[pltpu.VMEM((2, tmm * TOKEN_TILE_ROWS, LANES), F32), pltpu.SemaphoreType.DMA((2,))]),
        out_shape=jax.ShapeDtypeStruct((n_tiles_max * tmm * TOKEN_TILE_ROWS, LANES), F32),
        compiler_params=_params("arbitrary"),
        name="moe_sparse",
    )(tile_ea, tile_eb, n_tiles, src, src, gffn[None, :], wr, rb, x1t, wg, wu, wd, wg, wu, wd)


def _unpermute(sorted_rows, pos, tm=256):
    n = pos.shape[0]
    d = TOKEN_TILE_ROWS * LANES
    pos2 = jnp.concatenate([pos, jnp.zeros((tm,), jnp.int32)]).reshape(n // tm + 1, 1, tm)
    idx_spec = lambda off: pl.BlockSpec((1, 1, tm), lambda t: (t + off, 0, 0), memory_space=pltpu.SMEM)
    return pl.pallas_call(
        _unpermute_kernel,
        grid=(n // tm,),
        in_specs=[idx_spec(0), idx_spec(1), pl.BlockSpec(memory_space=pl.ANY)],
        out_specs=pl.BlockSpec((tm, d), lambda t: (t, 0)),
        out_shape=jax.ShapeDtypeStruct((n, d), F32),
        scratch_shapes=[pltpu.VMEM((2, tm * TOKEN_TILE_ROWS, LANES), F32), pltpu.SemaphoreType.DMA((2,))],
        compiler_params=_params("arbitrary"),
        name="unpermute",
    )(pos2, pos2, sorted_rows)


def kernel(x, mix_norm, ffn_norm, w_in_ab, mla_q_a_norm, mla_w_q_up, mla_kv_a_norm, mla_w_kv_up, mla_qn_gain, mla_kn_gain, mla_qr_gain, mla_kr_gain, gqa_q_gain, gqa_k_gain, w_out_ab, w_in_c, win_q_gain, win_k_gain, win_sink, w_out_c, rel_bias, moe_w_group, moe_b_group, moe_w_router, moe_b_router, moe_w_gate, moe_w_up, moe_w_down):
    b, s, d = x.shape
    n = b * s
    depth = mix_norm.shape[0]
    x2 = x.reshape(n, d)
    bias = None
    for layer in range(depth):
        i = layer // 2
        if layer % 2 == 0:
            qa, ka, va, qb, kb, vb = _prep_ab(
                x2, s, mix_norm[layer], w_in_ab[i], mla_q_a_norm[i], mla_w_q_up[i], mla_kv_a_norm[i],
                mla_w_kv_up[i], mla_qn_gain[i], mla_kn_gain[i], mla_qr_gain[i], mla_kr_gain[i],
                gqa_q_gain[i], gqa_k_gain[i])
            r3 = lambda t: t.reshape(b, s, t.shape[1])
            out_a = _mla_attn(r3(qa), r3(ka), r3(va)).reshape(n, 512)
            out_b = _gqa_attn(r3(qb), r3(kb), r3(vb)).reshape(n, 512)
            parts, ws = [out_a, out_b], [w_out_ab[i][:512], w_out_ab[i][512:]]
        else:
            if bias is None:
                bias = _rel_bias_table(rel_bias)
            qc, kc, vc = _prep_c(x2, mix_norm[layer], w_in_c[i], win_q_gain[i], win_k_gain[i])
            r3 = lambda t: t.reshape(b, s, t.shape[1])
            out_c = _win_attn(r3(qc), r3(kc), r3(vc), win_sink[i], bias).reshape(n, 1024)
            parts, ws = [out_c], [w_out_c[i]]
        router = (moe_w_group[layer], moe_b_group[layer], moe_w_router[layer], moe_b_router[layer])
        x1t, bucket, rank, counts = _post(parts, ws, x2, ffn_norm[layer], *router)
        pos, src, tile_ea, tile_eb, n_tiles = _route(bucket, rank, counts, MOE_TILE)
        y_sorted = _moe_sparse(x1t, ffn_norm[layer], *router, src, tile_ea, tile_eb, n_tiles,
                               moe_w_gate, moe_w_up, moe_w_down, layer, MOE_TILE)
        x2 = _unpermute(y_sorted, pos)
    return x2.reshape(b, s, d)
```

```python
import functools
import math

import numpy as np
import jax
import jax.numpy as jnp
from jax import lax
from jax.experimental import pallas as pl
from jax.experimental.pallas import tpu as pltpu

F32 = jnp.float32
BF16 = jnp.bfloat16

EPS = 1e-6
ROPE_THETA = 10000.0
LANES = 128
HEAD = 64
ROPE_HALF = 16
GRID_W = 64
WINDOW = 128
Q_BLOCK = 128
REL_BUCKETS = 32
REL_MAX_DIST = 128
N_GROUPS = 4
EXPERTS_PER_GROUP = 4
N_EXPERTS = 16
PAIRS_PER_GROUP = 6
N_BUCKETS = N_GROUPS * PAIRS_PER_GROUP
PAIR_LO = (0, 0, 0, 1, 1, 2)
PAIR_HI = (1, 2, 3, 2, 3, 3)
TOKEN_TILE_ROWS = 8
KEY_CHUNK = 512
LOG2E = math.log2(math.e)
PREP_SUB = 256
WIN_ROW_TILE = 32
GATHER_UNROLL = 8
MOE_TILE = 256
ROUTER_ROWS = 32
VMEM_LIMIT = 56 * 1024 * 1024

_NT = (((1,), (1,)), ((), ()))


def _dot(a, b):
    return jnp.dot(a, b, preferred_element_type=F32)


def _dot_nt(a, b):
    return lax.dot_general(a, b, _NT, preferred_element_type=F32)


def _split_bf16(a):
    hi = a.astype(BF16)
    lo = (a - hi.astype(F32)).astype(BF16)
    return hi, lo


def _row_rmsnorm(t, gain):
    return t * lax.rsqrt(jnp.mean(t * t, axis=-1, keepdims=True) + EPS) * gain


def _seg_rmsnorm(t, mseg, msegt, invlen, gain):
    sums = _dot((t * t).astype(BF16), mseg)
    inv = lax.rsqrt(sums * invlen + EPS)
    ihi, ilo = _split_bf16(inv)
    scale = _dot(jnp.concatenate([ihi, ilo], axis=1), msegt)
    return t * scale * gain


def _rope128(t, cos, sin_signed, first_half):
    up = pltpu.roll(t, LANES - ROPE_HALF, 1)
    dn = pltpu.roll(t, ROPE_HALF, 1)
    return t * cos + jnp.where(first_half, up, dn) * sin_signed


def _dup_halves(blk, lo):
    sw = pltpu.roll(blk, HEAD, 1)
    return jnp.where(lo, blk, sw), jnp.where(lo, sw, blk)


def _lane_masks():
    lane = lax.broadcasted_iota(jnp.int32, (1, LANES), 1)
    return lane < HEAD, (lane % (2 * ROPE_HALF)) < ROPE_HALF


def _by_sub_tiles(rows_fn, refs, n_const):
    row_refs = (refs[0],) + tuple(refs[1 + n_const:])
    for h in range(refs[0].shape[0] // PREP_SUB):
        view = [r.at[pl.ds(PREP_SUB * h, PREP_SUB)] for r in row_refs]
        rows_fn(view[0], *refs[1:1 + n_const], *view[1:])


def _prep_ab_kernel(*refs):
    _by_sub_tiles(_prep_ab_rows, refs, 26)


def _prep_c_kernel(*refs):
    _by_sub_tiles(_prep_c_rows, refs, 10)


def _prep_ab_rows(x_ref, gmix_ref, w0_ref, qag_ref, wq_ref, kvag_ref, wkv_ref,
                    mq_ref, mqt_ref, ilq_ref, gq_ref,
                    mk_ref, mkt_ref, ilk_ref, gk_ref,
                    mr_ref, mrt_ref, ilr_ref, gr_ref,
                    mbq_ref, mbqt_ref, ilbq_ref, gbq_ref,
                    mbk_ref, mbkt_ref, ilbk_ref, gbk_ref,
                    cosa_ref, sina_ref, cosb_ref, sinb_ref,
                    qa_ref, ka_ref, va_ref, qb_ref, kb_ref, vb_ref):
    lo, first_half = _lane_masks()
    h = _row_rmsnorm(x_ref[...], gmix_ref[...]).astype(BF16)
    proj = _dot(h, w0_ref[...])
    cosa, sina = cosa_ref[...], sina_ref[...]
    cosb, sinb = cosb_ref[...], sinb_ref[...]

    qn = _row_rmsnorm(proj[:, 0:256], qag_ref[...]).astype(BF16)
    q = _seg_rmsnorm(_dot(qn, wq_ref[...]), mq_ref[...], mqt_ref[...], ilq_ref[...], gq_ref[...])
    for hd in range(8):
        sl = slice(LANES * hd, LANES * (hd + 1))
        qa_ref[:, sl] = _rope128(q[:, sl], cosa, sina, first_half).astype(BF16)

    kvn = _row_rmsnorm(proj[:, 256:384], kvag_ref[...]).astype(BF16)
    kv = _dot(kvn, wkv_ref[...])
    ones_hi = jnp.where(lo, 0.0, 1.0)
    kn = _seg_rmsnorm(kv[:, 0:1024], mk_ref[...], mkt_ref[...], ilk_ref[...], gk_ref[...])
    kr = _seg_rmsnorm(proj[:, 384:512], mr_ref[...], mrt_ref[...], ilr_ref[...], gr_ref[...])
    kr = _rope128(kr, cosa, sina, first_half)
    for hd in range(8):
        sl = slice(LANES * hd, LANES * (hd + 1))
        ka_ref[:, sl] = (kn[:, sl] + kr).astype(BF16)
        va_ref[:, sl] = (kv[:, 1024 + LANES * hd:1024 + LANES * (hd + 1)] + ones_hi).astype(BF16)

    bq = _seg_rmsnorm(proj[:, 512:1024], mbq_ref[...], mbqt_ref[...], ilbq_ref[...], gbq_ref[...])
    for blk in range(4):
        sl = slice(LANES * blk, LANES * (blk + 1))
        qb_ref[:, sl] = _rope128(bq[:, sl], cosb, sinb, first_half).astype(BF16)
    bk = _seg_rmsnorm(proj[:, 1024:1152], mbk_ref[...], mbkt_ref[...], ilbk_ref[...], gbk_ref[...])
    bk = _rope128(bk, cosb, sinb, first_half)
    k0, k1 = _dup_halves(bk, lo)
    kb_ref[:, 0:LANES] = k0.astype(BF16)
    kb_ref[:, LANES:2 * LANES] = k1.astype(BF16)
    bv = proj[:, 1152:1280]
    vb_ref[:, 0:LANES] = jnp.where(lo, bv, 1.0).astype(BF16)
    vb_ref[:, LANES:2 * LANES] = jnp.where(lo, pltpu.roll(bv, HEAD, 1), 1.0).astype(BF16)


def _prep_c_rows(x_ref, gmix_ref, w_ref,
                   mq_ref, mqt_ref, ilq_ref, gq_ref,
                   mk_ref, mkt_ref, ilk_ref, gk_ref,
                   qc_ref, kc_ref, vc_ref):
    lo, _ = _lane_masks()
    h = _row_rmsnorm(x_ref[...], gmix_ref[...]).astype(BF16)
    proj = _dot(h, w_ref[...])
    q = _seg_rmsnorm(proj[:, 0:1024], mq_ref[...], mqt_ref[...], ilq_ref[...], gq_ref[...])
    qc_ref[...] = q.astype(BF16)
    k = _seg_rmsnorm(proj[:, 1024:1280], mk_ref[...], mkt_ref[...], ilk_ref[...], gk_ref[...])
    for blk in range(2):
        sl = slice(LANES * blk, LANES * (blk + 1))
        k0, k1 = _dup_halves(k[:, sl], lo)
        kc_ref[:, 2 * blk * LANES:(2 * blk + 1) * LANES] = k0.astype(BF16)
        kc_ref[:, (2 * blk + 1) * LANES:(2 * blk + 2) * LANES] = k1.astype(BF16)
        v = proj[:, 1280 + LANES * blk:1280 + LANES * (blk + 1)]
        vc_ref[:, 2 * blk * LANES:(2 * blk + 1) * LANES] = jnp.where(lo, v, 1.0).astype(BF16)
        vc_ref[:, (2 * blk + 1) * LANES:(2 * blk + 2) * LANES] = jnp.where(
            lo, pltpu.roll(v, HEAD, 1), 1.0).astype(BF16)


def _softmax_pv(s, v):
    m = jnp.max(s, axis=-1, keepdims=True)
    p = jnp.exp(s - m)
    l = jnp.sum(p, axis=-1, keepdims=True)
    return _dot(p.astype(BF16), v) / l


def _lane_chunk_reduce(op, t):
    out = t[:, 0:LANES]
    for j in range(1, t.shape[1] // LANES):
        out = op(out, t[:, LANES * j:LANES * (j + 1)])
    return out


def _scores_phase(q, k_ref, lanes, s_buf):
    m_part = None
    for c in range(k_ref.shape[1] // KEY_CHUNK):
        ks = slice(KEY_CHUNK * c, KEY_CHUNK * (c + 1))
        s_c = _dot_nt(q, k_ref[0, ks, lanes])
        s_buf[:, ks] = s_c
        mc = _lane_chunk_reduce(jnp.maximum, s_c)
        m_part = mc if m_part is None else jnp.maximum(m_part, mc)
    return jnp.max(m_part, axis=-1, keepdims=True)


def _pv_phase(s_buf, m, v_ref, lanes):
    lo, _ = _lane_masks()
    acc = None
    for c in range(v_ref.shape[1] // KEY_CHUNK):
        ks = slice(KEY_CHUNK * c, KEY_CHUNK * (c + 1))
        p = jnp.exp2(s_buf[:, ks] - m)
        pv = _dot(p.astype(BF16), v_ref[0, ks, lanes])
        acc = pv if acc is None else acc + pv
    return acc / jnp.where(lo, pltpu.roll(acc, HEAD, 1), 1.0)


def _pair_heads(o_even, o_odd):
    lo, _ = _lane_masks()
    return jnp.where(lo, o_even, pltpu.roll(o_odd, HEAD, 1))


def _mla_attn_kernel(q_ref, k_ref, v_ref, o_ref, s0_ref, s1_ref):
    first, second = slice(0, LANES), slice(LANES, 2 * LANES)
    m0 = _scores_phase(q_ref[0, :, first], k_ref, first, s0_ref)
    m1 = _scores_phase(q_ref[0, :, second], k_ref, second, s1_ref)
    o0 = _pv_phase(s0_ref, m0, v_ref, first)
    o1 = _pv_phase(s1_ref, m1, v_ref, second)
    o_ref[0] = _pair_heads(o0, o1).astype(o_ref.dtype)


def _gqa_attn_kernel(q_ref, k_ref, v_ref, o_ref, s0_ref, s1_ref):
    lo, _ = _lane_masks()
    tq = q_ref.shape[1]
    hi = jnp.logical_not(lo)
    ms = []
    for j, s_ref in enumerate((s0_ref, s1_ref)):
        blk = q_ref[0, :, LANES * j:LANES * (j + 1)]
        zero = jnp.zeros_like(blk)
        qs = jnp.concatenate([jnp.where(lo, blk, zero), jnp.where(hi, blk, zero)], axis=0)
        ms.append(_scores_phase(qs, k_ref, slice(0, LANES), s_ref))
    for j, s_ref in enumerate((s0_ref, s1_ref)):
        o = _pv_phase(s_ref, ms[j], v_ref, slice(0, LANES))
        o_ref[0, :, LANES * j:LANES * (j + 1)] = _pair_heads(o[0:tq], o[tq:2 * tq]).astype(o_ref.dtype)


def _win_attn_kernel(sink_ref, q_ref, kp_ref, kc_ref, kn_ref, vp_ref, vc_ref, vn_ref, bias_ref, o_ref,
                     s_scr, p_scr, m_scr, *, seq_len):
    lo, _ = _lane_masks()
    tq = Q_BLOCK
    nq = q_ref.shape[1] // tq
    i = pl.program_id(1)
    kj = lax.broadcasted_iota(jnp.int32, (1, 3 * tq), 1)
    units = [(g, u) for g in range(4) for u in range(nq)]
    for idx, (g, u) in enumerate(units):
        sl = slice(LANES * g, LANES * (g + 1))
        kcat = jnp.concatenate([kp_ref[0, :, sl], kc_ref[0, :, sl], kn_ref[0, :, sl]], axis=0)
        parts = []
        for a in range(4):
            blk = q_ref[0, tq * u:tq * (u + 1), LANES * (2 * g + a // 2):LANES * (2 * g + a // 2 + 1)]
            keep = lo if a % 2 == 0 else jnp.logical_not(lo)
            parts.append(jnp.where(keep, blk, jnp.zeros_like(blk)))
        s_scr[idx] = _dot_nt(jnp.concatenate(parts, axis=0), kcat[tq * u:tq * (u + 3)])
    for idx, (g, u) in enumerate(units):
        key_pos = (i * nq + u - 1) * tq + kj
        valid = jnp.logical_and(key_pos >= 0, key_pos < seq_len)
        for rt in range(4 * tq // WIN_ROW_TILE):
            a, r0 = divmod(rt * WIN_ROW_TILE, tq)
            rows = slice(rt * WIN_ROW_TILE, (rt + 1) * WIN_ROW_TILE)
            s = s_scr[idx, rows, :] + bias_ref[4 * g + a, r0:r0 + WIN_ROW_TILE, :]
            s = jnp.where(valid, s, -jnp.inf)
            m = jnp.maximum(jnp.max(_lane_chunk_reduce(jnp.maximum, s), axis=-1, keepdims=True),
                            sink_ref[4 * g + a] * LOG2E)
            p_scr[idx, rows, :] = jnp.exp2(s - m).astype(BF16)
            m_scr[idx, rows, :] = jnp.broadcast_to(m, (WIN_ROW_TILE, LANES))
    for idx, (g, u) in enumerate(units):
        sl = slice(LANES * g, LANES * (g + 1))
        vcat = jnp.concatenate([vp_ref[0, :, sl], vc_ref[0, :, sl], vn_ref[0, :, sl]], axis=0)
        acc = _dot(p_scr[idx], vcat[tq * u:tq * (u + 3)])
        outs = []
        for a in range(4):
            rows = slice(tq * a, tq * (a + 1))
            sink_term = jnp.exp2(sink_ref[4 * g + a] * LOG2E - m_scr[idx, rows, :])
            den = jnp.where(lo, pltpu.roll(acc[rows], HEAD, 1) + sink_term, 1.0)
            outs.append(acc[rows] / den)
        for j in range(2):
            blk = 2 * g + j
            o_ref[0, tq * u:tq * (u + 1), LANES * blk:LANES * (blk + 1)] = _pair_heads(
                outs[2 * j], outs[2 * j + 1]).astype(o_ref.dtype)


def _rel_bias_kernel(bucket_ref, band_ref, relt_ref, o_ref):
    bucket = bucket_ref[...]
    acc = jnp.zeros(o_ref.shape, F32)
    for r in range(REL_BUCKETS):
        acc = acc + jnp.where(bucket == r, relt_ref[:, r:r + 1], 0.0)
    o_ref[...] = jnp.where(band_ref[...] > 0, acc * LOG2E, -jnp.inf)


def _to_tiles(ref, val, first_token=0):
    for c in range(TOKEN_TILE_ROWS):
        ref[pl.ds(first_token * TOKEN_TILE_ROWS + c, val.shape[0], stride=TOKEN_TILE_ROWS), :] = (
            val[:, LANES * c:LANES * (c + 1)])


def _from_tiles(ref):
    rows = ref.shape[0] // TOKEN_TILE_ROWS
    return jnp.concatenate([ref[pl.ds(c, rows, stride=TOKEN_TILE_ROWS), :] for c in range(TOKEN_TILE_ROWS)],
                           axis=1)


def _route_bucket(xn, wh, wl, rb):
    xh, xl = _split_bf16(xn)
    logit = _dot_nt(wh, xh) + _dot_nt(wh, xl) + _dot_nt(wl, xh) + rb
    g = [logit[r:r + 1, :] for r in range(N_GROUPS)]
    gmax = jnp.maximum(jnp.maximum(g[0], g[1]), jnp.maximum(g[2], g[3]))
    gidx = jnp.where(g[0] == gmax, 0, jnp.where(g[1] == gmax, 1, jnp.where(g[2] == gmax, 2, 3)))
    e = []
    for j in range(EXPERTS_PER_GROUP):
        rows = [logit[N_GROUPS + EXPERTS_PER_GROUP * gg + j:N_GROUPS + EXPERTS_PER_GROUP * gg + j + 1, :]
                for gg in range(N_GROUPS)]
        e.append(jnp.where(gidx == 0, rows[0], jnp.where(gidx == 1, rows[1],
                                                         jnp.where(gidx == 2, rows[2], rows[3]))))
    emax = jnp.maximum(jnp.maximum(e[0], e[1]), jnp.maximum(e[2], e[3]))
    ex = [jnp.exp(ej - emax) for ej in e]
    esum = ex[0] + ex[1] + ex[2] + ex[3]
    pr = [exj / esum for exj in ex]
    p1 = jnp.maximum(jnp.maximum(pr[0], pr[1]), jnp.maximum(pr[2], pr[3]))
    i1 = jnp.where(pr[0] == p1, 0, jnp.where(pr[1] == p1, 1, jnp.where(pr[2] == p1, 2, 3)))
    rest = [jnp.where(i1 == j, -1.0, pr[j]) for j in range(EXPERTS_PER_GROUP)]
    p2 = jnp.maximum(jnp.maximum(rest[0], rest[1]), jnp.maximum(rest[2], rest[3]))
    i2 = jnp.where(rest[0] == p2, 0, jnp.where(rest[1] == p2, 1, jnp.where(rest[2] == p2, 2, 3)))
    lo_e = jnp.minimum(i1, i2)
    hi_e = jnp.maximum(i1, i2)
    pair = jnp.where(lo_e == 0, hi_e - 1, jnp.where(lo_e == 1, hi_e + 1, 5))
    return gidx * PAIRS_PER_GROUP + pair


def _post_kernel(*refs, n_parts):
    parts = refs[:n_parts]
    ws = refs[n_parts:2 * n_parts]
    (x_ref, gffn_ref, wrh_ref, wrl_ref, rb_ref, tri_ref,
     x1t_ref, bucket_ref, rank_ref, counts_ref, carry_ref) = refs[2 * n_parts:]

    @pl.when(pl.program_id(0) == 0)
    def _():
        carry_ref[...] = jnp.zeros_like(carry_ref)

    sub = tri_ref.shape[0]
    buckets = []
    for h in range(x_ref.shape[0] // sub):
        rows = slice(sub * h, sub * (h + 1))
        acc = x_ref[rows, :]
        for p_ref, w_ref in zip(parts, ws):
            acc = acc + _dot(p_ref[rows, :], w_ref[...])
        _to_tiles(x1t_ref, acc, sub * h)
        bucket = _route_bucket(_row_rmsnorm(acc, gffn_ref[...]), wrh_ref[...], wrl_ref[...], rb_ref[...])
        bucket_ref[:, rows] = bucket
        buckets.append(bucket)
    for h, bucket in enumerate(buckets):
        rows = slice(sub * h, sub * (h + 1))
        onehot = (lax.broadcasted_iota(jnp.int32, (ROUTER_ROWS, sub), 0) == bucket).astype(F32)
        before = _dot(onehot.astype(BF16), tri_ref[...]) + carry_ref[:, 0:1]
        rank_ref[:, rows] = jnp.sum(onehot * before, axis=0, keepdims=True).astype(jnp.int32)
        carry_ref[...] = carry_ref[...] + jnp.sum(onehot, axis=1, keepdims=True)
    counts_ref[...] = carry_ref[...]


def _start_row_gather(idx_ref, table_hbm, dst, sem):
    def body(blk, carry):
        for j in range(GATHER_UNROLL):
            r = blk * GATHER_UNROLL + j
            src_row = pl.multiple_of(idx_ref[0, 0, r] * TOKEN_TILE_ROWS, TOKEN_TILE_ROWS)
            dst_row = pl.multiple_of(r * TOKEN_TILE_ROWS, TOKEN_TILE_ROWS)
            pltpu.make_async_copy(table_hbm.at[pl.ds(src_row, TOKEN_TILE_ROWS)],
                                  dst.at[pl.ds(dst_row, TOKEN_TILE_ROWS)], sem).start()
        return carry
    lax.fori_loop(0, dst.shape[0] // (TOKEN_TILE_ROWS * GATHER_UNROLL), body, 0)


def _wait_row_gather(table_hbm, dst, sem):
    pltpu.make_async_copy(table_hbm.at[pl.ds(0, dst.shape[0])], dst, sem).wait()


def _tile_gates(logit, ea, eb):
    lane = lax.broadcasted_iota(jnp.int32, (1, LANES), 1)
    pick = lambda idx: jnp.sum(jnp.where(lane == idx, logit, 0.0), axis=-1, keepdims=True)
    glog = jnp.where(lane < N_GROUPS, logit, -jnp.inf)
    gmax = jnp.max(glog, axis=-1, keepdims=True)
    gsum = jnp.sum(jnp.exp(glog - gmax), axis=-1, keepdims=True)
    g_p = jnp.exp(pick(ea // EXPERTS_PER_GROUP) - gmax) / gsum
    la, lb = pick(N_GROUPS + ea), pick(N_GROUPS + eb)
    top = jnp.maximum(la, lb)
    pa, pb = jnp.exp(la - top), jnp.exp(lb - top)
    return g_p * pa / (pa + pb), g_p * pb / (pa + pb)


def _moe_sparse_kernel(ea_ref, eb_ref, nt_ref, src_cur_ref, src_nxt_ref, gffn_ref, wr_ref, rb_ref, x1t_hbm,
                       wga_ref, wua_ref, wda_ref, wgb_ref, wub_ref, wdb_ref, o_ref, buf, sem):
    t = pl.program_id(0)
    nt = nt_ref[0]
    slot = t % 2

    @pl.when(t == 0)
    def _():
        _start_row_gather(src_cur_ref, x1t_hbm, buf.at[0], sem.at[0])

    @pl.when(t + 1 < nt)
    def _():
        _start_row_gather(src_nxt_ref, x1t_hbm, buf.at[1 - slot], sem.at[1 - slot])

    @pl.when(t < nt)
    def _():
        _wait_row_gather(x1t_hbm, buf.at[slot], sem.at[slot])
        x = _from_tiles(buf.at[slot])
        xn = _row_rmsnorm(x, gffn_ref[...])
        gate_a, gate_b = _tile_gates(_dot(xn, wr_ref[...]) + rb_ref[...], ea_ref[t], eb_ref[t])
        hid_a = jax.nn.silu(_dot(xn, wga_ref[0, 0])) * _dot(xn, wua_ref[0, 0])
        hid_b = jax.nn.silu(_dot(xn, wgb_ref[0, 0])) * _dot(xn, wub_ref[0, 0])
        out = x + gate_a * _dot(hid_a, wda_ref[0, 0]) + gate_b * _dot(hid_b, wdb_ref[0, 0])
        _to_tiles(o_ref, out)

    @pl.when(t >= nt)
    def _():
        o_ref[...] = jnp.zeros_like(o_ref)


def _unpermute_kernel(pos_cur_ref, pos_nxt_ref, sorted_hbm, o_ref, buf, sem):
    t = pl.program_id(0)
    slot = t % 2

    @pl.when(t == 0)
    def _():
        _start_row_gather(pos_cur_ref, sorted_hbm, buf.at[0], sem.at[0])

    @pl.when(t + 1 < pl.num_programs(0))
    def _():
        _start_row_gather(pos_nxt_ref, sorted_hbm, buf.at[1 - slot], sem.at[1 - slot])

    _wait_row_gather(sorted_hbm, buf.at[slot], sem.at[slot])
    o_ref[...] = _from_tiles(buf.at[slot])


def _seg_mats(width, segments):
    m = np.zeros((width, LANES), np.float32)
    invlen = np.ones((1, LANES), np.float32)
    for c, (start, length) in enumerate(segments):
        m[start:start + length, c] = 1.0
        invlen[0, c] = 1.0 / length
    return jnp.asarray(m, BF16), jnp.asarray(np.concatenate([m.T, m.T], axis=0), BF16), jnp.asarray(invlen)


def _rope_tables(pos_list, seq_len):
    inv = np.float32(ROPE_THETA) ** (-np.arange(0, 2 * ROPE_HALF, 2, dtype=np.float32) / np.float32(2 * ROPE_HALF))
    cos_cols, sin_cols = [], []
    for pos in pos_list:
        if pos is None:
            cos_cols.append(np.ones((seq_len, 2 * ROPE_HALF), np.float32))
            sin_cols.append(np.zeros((seq_len, 2 * ROPE_HALF), np.float32))
        else:
            ang = pos.astype(np.float32)[:, None] * inv[None, :]
            c, s = np.cos(ang), np.sin(ang)
            cos_cols.append(np.concatenate([c, c], axis=1))
            sin_cols.append(np.concatenate([-s, s], axis=1))
    return jnp.asarray(np.concatenate(cos_cols, axis=1)), jnp.asarray(np.concatenate(sin_cols, axis=1))


def _t5_bucket_np(rel):
    nb = REL_BUCKETS // 2
    max_exact = nb // 2
    ret = np.where(rel > 0, nb, 0)
    n = np.abs(rel)
    nf = np.maximum(n, 1).astype(np.float32)
    large = max_exact + (np.log(nf / np.float32(max_exact)) / np.float32(math.log(REL_MAX_DIST / max_exact))
                         * np.float32(nb - max_exact)).astype(np.int32)
    large = np.minimum(large, nb - 1)
    return ret + np.where(n < max_exact, n, large)


def _full(shape):
    nd = len(shape)
    return pl.BlockSpec(shape, lambda *_: (0,) * nd)


def _params(*sem):
    return pltpu.CompilerParams(dimension_semantics=sem, vmem_limit_bytes=VMEM_LIMIT)


def _prep_ab(x2, seq_len, gmix, w_in, qag, wq, kvag, wkv, qn_g, kn_g, qr_g, kr_g, bq_g, bk_g, tm=512):
    n = x2.shape[0]
    scale_a = (HEAD + 2 * ROPE_HALF) ** -0.5
    scale_b = HEAD ** -0.5
    zeros = lambda r, c: jnp.zeros((r, c), F32)
    d = w_in.shape[0]
    w0 = jnp.concatenate([w_in[:, 0:384], zeros(d, HEAD), w_in[:, 384:416], zeros(d, 32), w_in[:, 416:1184]],
                         axis=1).astype(BF16)
    wq_p = jnp.concatenate([wq.reshape(-1, 8, 96), jnp.zeros((wq.shape[0], 8, 32), F32)], axis=2)
    wq_p = wq_p.reshape(-1, 1024).astype(BF16)
    wkv_r = wkv.reshape(-1, 8, 128)
    wk_p = jnp.concatenate([wkv_r[:, :, :HEAD], jnp.zeros_like(wkv_r[:, :, :HEAD])], axis=2).reshape(-1, 1024)
    wv_p = jnp.concatenate([wkv_r[:, :, HEAD:], jnp.zeros_like(wkv_r[:, :, HEAD:])], axis=2).reshape(-1, 1024)
    wkv_p = jnp.concatenate([wk_p, wv_p], axis=1).astype(BF16)

    seg_q = [(LANES * h, HEAD) for h in range(8)] + [(LANES * h + HEAD, 32) for h in range(8)]
    gq = jnp.tile(jnp.concatenate([qn_g, qr_g, jnp.zeros((32,), F32)]), 8)[None, :] * (scale_a * LOG2E)
    seg_k = [(LANES * h, HEAD) for h in range(8)]
    gk = jnp.tile(jnp.concatenate([kn_g, jnp.zeros((HEAD,), F32)]), 8)[None, :]
    seg_r = [(HEAD, 32)]
    gr = jnp.concatenate([jnp.zeros((HEAD,), F32), kr_g, jnp.zeros((32,), F32)])[None, :]
    seg_bq = [(HEAD * h, HEAD) for h in range(8)]
    gbq = jnp.tile(bq_g, 8)[None, :] * (scale_b * LOG2E)
    seg_bk = [(HEAD * h, HEAD) for h in range(2)]
    gbk = jnp.tile(bk_g, 2)[None, :]

    pos = np.arange(seq_len)
    cosa, sina = _rope_tables([None, None, pos, None], seq_len)
    row, col = pos // GRID_W, pos % GRID_W
    cosb, sinb = _rope_tables([row, col, row, col], seq_len)

    consts = [gmix[None, :], w0, qag[None, :], wq_p, kvag[None, :], wkv_p,
              *_seg_mats(1024, seg_q), gq, *_seg_mats(1024, seg_k), gk, *_seg_mats(LANES, seg_r), gr,
              *_seg_mats(512, seg_bq), gbq, *_seg_mats(LANES, seg_bk), gbk]
    nsb = seq_len // tm
    rope_spec = pl.BlockSpec((tm, LANES), lambda i: (i % nsb, 0))
    row_spec = lambda w: pl.BlockSpec((tm, w), lambda i: (i, 0))
    out_widths = (1024, 1024, 1024, 512, 256, 256)
    return pl.pallas_call(
        _prep_ab_kernel,
        grid=(n // tm,),
        in_specs=[row_spec(x2.shape[1])] + [_full(c.shape) for c in consts] + [rope_spec] * 4,
        out_specs=[row_spec(w) for w in out_widths],
        out_shape=[jax.ShapeDtypeStruct((n, w), BF16) for w in out_widths],
        compiler_params=_params("parallel"),
        name="prep_ab",
    )(x2, *consts, cosa, sina, cosb, sinb)


def _prep_c(x2, gmix, w_in, q_g, k_g, tm=512):
    n = x2.shape[0]
    seg_q = [(HEAD * h, HEAD) for h in range(16)]
    gq = jnp.tile(q_g, 16)[None, :] * (HEAD ** -0.5 * LOG2E)
    seg_k = [(HEAD * h, HEAD) for h in range(4)]
    gk = jnp.tile(k_g, 4)[None, :]
    consts = [gmix[None, :], w_in.astype(BF16), *_seg_mats(1024, seg_q), gq, *_seg_mats(256, seg_k), gk]
    row_spec = lambda w: pl.BlockSpec((tm, w), lambda i: (i, 0))
    out_widths = (1024, 512, 512)
    return pl.pallas_call(
        _prep_c_kernel,
        grid=(n // tm,),
        in_specs=[row_spec(x2.shape[1])] + [_full(c.shape) for c in consts],
        out_specs=[row_spec(w) for w in out_widths],
        out_shape=[jax.ShapeDtypeStruct((n, w), BF16) for w in out_widths],
        compiler_params=_params("parallel"),
        name="prep_c",
    )(x2, *consts)


def _mla_attn(qa, ka, va, tq=512):
    b, s, _ = qa.shape
    return pl.pallas_call(
        _mla_attn_kernel,
        grid=(b, 4, s // tq),
        in_specs=[pl.BlockSpec((1, tq, 2 * LANES), lambda bi, hp, qi: (bi, qi, hp)),
                  pl.BlockSpec((1, s, 2 * LANES), lambda bi, hp, qi: (bi, 0, hp)),
                  pl.BlockSpec((1, s, 2 * LANES), lambda bi, hp, qi: (bi, 0, hp))],
        out_specs=pl.BlockSpec((1, tq, LANES), lambda bi, hp, qi: (bi, qi, hp)),
        out_shape=jax.ShapeDtypeStruct((b, s, 512), BF16),
        scratch_shapes=[pltpu.VMEM((tq, s), F32), pltpu.VMEM((tq, s), F32)],
        compiler_params=_params("parallel", "parallel", "parallel"),
        name="mla_attn",
    )(qa, ka, va)


def _gqa_attn(qb, kb, vb, tq=256):
    b, s, _ = qb.shape
    return pl.pallas_call(
        _gqa_attn_kernel,
        grid=(b, 2, s // tq),
        in_specs=[pl.BlockSpec((1, tq, 2 * LANES), lambda bi, g, qi: (bi, qi, g)),
                  pl.BlockSpec((1, s, LANES), lambda bi, g, qi: (bi, 0, g)),
                  pl.BlockSpec((1, s, LANES), lambda bi, g, qi: (bi, 0, g))],
        out_specs=pl.BlockSpec((1, tq, 2 * LANES), lambda bi, g, qi: (bi, qi, g)),
        out_shape=jax.ShapeDtypeStruct((b, s, 512), BF16),
        scratch_shapes=[pltpu.VMEM((2 * tq, s), F32), pltpu.VMEM((2 * tq, s), F32)],
        compiler_params=_params("parallel", "parallel", "parallel"),
        name="gqa_attn",
    )(qb, kb, vb)


def _rel_bias_table(rel_bias):
    span = Q_BLOCK + 2 * WINDOW
    rel = np.arange(span)[None, :] - WINDOW - np.arange(Q_BLOCK)[:, None]
    bucket = _t5_bucket_np(rel).astype(np.int32).reshape(1, -1)
    band = (np.abs(rel) <= WINDOW).astype(np.int32).reshape(1, -1)
    heads = rel_bias.shape[1]
    cols = bucket.shape[1]
    chunk = cols // 8
    table = pl.pallas_call(
        _rel_bias_kernel,
        grid=(8,),
        in_specs=[pl.BlockSpec((1, chunk), lambda i: (0, i)), pl.BlockSpec((1, chunk), lambda i: (0, i)),
                  _full((heads, REL_BUCKETS))],
        out_specs=pl.BlockSpec((heads, chunk), lambda i: (0, i)),
        out_shape=jax.ShapeDtypeStruct((heads, cols), F32),
        compiler_params=_params("parallel"),
        name="rel_bias",
    )(jnp.asarray(bucket), jnp.asarray(band), rel_bias.T)
    return table.reshape(heads, Q_BLOCK, span)


def _win_attn(qc, kc, vc, sink, bias, nq=4):
    b, s, _ = qc.shape
    tq = Q_BLOCK
    nb = s // tq
    prev = lambda bi, i: (bi, jnp.maximum(i * nq - 1, 0), 0)
    cur = lambda bi, i: (bi, i, 0)
    nxt = lambda bi, i: (bi, jnp.minimum((i + 1) * nq, nb - 1), 0)
    edge_spec = lambda im: pl.BlockSpec((1, tq, 4 * LANES), im)
    cur_spec = pl.BlockSpec((1, nq * tq, 4 * LANES), cur)
    return pl.pallas_call(
        functools.partial(_win_attn_kernel, seq_len=s),
        grid=(b, nb // nq),
        in_specs=[pl.BlockSpec(memory_space=pltpu.SMEM),
                  pl.BlockSpec((1, nq * tq, 8 * LANES), cur),
                  edge_spec(prev), cur_spec, edge_spec(nxt),
                  edge_spec(prev), cur_spec, edge_spec(nxt),
                  _full(bias.shape)],
        out_specs=pl.BlockSpec((1, nq * tq, 8 * LANES), cur),
        scratch_shapes=[pltpu.VMEM((4 * nq, 4 * tq, 3 * tq), F32), pltpu.VMEM((4 * nq, 4 * tq, 3 * tq), BF16),
                        pltpu.VMEM((4 * nq, 4 * tq, LANES), F32)],
        out_shape=jax.ShapeDtypeStruct((b, s, 1024), BF16),
        compiler_params=_params("parallel", "parallel"),
        name="win_attn",
    )(sink, qc, kc, kc, kc, vc, vc, vc, bias)


def _post(parts, ws, x2, gffn, w_group, b_group, w_router, b_router, tm=1024, sub=256):
    n, d = x2.shape
    wr = jnp.concatenate([w_group.T, w_router.T, jnp.zeros((ROUTER_ROWS - N_GROUPS - N_EXPERTS, d), F32)], axis=0)
    wrh = wr.astype(BF16)
    wrl = (wr - wrh.astype(F32)).astype(BF16)
    rb = jnp.concatenate([b_group, b_router, jnp.zeros((ROUTER_ROWS - N_GROUPS - N_EXPERTS,), F32)])[:, None]
    row_spec = lambda w: pl.BlockSpec((tm, w), lambda i: (i, 0))
    lane_spec = lambda r: pl.BlockSpec((r, tm), lambda i: (0, i))
    ws = [w.astype(BF16) for w in ws]
    tri = jnp.asarray(np.triu(np.ones((sub, sub), np.float32), 1), BF16)
    return pl.pallas_call(
        functools.partial(_post_kernel, n_parts=len(parts)),
        grid=(n // tm,),
        in_specs=[row_spec(p.shape[1]) for p in parts] + [_full(w.shape) for w in ws]
                 + [row_spec(d), _full((1, d)), _full(wrh.shape), _full(wrl.shape), _full(rb.shape),
                    _full(tri.shape)],
        out_specs=[pl.BlockSpec((tm * TOKEN_TILE_ROWS, LANES), lambda i: (i, 0)),
                   lane_spec(1), lane_spec(1), _full((ROUTER_ROWS, LANES))],
        out_shape=[jax.ShapeDtypeStruct((n * TOKEN_TILE_ROWS, LANES), F32),
                   jax.ShapeDtypeStruct((1, n), jnp.int32), jax.ShapeDtypeStruct((1, n), jnp.int32),
                   jax.ShapeDtypeStruct((ROUTER_ROWS, LANES), F32)],
        scratch_shapes=[pltpu.VMEM((ROUTER_ROWS, LANES), F32)],
        compiler_params=_params("arbitrary"),
        name="post",
    )(*parts, *ws, x2, gffn[None, :], wrh, wrl, rb, tri)


def _route(bucket, rank, counts, tmm):
    n = bucket.shape[1]
    n_tiles_max = n // tmm + N_BUCKETS
    n_slots = n_tiles_max * tmm
    bucket, rank = bucket[0], rank[0]
    cnt = counts[:N_BUCKETS, 0].astype(jnp.int32)
    padded = (cnt + tmm - 1) // tmm * tmm
    ends = jnp.cumsum(padded)
    starts = ends - padded
    pos = starts[bucket] + rank
    n_tiles = ends[-1] // tmm
    tile = jnp.minimum(jnp.arange(n_tiles_max, dtype=jnp.int32), n_tiles - 1)
    tile_bucket = jnp.sum((tile * tmm)[:, None] >= ends[None, :], axis=1).astype(jnp.int32)
    grp, pair = tile_bucket // PAIRS_PER_GROUP, tile_bucket % PAIRS_PER_GROUP
    tile_ea = grp * EXPERTS_PER_GROUP + jnp.asarray(PAIR_LO, jnp.int32)[pair]
    tile_eb = grp * EXPERTS_PER_GROUP + jnp.asarray(PAIR_HI, jnp.int32)[pair]
    token = jnp.arange(n, dtype=jnp.int32)
    src = (jnp.arange(n_slots + tmm, dtype=jnp.int32) % n).at[pos].set(
        token, unique_indices=True, mode="promise_in_bounds")
    return pos, src.reshape(n_tiles_max + 1, 1, tmm), tile_ea, tile_eb, n_tiles.reshape(1)


def _moe_sparse(x1t, gffn, w_group, b_group, w_router, b_router, src, tile_ea, tile_eb, n_tiles,
                w_gate, w_up, w_down, layer, tmm):
    n_tiles_max = src.shape[0] - 1
    _, ne, d, de = w_gate.shape
    idx_spec = lambda off: pl.BlockSpec((1, 1, tmm), lambda t, ea, eb, nt: (t + off, 0, 0),
                                        memory_space=pltpu.SMEM)
    up_spec = lambda which: pl.BlockSpec((1, 1, d, de), lambda t, ea, eb, nt: (layer, (ea, eb)[which][t], 0, 0))
    down_spec = lambda which: pl.BlockSpec((1, 1, de, d), lambda t, ea, eb, nt: (layer, (ea, eb)[which][t], 0, 0))
    wg, wu, wd = w_gate, w_up, w_down
    pad = LANES - N_GROUPS - N_EXPERTS
    wr = jnp.concatenate([w_group, w_router, jnp.zeros((d, pad), F32)], axis=1)
    rb = jnp.concatenate([b_group, b_router, jnp.zeros((pad,), F32)])[None, :]
    return pl.pallas_call(
        _moe_sparse_kernel,
        grid_spec=pltpu.PrefetchScalarGridSpec(
            num_scalar_prefetch=3,
            grid=(n_tiles_max,),
            in_specs=[idx_spec(0), idx_spec(1),
                      pl.BlockSpec((1, d), lambda t, ea, eb, nt: (0, 0)),
                      pl.BlockSpec((d, LANES), lambda t, ea, eb, nt: (0, 0)),
                      pl.BlockSpec((1, LANES), lambda t, ea, eb, nt: (0, 0)),
                      pl.BlockSpec(memory_space=pl.ANY),
                      up_spec(0), up_spec(0), down_spec(0), up_spec(1), up_spec(1), down_spec(1)],
            out_specs=pl.BlockSpec((tmm * TOKEN_TILE_ROWS, LANES), lambda t, ea, eb, nt: (t, 0)),
            scratch_shapes=[pltpu.VMEM((2, tmm * TOKEN_TILE_ROWS, LANES), F32), pltpu.SemaphoreType.DMA((2,))]),
        out_shape=jax.ShapeDtypeStruct((n_tiles_max * tmm * TOKEN_TILE_ROWS, LANES), F32),
        compiler_params=_params("arbitrary"),
        name="moe_sparse",
    )(tile_ea, tile_eb, n_tiles, src, src, gffn[None, :], wr, rb, x1t, wg, wu, wd, wg, wu, wd)


def _unpermute(sorted_rows, pos, tm=256):
    n = pos.shape[0]
    d = TOKEN_TILE_ROWS * LANES
    pos2 = jnp.concatenate([pos, jnp.zeros((tm,), jnp.int32)]).reshape(n // tm + 1, 1, tm)
    idx_spec = lambda off: pl.BlockSpec((1, 1, tm), lambda t: (t + off, 0, 0), memory_space=pltpu.SMEM)
    return pl.pallas_call(
        _unpermute_kernel,
        grid=(n // tm,),
        in_specs=[idx_spec(0), idx_spec(1), pl.BlockSpec(memory_space=pl.ANY)],
        out_specs=pl.BlockSpec((tm, d), lambda t: (t, 0)),
        out_shape=jax.ShapeDtypeStruct((n, d), F32),
        scratch_shapes=[pltpu.VMEM((2, tm * TOKEN_TILE_ROWS, LANES), F32), pltpu.SemaphoreType.DMA((2,))],
        compiler_params=_params("arbitrary"),
        name="unpermute",
    )(pos2, pos2, sorted_rows)


def kernel(x, mix_norm, ffn_norm, w_in_ab, mla_q_a_norm, mla_w_q_up, mla_kv_a_norm, mla_w_kv_up, mla_qn_gain, mla_kn_gain, mla_qr_gain, mla_kr_gain, gqa_q_gain, gqa_k_gain, w_out_ab, w_in_c, win_q_gain, win_k_gain, win_sink, w_out_c, rel_bias, moe_w_group, moe_b_group, moe_w_router, moe_b_router, moe_w_gate, moe_w_up, moe_w_down):
    b, s, d = x.shape
    n = b * s
    depth = mix_norm.shape[0]
    x2 = x.reshape(n, d)
    bias = None
    for layer in range(depth):
        i = layer // 2
        if layer % 2 == 0:
            qa, ka, va, qb, kb, vb = _prep_ab(
                x2, s, mix_norm[layer], w_in_ab[i], mla_q_a_norm[i], mla_w_q_up[i], mla_kv_a_norm[i],
                mla_w_kv_up[i], mla_qn_gain[i], mla_kn_gain[i], mla_qr_gain[i], mla_kr_gain[i],
                gqa_q_gain[i], gqa_k_gain[i])
            r3 = lambda t: t.reshape(b, s, t.shape[1])
            out_a = _mla_attn(r3(qa), r3(ka), r3(va)).reshape(n, 512)
            out_b = _gqa_attn(r3(qb), r3(kb), r3(vb)).reshape(n, 512)
            parts, ws = [out_a, out_b], [w_out_ab[i][:512], w_out_ab[i][512:]]
        else:
            if bias is None:
                bias = _rel_bias_table(rel_bias)
            qc, kc, vc = _prep_c(x2, mix_norm[layer], w_in_c[i], win_q_gain[i], win_k_gain[i])
            r3 = lambda t: t.reshape(b, s, t.shape[1])
            out_c = _win_attn(r3(qc), r3(kc), r3(vc), win_sink[i], bias).reshape(n, 1024)
            parts, ws = [out_c], [w_out_c[i]]
        router = (moe_w_group[layer], moe_b_group[layer], moe_w_router[layer], moe_b_router[layer])
        x1t, bucket, rank, counts = _post(parts, ws, x2, ffn_norm[layer], *router)
        pos, src, tile_ea, tile_eb, n_tiles = _route(bucket, rank, counts, MOE_TILE)
        y_sorted = _moe_sparse(x1t, ffn_norm[layer], *router, src, tile_ea, tile_eb, n_tiles,
                               moe_w_gate, moe_w_up, moe_w_down, layer, MOE_TILE)
        x2 = _unpermute(y_sorted, pos)
    return x2.reshape(b, s, d)
```

```python
import functools
import math

import numpy as np
import jax
import jax.numpy as jnp
from jax import lax
from jax.experimental import pallas as pl
from jax.experimental.pallas import tpu as pltpu

F32 = jnp.float32
BF16 = jnp.bfloat16

EPS = 1e-6
ROPE_THETA = 10000.0
LANES = 128
HEAD = 64
ROPE_HALF = 16
GRID_W = 64
WINDOW = 128
Q_BLOCK = 128
REL_BUCKETS = 32
REL_MAX_DIST = 128
N_GROUPS = 4
EXPERTS_PER_GROUP = 4
N_EXPERTS = 16
PAIRS_PER_GROUP = 6
N_BUCKETS = N_GROUPS * PAIRS_PER_GROUP
PAIR_LO = (0, 0, 0, 1, 1, 2)
PAIR_HI = (1, 2, 3, 2, 3, 3)
TOKEN_TILE_ROWS = 8
KEY_CHUNK = 512
LOG2E = math.log2(math.e)
PREP_SUB = 256
WIN_ROW_TILE = 32
GATHER_UNROLL = 8
MOE_TILE = 256
ROUTER_ROWS = 32
VMEM_LIMIT = 56 * 1024 * 1024

_NT = (((1,), (1,)), ((), ()))


def _dot(a, b):
    return jnp.dot(a, b, preferred_element_type=F32)


def _dot_nt(a, b):
    return lax.dot_general(a, b, _NT, preferred_element_type=F32)


def _split_bf16(a):
    hi = a.astype(BF16)
    lo = (a - hi.astype(F32)).astype(BF16)
    return hi, lo


def _row_rmsnorm(t, gain):
    return t * lax.rsqrt(jnp.mean(t * t, axis=-1, keepdims=True) + EPS) * gain


def _seg_rmsnorm(t, mseg, msegt, invlen, gain):
    sums = _dot((t * t).astype(BF16), mseg)
    inv = lax.rsqrt(sums * invlen + EPS)
    ihi, ilo = _split_bf16(inv)
    scale = _dot(jnp.concatenate([ihi, ilo], axis=1), msegt)
    return t * scale * gain


def _rope128(t, cos, sin_signed, first_half):
    up = pltpu.roll(t, LANES - ROPE_HALF, 1)
    dn = pltpu.roll(t, ROPE_HALF, 1)
    return t * cos + jnp.where(first_half, up, dn) * sin_signed


def _dup_halves(blk, lo):
    sw = pltpu.roll(blk, HEAD, 1)
    return jnp.where(lo, blk, sw), jnp.where(lo, sw, blk)


def _lane_masks():
    lane = lax.broadcasted_iota(jnp.int32, (1, LANES), 1)
    return lane < HEAD, (lane % (2 * ROPE_HALF)) < ROPE_HALF


def _by_sub_tiles(rows_fn, refs, n_const):
    row_refs = (refs[0],) + tuple(refs[1 + n_const:])
    for h in range(refs[0].shape[0] // PREP_SUB):
        view = [r.at[pl.ds(PREP_SUB * h, PREP_SUB)] for r in row_refs]
        rows_fn(view[0], *refs[1:1 + n_const], *view[1:])


def _prep_ab_kernel(*refs):
    _by_sub_tiles(_prep_ab_rows, refs, 26)


def _prep_c_kernel(*refs):
    _by_sub_tiles(_prep_c_rows, refs, 10)


def _prep_ab_rows(x_ref, gmix_ref, w0_ref, qag_ref, wq_ref, kvag_ref, wkv_ref,
                    mq_ref, mqt_ref, ilq_ref, gq_ref,
                    mk_ref, mkt_ref, ilk_ref, gk_ref,
                    mr_ref, mrt_ref, ilr_ref, gr_ref,
                    mbq_ref, mbqt_ref, ilbq_ref, gbq_ref,
                    mbk_ref, mbkt_ref, ilbk_ref, gbk_ref,
                    cosa_ref, sina_ref, cosb_ref, sinb_ref,
                    qa_ref, ka_ref, va_ref, qb_ref, kb_ref, vb_ref):
    lo, first_half = _lane_masks()
    h = _row_rmsnorm(x_ref[...], gmix_ref[...]).astype(BF16)
    proj = _dot(h, w0_ref[...])
    cosa, sina = cosa_ref[...], sina_ref[...]
    cosb, sinb = cosb_ref[...], sinb_ref[...]

    qn = _row_rmsnorm(proj[:, 0:256], qag_ref[...]).astype(BF16)
    q = _seg_rmsnorm(_dot(qn, wq_ref[...]), mq_ref[...], mqt_ref[...], ilq_ref[...], gq_ref[...])
    for hd in range(8):
        sl = slice(LANES * hd, LANES * (hd + 1))
        qa_ref[:, sl] = _rope128(q[:, sl], cosa, sina, first_half).astype(BF16)

    kvn = _row_rmsnorm(proj[:, 256:384], kvag_ref[...]).astype(BF16)
    kv = _dot(kvn, wkv_ref[...])
    ones_hi = jnp.where(lo, 0.0, 1.0)
    kn = _seg_rmsnorm(kv[:, 0:1024], mk_ref[...], mkt_ref[...], ilk_ref[...], gk_ref[...])
    kr = _seg_rmsnorm(proj[:, 384:512], mr_ref[...], mrt_ref[...], ilr_ref[...], gr_ref[...])
    kr = _rope128(kr, cosa, sina, first_half)
    for hd in range(8):
        sl = slice(LANES * hd, LANES * (hd + 1))
        ka_ref[:, sl] = (kn[:, sl] + kr).astype(BF16)
        va_ref[:, sl] = (kv[:, 1024 + LANES * hd:1024 + LANES * (hd + 1)] + ones_hi).astype(BF16)

    bq = _seg_rmsnorm(proj[:, 512:1024], mbq_ref[...], mbqt_ref[...], ilbq_ref[...], gbq_ref[...])
    for blk in range(4):
        sl = slice(LANES * blk, LANES * (blk + 1))
        qb_ref[:, sl] = _rope128(bq[:, sl], cosb, sinb, first_half).astype(BF16)
    bk = _seg_rmsnorm(proj[:, 1024:1152], mbk_ref[...], mbkt_ref[...], ilbk_ref[...], gbk_ref[...])
    bk = _rope128(bk, cosb, sinb, first_half)
    k0, k1 = _dup_halves(bk, lo)
    kb_ref[:, 0:LANES] = k0.astype(BF16)
    kb_ref[:, LANES:2 * LANES] = k1.astype(BF16)
    bv = proj[:, 1152:1280]
    vb_ref[:, 0:LANES] = jnp.where(lo, bv, 1.0).astype(BF16)
    vb_ref[:, LANES:2 * LANES] = jnp.where(lo, pltpu.roll(bv, HEAD, 1), 1.0).astype(BF16)


def _prep_c_rows(x_ref, gmix_ref, w_ref,
                   mq_ref, mqt_ref, ilq_ref, gq_ref,
                   mk_ref, mkt_ref, ilk_ref, gk_ref,
                   qc_ref, kc_ref, vc_ref):
    lo, _ = _lane_masks()
    h = _row_rmsnorm(x_ref[...], gmix_ref[...]).astype(BF16)
    proj = _dot(h, w_ref[...])
    q = _seg_rmsnorm(proj[:, 0:1024], mq_ref[...], mqt_ref[...], ilq_ref[...], gq_ref[...])
    qc_ref[...] = q.astype(BF16)
    k = _seg_rmsnorm(proj[:, 1024:1280], mk_ref[...], mkt_ref[...], ilk_ref[...], gk_ref[...])
    for blk in range(2):
        sl = slice(LANES * blk, LANES * (blk + 1))
        k0, k1 = _dup_halves(k[:, sl], lo)
        kc_ref[:, 2 * blk * LANES:(2 * blk + 1) * LANES] = k0.astype(BF16)
        kc_ref[:, (2 * blk + 1) * LANES:(2 * blk + 2) * LANES] = k1.astype(BF16)
        v = proj[:, 1280 + LANES * blk:1280 + LANES * (blk + 1)]
        vc_ref[:, 2 * blk * LANES:(2 * blk + 1) * LANES] = jnp.where(lo, v, 1.0).astype(BF16)
        vc_ref[:, (2 * blk + 1) * LANES:(2 * blk + 2) * LANES] = jnp.where(
            lo, pltpu.roll(v, HEAD, 1), 1.0).astype(BF16)


def _softmax_pv(s, v):
    m = jnp.max(s, axis=-1, keepdims=True)
    p = jnp.exp(s - m)
    l = jnp.sum(p, axis=-1, keepdims=True)
    return _dot(p.astype(BF16), v) / l


def _lane_chunk_reduce(op, t):
    out = t[:, 0:LANES]
    for j in range(1, t.shape[1] // LANES):
        out = op(out, t[:, LANES * j:LANES * (j + 1)])
    return out


def _scores_phase(q, k_ref, lanes, s_buf):
    m_part = None
    for c in range(k_ref.shape[1] // KEY_CHUNK):
        ks = slice(KEY_CHUNK * c, KEY_CHUNK * (c + 1))
        s_c = _dot_nt(q, k_ref[0, ks, lanes])
        s_buf[:, ks] = s_c
        mc = _lane_chunk_reduce(jnp.maximum, s_c)
        m_part = mc if m_part is None else jnp.maximum(m_part, mc)
    return jnp.max(m_part, axis=-1, keepdims=True)


def _pv_phase(s_buf, m, v_ref, lanes):
    lo, _ = _lane_masks()
    acc = None
    for c in range(v_ref.shape[1] // KEY_CHUNK):
        ks = slice(KEY_CHUNK * c, KEY_CHUNK * (c + 1))
        p = jnp.exp2(s_buf[:, ks] - m)
        pv = _dot(p.astype(BF16), v_ref[0, ks, lanes])
        acc = pv if acc is None else acc + pv
    return acc / jnp.where(lo, pltpu.roll(acc, HEAD, 1), 1.0)


def _pair_heads(o_even, o_odd):
    lo, _ = _lane_masks()
    return jnp.where(lo, o_even, pltpu.roll(o_odd, HEAD, 1))


def _mla_attn_kernel(q_ref, k_ref, v_ref, o_ref, s0_ref, s1_ref):
    first, second = slice(0, LANES), slice(LANES, 2 * LANES)
    m0 = _scores_phase(q_ref[0, :, first], k_ref, first, s0_ref)
    m1 = _scores_phase(q_ref[0, :, second], k_ref, second, s1_ref)
    o0 = _pv_phase(s0_ref, m0, v_ref, first)
    o1 = _pv_phase(s1_ref, m1, v_ref, second)
    o_ref[0] = _pair_heads(o0, o1).astype(o_ref.dtype)


def _gqa_attn_kernel(q_ref, k_ref, v_ref, o_ref, s0_ref, s1_ref):
    lo, _ = _lane_masks()
    tq = q_ref.shape[1]
    hi = jnp.logical_not(lo)
    ms = []
    for j, s_ref in enumerate((s0_ref, s1_ref)):
        blk = q_ref[0, :, LANES * j:LANES * (j + 1)]
        zero = jnp.zeros_like(blk)
        qs = jnp.concatenate([jnp.where(lo, blk, zero), jnp.where(hi, blk, zero)], axis=0)
        ms.append(_scores_phase(qs, k_ref, slice(0, LANES), s_ref))
    for j, s_ref in enumerate((s0_ref, s1_ref)):
        o = _pv_phase(s_ref, ms[j], v_ref, slice(0, LANES))
        o_ref[0, :, LANES * j:LANES * (j + 1)] = _pair_heads(o[0:tq], o[tq:2 * tq]).astype(o_ref.dtype)


def _win_attn_kernel(sink_ref, q_ref, kp_ref, kc_ref, kn_ref, vp_ref, vc_ref, vn_ref, bias_ref, o_ref,
                     s_scr, p_scr, m_scr, *, seq_len):
    lo, _ = _lane_masks()
    tq = Q_BLOCK
    nq = q_ref.shape[1] // tq
    i = pl.program_id(1)
    kj = lax.broadcasted_iota(jnp.int32, (1, 3 * tq), 1)
    units = [(g, u) for g in range(4) for u in range(nq)]
    for idx, (g, u) in enumerate(units):
        sl = slice(LANES * g, LANES * (g + 1))
        kcat = jnp.concatenate([kp_ref[0, :, sl], kc_ref[0, :, sl], kn_ref[0, :, sl]], axis=0)
        parts = []
        for a in range(4):
            blk = q_ref[0, tq * u:tq * (u + 1), LANES * (2 * g + a // 2):LANES * (2 * g + a // 2 + 1)]
            keep = lo if a % 2 == 0 else jnp.logical_not(lo)
            parts.append(jnp.where(keep, blk, jnp.zeros_like(blk)))
        s_scr[idx] = _dot_nt(jnp.concatenate(parts, axis=0), kcat[tq * u:tq * (u + 3)])
    for idx, (g, u) in enumerate(units):
        key_pos = (i * nq + u - 1) * tq + kj
        valid = jnp.logical_and(key_pos >= 0, key_pos < seq_len)
        for rt in range(4 * tq // WIN_ROW_TILE):
            a, r0 = divmod(rt * WIN_ROW_TILE, tq)
            rows = slice(rt * WIN_ROW_TILE, (rt + 1) * WIN_ROW_TILE)
            s = s_scr[idx, rows, :] + bias_ref[4 * g + a, r0:r0 + WIN_ROW_TILE, :]
            s = jnp.where(valid, s, -jnp.inf)
            m = jnp.maximum(jnp.max(_lane_chunk_reduce(jnp.maximum, s), axis=-1, keepdims=True),
                            sink_ref[4 * g + a] * LOG2E)
            p_scr[idx, rows, :] = jnp.exp2(s - m).astype(BF16)
            m_scr[idx, rows, :] = jnp.broadcast_to(m, (WIN_ROW_TILE, LANES))
    for idx, (g, u) in enumerate(units):
        sl = slice(LANES * g, LANES * (g + 1))
        vcat = jnp.concatenate([vp_ref[0, :, sl], vc_ref[0, :, sl], vn_ref[0, :, sl]], axis=0)
        acc = _dot(p_scr[idx], vcat[tq * u:tq * (u + 3)])
        outs = []
        for a in range(4):
            rows = slice(tq * a, tq * (a + 1))
            sink_term = jnp.exp2(sink_ref[4 * g + a] * LOG2E - m_scr[idx, rows, :])
            den = jnp.where(lo, pltpu.roll(acc[rows], HEAD, 1) + sink_term, 1.0)
            outs.append(acc[rows] / den)
        for j in range(2):
            blk = 2 * g + j
            o_ref[0, tq * u:tq * (u + 1), LANES * blk:LANES * (blk + 1)] = _pair_heads(
                outs[2 * j], outs[2 * j + 1]).astype(o_ref.dtype)


def _rel_bias_kernel(bucket_ref, band_ref, relt_ref, o_ref):
    bucket = bucket_ref[...]
    acc = jnp.zeros(o_ref.shape, F32)
    for r in range(REL_BUCKETS):
        acc = acc + jnp.where(bucket == r, relt_ref[:, r:r + 1], 0.0)
    o_ref[...] = jnp.where(band_ref[...] > 0, acc * LOG2E, -jnp.inf)


def _to_tiles(ref, val, first_token=0):
    for c in range(TOKEN_TILE_ROWS):
        ref[pl.ds(first_token * TOKEN_TILE_ROWS + c, val.shape[0], stride=TOKEN_TILE_ROWS), :] = (
            val[:, LANES * c:LANES * (c + 1)])


def _from_tiles(ref):
    rows = ref.shape[0] // TOKEN_TILE_ROWS
    return jnp.concatenate([ref[pl.ds(c, rows, stride=TOKEN_TILE_ROWS), :] for c in range(TOKEN_TILE_ROWS)],
                           axis=1)


def _route_bucket(xn, wh, wl, rb):
    xh, xl = _split_bf16(xn)
    logit = _dot_nt(wh, xh) + _dot_nt(wh, xl) + _dot_nt(wl, xh) + rb
    g = [logit[r:r + 1, :] for r in range(N_GROUPS)]
    gmax = jnp.maximum(jnp.maximum(g[0], g[1]), jnp.maximum(g[2], g[3]))
    gidx = jnp.where(g[0] == gmax, 0, jnp.where(g[1] == gmax, 1, jnp.where(g[2] == gmax, 2, 3)))
    e = []
    for j in range(EXPERTS_PER_GROUP):
        rows = [logit[N_GROUPS + EXPERTS_PER_GROUP * gg + j:N_GROUPS + EXPERTS_PER_GROUP * gg + j + 1, :]
                for gg in range(N_GROUPS)]
        e.append(jnp.where(gidx == 0, rows[0], jnp.where(gidx == 1, rows[1],
                                                         jnp.where(gidx == 2, rows[2], rows[3]))))
    emax = jnp.maximum(jnp.maximum(e[0], e[1]), jnp.maximum(e[2], e[3]))
    ex = [jnp.exp(ej - emax) for ej in e]
    esum = ex[0] + ex[1] + ex[2] + ex[3]
    pr = [exj / esum for exj in ex]
    p1 = jnp.maximum(jnp.maximum(pr[0], pr[1]), jnp.maximum(pr[2], pr[3]))
    i1 = jnp.where(pr[0] == p1, 0, jnp.where(pr[1] == p1, 1, jnp.where(pr[2] == p1, 2, 3)))
    rest = [jnp.where(i1 == j, -1.0, pr[j]) for j in range(EXPERTS_PER_GROUP)]
    p2 = jnp.maximum(jnp.maximum(rest[0], rest[1]), jnp.maximum(rest[2], rest[3]))
    i2 = jnp.where(rest[0] == p2, 0, jnp.where(rest[1] == p2, 1, jnp.where(rest[2] == p2, 2, 3)))
    lo_e = jnp.minimum(i1, i2)
    hi_e = jnp.maximum(i1, i2)
    pair = jnp.where(lo_e == 0, hi_e - 1, jnp.where(lo_e == 1, hi_e + 1, 5))
    return gidx * PAIRS_PER_GROUP + pair


def _post_kernel(*refs, n_parts):
    parts = refs[:n_parts]
    ws = refs[n_parts:2 * n_parts]
    (x_ref, gffn_ref, wrh_ref, wrl_ref, rb_ref, tri_ref,
     x1t_ref, bucket_ref, rank_ref, counts_ref, carry_ref) = refs[2 * n_parts:]

    @pl.when(pl.program_id(0) == 0)
    def _():
        carry_ref[...] = jnp.zeros_like(carry_ref)

    sub = tri_ref.shape[0]
    buckets = []
    for h in range(x_ref.shape[0] // sub):
        rows = slice(sub * h, sub * (h + 1))
        acc = x_ref[rows, :]
        for p_ref, w_ref in zip(parts, ws):
            acc = acc + _dot(p_ref[rows, :], w_ref[...])
        _to_tiles(x1t_ref, acc, sub * h)
        bucket = _route_bucket(_row_rmsnorm(acc, gffn_ref[...]), wrh_ref[...], wrl_ref[...], rb_ref[...])
        bucket_ref[:, rows] = bucket
        buckets.append(bucket)
    for h, bucket in enumerate(buckets):
        rows = slice(sub * h, sub * (h + 1))
        onehot = (lax.broadcasted_iota(jnp.int32, (ROUTER_ROWS, sub), 0) == bucket).astype(F32)
        before = _dot(onehot.astype(BF16), tri_ref[...]) + carry_ref[:, 0:1]
        rank_ref[:, rows] = jnp.sum(onehot * before, axis=0, keepdims=True).astype(jnp.int32)
        carry_ref[...] = carry_ref[...] + jnp.sum(onehot, axis=1, keepdims=True)
    counts_ref[...] = carry_ref[...]


def _start_row_gather(idx_of, table_hbm, dst, sem):
    def body(blk, carry):
        for j in range(GATHER_UNROLL):
            r = blk * GATHER_UNROLL + j
            src_row = pl.multiple_of(idx_of(r) * TOKEN_TILE_ROWS, TOKEN_TILE_ROWS)
            dst_row = pl.multiple_of(r * TOKEN_TILE_ROWS, TOKEN_TILE_ROWS)
            pltpu.make_async_copy(table_hbm.at[pl.ds(src_row, TOKEN_TILE_ROWS)],
                                  dst.at[pl.ds(dst_row, TOKEN_TILE_ROWS)], sem).start()
        return carry
    lax.fori_loop(0, dst.shape[0] // (TOKEN_TILE_ROWS * GATHER_UNROLL), body, 0)


def _wait_row_gather(table_hbm, dst, sem):
    pltpu.make_async_copy(table_hbm.at[pl.ds(0, dst.shape[0])], dst, sem).wait()


def _tile_gates(logit, ea, eb):
    lane = lax.broadcasted_iota(jnp.int32, (1, LANES), 1)
    pick = lambda idx: jnp.sum(jnp.where(lane == idx, logit, 0.0), axis=-1, keepdims=True)
    glog = jnp.where(lane < N_GROUPS, logit, -jnp.inf)
    gmax = jnp.max(glog, axis=-1, keepdims=True)
    gsum = jnp.sum(jnp.exp(glog - gmax), axis=-1, keepdims=True)
    g_p = jnp.exp(pick(ea // EXPERTS_PER_GROUP) - gmax) / gsum
    la, lb = pick(N_GROUPS + ea), pick(N_GROUPS + eb)
    top = jnp.maximum(la, lb)
    pa, pb = jnp.exp(la - top), jnp.exp(lb - top)
    return g_p * pa / (pa + pb), g_p * pb / (pa + pb)


def _moe_sparse_kernel(ea_ref, eb_ref, nt_ref, first_ref, order_ref, gffn_ref, wr_ref, rb_ref, x1t_hbm,
                       wga_ref, wua_ref, wda_ref, wgb_ref, wub_ref, wdb_ref, o_ref, buf, sem):
    t = pl.program_id(0)
    nt = nt_ref[0]
    slot = t % 2

    def tokens_of(tile):
        base = first_ref[tile]
        return lambda r: order_ref[base + r]

    @pl.when(t == 0)
    def _():
        _start_row_gather(tokens_of(0), x1t_hbm, buf.at[0], sem.at[0])

    @pl.when(t + 1 < nt)
    def _():
        _start_row_gather(tokens_of(t + 1), x1t_hbm, buf.at[1 - slot], sem.at[1 - slot])

    @pl.when(t < nt)
    def _():
        _wait_row_gather(x1t_hbm, buf.at[slot], sem.at[slot])
        x = _from_tiles(buf.at[slot])
        xn = _row_rmsnorm(x, gffn_ref[...])
        gate_a, gate_b = _tile_gates(_dot(xn, wr_ref[...]) + rb_ref[...], ea_ref[t], eb_ref[t])
        hid_a = jax.nn.silu(_dot(xn, wga_ref[0, 0])) * _dot(xn, wua_ref[0, 0])
        hid_b = jax.nn.silu(_dot(xn, wgb_ref[0, 0])) * _dot(xn, wub_ref[0, 0])
        out = x + gate_a * _dot(hid_a, wda_ref[0, 0]) + gate_b * _dot(hid_b, wdb_ref[0, 0])
        _to_tiles(o_ref, out)

    @pl.when(t >= nt)
    def _():
        o_ref[...] = jnp.zeros_like(o_ref)


def _unpermute_kernel(pos_cur_ref, pos_nxt_ref, sorted_hbm, o_ref, buf, sem):
    t = pl.program_id(0)
    slot = t % 2

    @pl.when(t == 0)
    def _():
        _start_row_gather(lambda r: pos_cur_ref[0, 0, r], sorted_hbm, buf.at[0], sem.at[0])

    @pl.when(t + 1 < pl.num_programs(0))
    def _():
        _start_row_gather(lambda r: pos_nxt_ref[0, 0, r], sorted_hbm, buf.at[1 - slot], sem.at[1 - slot])

    _wait_row_gather(sorted_hbm, buf.at[slot], sem.at[slot])
    o_ref[...] = _from_tiles(buf.at[slot])


def _seg_mats(width, segments):
    m = np.zeros((width, LANES), np.float32)
    invlen = np.ones((1, LANES), np.float32)
    for c, (start, length) in enumerate(segments):
        m[start:start + length, c] = 1.0
        invlen[0, c] = 1.0 / length
    return jnp.asarray(m, BF16), jnp.asarray(np.concatenate([m.T, m.T], axis=0), BF16), jnp.asarray(invlen)


def _rope_tables(pos_list, seq_len):
    inv = np.float32(ROPE_THETA) ** (-np.arange(0, 2 * ROPE_HALF, 2, dtype=np.float32) / np.float32(2 * ROPE_HALF))
    cos_cols, sin_cols = [], []
    for pos in pos_list:
        if pos is None:
            cos_cols.append(np.ones((seq_len, 2 * ROPE_HALF), np.float32))
            sin_cols.append(np.zeros((seq_len, 2 * ROPE_HALF), np.float32))
        else:
            ang = pos.astype(np.float32)[:, None] * inv[None, :]
            c, s = np.cos(ang), np.sin(ang)
            cos_cols.append(np.concatenate([c, c], axis=1))
            sin_cols.append(np.concatenate([-s, s], axis=1))
    return jnp.asarray(np.concatenate(cos_cols, axis=1)), jnp.asarray(np.concatenate(sin_cols, axis=1))


def _t5_bucket_np(rel):
    nb = REL_BUCKETS // 2
    max_exact = nb // 2
    ret = np.where(rel > 0, nb, 0)
    n = np.abs(rel)
    nf = np.maximum(n, 1).astype(np.float32)
    large = max_exact + (np.log(nf / np.float32(max_exact)) / np.float32(math.log(REL_MAX_DIST / max_exact))
                         * np.float32(nb - max_exact)).astype(np.int32)
    large = np.minimum(large, nb - 1)
    return ret + np.where(n < max_exact, n, large)


def _full(shape):
    nd = len(shape)
    return pl.BlockSpec(shape, lambda *_: (0,) * nd)


def _params(*sem):
    return pltpu.CompilerParams(dimension_semantics=sem, vmem_limit_bytes=VMEM_LIMIT)


def _prep_ab(x2, seq_len, gmix, w_in, qag, wq, kvag, wkv, qn_g, kn_g, qr_g, kr_g, bq_g, bk_g, tm=512):
    n = x2.shape[0]
    scale_a = (HEAD + 2 * ROPE_HALF) ** -0.5
    scale_b = HEAD ** -0.5
    zeros = lambda r, c: jnp.zeros((r, c), F32)
    d = w_in.shape[0]
    w0 = jnp.concatenate([w_in[:, 0:384], zeros(d, HEAD), w_in[:, 384:416], zeros(d, 32), w_in[:, 416:1184]],
                         axis=1).astype(BF16)
    wq_p = jnp.concatenate([wq.reshape(-1, 8, 96), jnp.zeros((wq.shape[0], 8, 32), F32)], axis=2)
    wq_p = wq_p.reshape(-1, 1024).astype(BF16)
    wkv_r = wkv.reshape(-1, 8, 128)
    wk_p = jnp.concatenate([wkv_r[:, :, :HEAD], jnp.zeros_like(wkv_r[:, :, :HEAD])], axis=2).reshape(-1, 1024)
    wv_p = jnp.concatenate([wkv_r[:, :, HEAD:], jnp.zeros_like(wkv_r[:, :, HEAD:])], axis=2).reshape(-1, 1024)
    wkv_p = jnp.concatenate([wk_p, wv_p], axis=1).astype(BF16)

    seg_q = [(LANES * h, HEAD) for h in range(8)] + [(LANES * h + HEAD, 32) for h in range(8)]
    gq = jnp.tile(jnp.concatenate([qn_g, qr_g, jnp.zeros((32,), F32)]), 8)[None, :] * (scale_a * LOG2E)
    seg_k = [(LANES * h, HEAD) for h in range(8)]
    gk = jnp.tile(jnp.concatenate([kn_g, jnp.zeros((HEAD,), F32)]), 8)[None, :]
    seg_r = [(HEAD, 32)]
    gr = jnp.concatenate([jnp.zeros((HEAD,), F32), kr_g, jnp.zeros((32,), F32)])[None, :]
    seg_bq = [(HEAD * h, HEAD) for h in range(8)]
    gbq = jnp.tile(bq_g, 8)[None, :] * (scale_b * LOG2E)
    seg_bk = [(HEAD * h, HEAD) for h in range(2)]
    gbk = jnp.tile(bk_g, 2)[None, :]

    pos = np.arange(seq_len)
    cosa, sina = _rope_tables([None, None, pos, None], seq_len)
    row, col = pos // GRID_W, pos % GRID_W
    cosb, sinb = _rope_tables([row, col, row, col], seq_len)

    consts = [gmix[None, :], w0, qag[None, :], wq_p, kvag[None, :], wkv_p,
              *_seg_mats(1024, seg_q), gq, *_seg_mats(1024, seg_k), gk, *_seg_mats(LANES, seg_r), gr,
              *_seg_mats(512, seg_bq), gbq, *_seg_mats(LANES, seg_bk), gbk]
    nsb = seq_len // tm
    rope_spec = pl.BlockSpec((tm, LANES), lambda i: (i % nsb, 0))
    row_spec = lambda w: pl.BlockSpec((tm, w), lambda i: (i, 0))
    out_widths = (1024, 1024, 1024, 512, 256, 256)
    return pl.pallas_call(
        _prep_ab_kernel,
        grid=(n // tm,),
        in_specs=[row_spec(x2.shape[1])] + [_full(c.shape) for c in consts] + [rope_spec] * 4,
        out_specs=[row_spec(w) for w in out_widths],
        out_shape=[jax.ShapeDtypeStruct((n, w), BF16) for w in out_widths],
        compiler_params=_params("parallel"),
        name="prep_ab",
    )(x2, *consts, cosa, sina, cosb, sinb)


def _prep_c(x2, gmix, w_in, q_g, k_g, tm=512):
    n = x2.shape[0]
    seg_q = [(HEAD * h, HEAD) for h in range(16)]
    gq = jnp.tile(q_g, 16)[None, :] * (HEAD ** -0.5 * LOG2E)
    seg_k = [(HEAD * h, HEAD) for h in range(4)]
    gk = jnp.tile(k_g, 4)[None, :]
    consts = [gmix[None, :], w_in.astype(BF16), *_seg_mats(1024, seg_q), gq, *_seg_mats(256, seg_k), gk]
    row_spec = lambda w: pl.BlockSpec((tm, w), lambda i: (i, 0))
    out_widths = (1024, 512, 512)
    return pl.pallas_call(
        _prep_c_kernel,
        grid=(n // tm,),
        in_specs=[row_spec(x2.shape[1])] + [_full(c.shape) for c in consts],
        out_specs=[row_spec(w) for w in out_widths],
        out_shape=[jax.ShapeDtypeStruct((n, w), BF16) for w in out_widths],
        compiler_params=_params("parallel"),
        name="prep_c",
    )(x2, *consts)


def _mla_attn(qa, ka, va, tq=512):
    b, s, _ = qa.shape
    return pl.pallas_call(
        _mla_attn_kernel,
        grid=(b, 4, s // tq),
        in_specs=[pl.BlockSpec((1, tq, 2 * LANES), lambda bi, hp, qi: (bi, qi, hp)),
                  pl.BlockSpec((1, s, 2 * LANES), lambda bi, hp, qi: (bi, 0, hp)),
                  pl.BlockSpec((1, s, 2 * LANES), lambda bi, hp, qi: (bi, 0, hp))],
        out_specs=pl.BlockSpec((1, tq, LANES), lambda bi, hp, qi: (bi, qi, hp)),
        out_shape=jax.ShapeDtypeStruct((b, s, 512), BF16),
        scratch_shapes=[pltpu.VMEM((tq, s), F32), pltpu.VMEM((tq, s), F32)],
        compiler_params=_params("parallel", "parallel", "parallel"),
        name="mla_attn",
    )(qa, ka, va)


def _gqa_attn(qb, kb, vb, tq=256):
    b, s, _ = qb.shape
    return pl.pallas_call(
        _gqa_attn_kernel,
        grid=(b, 2, s // tq),
        in_specs=[pl.BlockSpec((1, tq, 2 * LANES), lambda bi, g, qi: (bi, qi, g)),
                  pl.BlockSpec((1, s, LANES), lambda bi, g, qi: (bi, 0, g)),
                  pl.BlockSpec((1, s, LANES), lambda bi, g, qi: (bi, 0, g))],
        out_specs=pl.BlockSpec((1, tq, 2 * LANES), lambda bi, g, qi: (bi, qi, g)),
        out_shape=jax.ShapeDtypeStruct((b, s, 512), BF16),
        scratch_shapes=[pltpu.VMEM((2 * tq, s), F32), pltpu.VMEM((2 * tq, s), F32)],
        compiler_params=_params("parallel", "parallel", "parallel"),
        name="gqa_attn",
    )(qb, kb, vb)


def _rel_bias_table(rel_bias):
    span = Q_BLOCK + 2 * WINDOW
    rel = np.arange(span)[None, :] - WINDOW - np.arange(Q_BLOCK)[:, None]
    bucket = _t5_bucket_np(rel).astype(np.int32).reshape(1, -1)
    band = (np.abs(rel) <= WINDOW).astype(np.int32).reshape(1, -1)
    heads = rel_bias.shape[1]
    cols = bucket.shape[1]
    chunk = cols // 8
    table = pl.pallas_call(
        _rel_bias_kernel,
        grid=(8,),
        in_specs=[pl.BlockSpec((1, chunk), lambda i: (0, i)), pl.BlockSpec((1, chunk), lambda i: (0, i)),
                  _full((heads, REL_BUCKETS))],
        out_specs=pl.BlockSpec((heads, chunk), lambda i: (0, i)),
        out_shape=jax.ShapeDtypeStruct((heads, cols), F32),
        compiler_params=_params("parallel"),
        name="rel_bias",
    )(jnp.asarray(bucket), jnp.asarray(band), rel_bias.T)
    return table.reshape(heads, Q_BLOCK, span)


def _win_attn(qc, kc, vc, sink, bias, nq=4):
    b, s, _ = qc.shape
    tq = Q_BLOCK
    nb = s // tq
    prev = lambda bi, i: (bi, jnp.maximum(i * nq - 1, 0), 0)
    cur = lambda bi, i: (bi, i, 0)
    nxt = lambda bi, i: (bi, jnp.minimum((i + 1) * nq, nb - 1), 0)
    edge_spec = lambda im: pl.BlockSpec((1, tq, 4 * LANES), im)
    cur_spec = pl.BlockSpec((1, nq * tq, 4 * LANES), cur)
    return pl.pallas_call(
        functools.partial(_win_attn_kernel, seq_len=s),
        grid=(b, nb // nq),
        in_specs=[pl.BlockSpec(memory_space=pltpu.SMEM),
                  pl.BlockSpec((1, nq * tq, 8 * LANES), cur),
                  edge_spec(prev), cur_spec, edge_spec(nxt),
                  edge_spec(prev), cur_spec, edge_spec(nxt),
                  _full(bias.shape)],
        out_specs=pl.BlockSpec((1, nq * tq, 8 * LANES), cur),
        scratch_shapes=[pltpu.VMEM((4 * nq, 4 * tq, 3 * tq), F32), pltpu.VMEM((4 * nq, 4 * tq, 3 * tq), BF16),
                        pltpu.VMEM((4 * nq, 4 * tq, LANES), F32)],
        out_shape=jax.ShapeDtypeStruct((b, s, 1024), BF16),
        compiler_params=_params("parallel", "parallel"),
        name="win_attn",
    )(sink, qc, kc, kc, kc, vc, vc, vc, bias)


def _post(parts, ws, x2, gffn, w_group, b_group, w_router, b_router, tm=1024, sub=256):
    n, d = x2.shape
    wr = jnp.concatenate([w_group.T, w_router.T, jnp.zeros((ROUTER_ROWS - N_GROUPS - N_EXPERTS, d), F32)], axis=0)
    wrh = wr.astype(BF16)
    wrl = (wr - wrh.astype(F32)).astype(BF16)
    rb = jnp.concatenate([b_group, b_router, jnp.zeros((ROUTER_ROWS - N_GROUPS - N_EXPERTS,), F32)])[:, None]
    row_spec = lambda w: pl.BlockSpec((tm, w), lambda i: (i, 0))
    lane_spec = lambda r: pl.BlockSpec((r, tm), lambda i: (0, i))
    ws = [w.astype(BF16) for w in ws]
    tri = jnp.asarray(np.triu(np.ones((sub, sub), np.float32), 1), BF16)
    return pl.pallas_call(
        functools.partial(_post_kernel, n_parts=len(parts)),
        grid=(n // tm,),
        in_specs=[row_spec(p.shape[1]) for p in parts] + [_full(w.shape) for w in ws]
                 + [row_spec(d), _full((1, d)), _full(wrh.shape), _full(wrl.shape), _full(rb.shape),
                    _full(tri.shape)],
        out_specs=[pl.BlockSpec((tm * TOKEN_TILE_ROWS, LANES), lambda i: (i, 0)),
                   lane_spec(1), lane_spec(1), _full((ROUTER_ROWS, LANES))],
        out_shape=[jax.ShapeDtypeStruct((n * TOKEN_TILE_ROWS, LANES), F32),
                   jax.ShapeDtypeStruct((1, n), jnp.int32), jax.ShapeDtypeStruct((1, n), jnp.int32),
                   jax.ShapeDtypeStruct((ROUTER_ROWS, LANES), F32)],
        scratch_shapes=[pltpu.VMEM((ROUTER_ROWS, LANES), F32)],
        compiler_params=_params("arbitrary"),
        name="post",
    )(*parts, *ws, x2, gffn[None, :], wrh, wrl, rb, tri)


def _route(bucket, rank, counts, tmm):
    n = bucket.shape[1]
    n_tiles_max = n // tmm + N_BUCKETS
    bucket, rank = bucket[0], rank[0]
    cnt = counts[:N_BUCKETS, 0].astype(jnp.int32)
    padded = (cnt + tmm - 1) // tmm * tmm
    ends = jnp.cumsum(padded)
    starts = ends - padded
    pos = starts[bucket] + rank
    n_tiles = ends[-1] // tmm
    tile = jnp.minimum(jnp.arange(n_tiles_max, dtype=jnp.int32), n_tiles - 1)
    tile_bucket = jnp.sum((tile * tmm)[:, None] >= ends[None, :], axis=1).astype(jnp.int32)
    grp, pair = tile_bucket // PAIRS_PER_GROUP, tile_bucket % PAIRS_PER_GROUP
    tile_ea = grp * EXPERTS_PER_GROUP + jnp.asarray(PAIR_LO, jnp.int32)[pair]
    tile_eb = grp * EXPERTS_PER_GROUP + jnp.asarray(PAIR_HI, jnp.int32)[pair]
    token = jnp.arange(n, dtype=jnp.int32)
    order = jnp.concatenate([jnp.sort(bucket * n + token) % n, token[:tmm]])
    first = jnp.cumsum(cnt) - cnt
    tile_first = first[tile_bucket] + tile * tmm - starts[tile_bucket]
    return pos, order, tile_first, tile_ea, tile_eb, n_tiles.reshape(1)


def _moe_sparse(x1t, gffn, w_group, b_group, w_router, b_router, order, tile_first, tile_ea, tile_eb, n_tiles,
                w_gate, w_up, w_down, layer, tmm):
    n_tiles_max = tile_ea.shape[0]
    _, ne, d, de = w_gate.shape
    up_spec = lambda which: pl.BlockSpec((1, 1, d, de), lambda t, ea, eb, *_: (layer, (ea, eb)[which][t], 0, 0))
    down_spec = lambda which: pl.BlockSpec((1, 1, de, d), lambda t, ea, eb, *_: (layer, (ea, eb)[which][t], 0, 0))
    wg, wu, wd = w_gate, w_up, w_down
    pad = LANES - N_GROUPS - N_EXPERTS
    wr = jnp.concatenate([w_group, w_router, jnp.zeros((d, pad), F32)], axis=1)
    rb = jnp.concatenate([b_group, b_router, jnp.zeros((pad,), F32)])[None, :]
    return pl.pallas_call(
        _moe_sparse_kernel,
        grid_spec=pltpu.PrefetchScalarGridSpec(
            num_scalar_prefetch=5,
            grid=(n_tiles_max,),
            in_specs=[pl.BlockSpec((1, d), lambda t, *_: (0, 0)),
                      pl.BlockSpec((d, LANES), lambda t, *_: (0, 0)),
                      pl.BlockSpec((1, LANES), lambda t, *_: (0, 0)),
                      pl.BlockSpec(memory_space=pl.ANY),
                      up_spec(0), up_spec(0), down_spec(0), up_spec(1), up_spec(1), down_spec(1)],
            out_specs=pl.BlockSpec((tmm * TOKEN_TILE_ROWS, LANES), lambda t, *_: (t, 0)),
            scratch_shapes=[pltpu.VMEM((2, tmm * TOKEN_TILE_ROWS, LANES), F32), pltpu.SemaphoreType.DMA((2,))]),
        out_shape=jax.ShapeDtypeStruct((n_tiles_max * tmm * TOKEN_TILE_ROWS, LANES), F32),
        compiler_params=_params("arbitrary"),
        name="moe_sparse",
    )(tile_ea, tile_eb, n_tiles, tile_first, order, gffn[None, :], wr, rb, x1t, wg, wu, wd, wg, wu, wd)


def _unpermute(sorted_rows, pos, tm=256):
    n = pos.shape[0]
    d = TOKEN_TILE_ROWS * LANES
    pos2 = jnp.concatenate([pos, jnp.zeros((tm,), jnp.int32)]).reshape(n // tm + 1, 1, tm)
    idx_spec = lambda off: pl.BlockSpec((1, 1, tm), lambda t: (t + off, 0, 0), memory_space=pltpu.SMEM)
    return pl.pallas_call(
        _unpermute_kernel,
        grid=(n // tm,),
        in_specs=[idx_spec(0), idx_spec(1), pl.BlockSpec(memory_space=pl.ANY)],
        out_specs=pl.BlockSpec((tm, d), lambda t: (t, 0)),
        out_shape=jax.ShapeDtypeStruct((n, d), F32),
        scratch_shapes=[pltpu.VMEM((2, tm * TOKEN_TILE_ROWS, LANES), F32), pltpu.SemaphoreType.DMA((2,))],
        compiler_params=_params("arbitrary"),
        name="unpermute",
    )(pos2, pos2, sorted_rows)


def kernel(x, mix_norm, ffn_norm, w_in_ab, mla_q_a_norm, mla_w_q_up, mla_kv_a_norm, mla_w_kv_up, mla_qn_gain, mla_kn_gain, mla_qr_gain, mla_kr_gain, gqa_q_gain, gqa_k_gain, w_out_ab, w_in_c, win_q_gain, win_k_gain, win_sink, w_out_c, rel_bias, moe_w_group, moe_b_group, moe_w_router, moe_b_router, moe_w_gate, moe_w_up, moe_w_down):
    b, s, d = x.shape
    n = b * s
    depth = mix_norm.shape[0]
    x2 = x.reshape(n, d)
    bias = None
    for layer in range(depth):
        i = layer // 2
        if layer % 2 == 0:
            qa, ka, va, qb, kb, vb = _prep_ab(
                x2, s, mix_norm[layer], w_in_ab[i], mla_q_a_norm[i], mla_w_q_up[i], mla_kv_a_norm[i],
                mla_w_kv_up[i], mla_qn_gain[i], mla_kn_gain[i], mla_qr_gain[i], mla_kr_gain[i],
                gqa_q_gain[i], gqa_k_gain[i])
            r3 = lambda t: t.reshape(b, s, t.shape[1])
            out_a = _mla_attn(r3(qa), r3(ka), r3(va)).reshape(n, 512)
            out_b = _gqa_attn(r3(qb), r3(kb), r3(vb)).reshape(n, 512)
            parts, ws = [out_a, out_b], [w_out_ab[i][:512], w_out_ab[i][512:]]
        else:
            if bias is None:
                bias = _rel_bias_table(rel_bias)
            qc, kc, vc = _prep_c(x2, mix_norm[layer], w_in_c[i], win_q_gain[i], win_k_gain[i])
            r3 = lambda t: t.reshape(b, s, t.shape[1])
            out_c = _win_attn(r3(qc), r3(kc), r3(vc), win_sink[i], bias).reshape(n, 1024)
            parts, ws = [out_c], [w_out_c[i]]
        router = (moe_w_group[layer], moe_b_group[layer], moe_w_router[layer], moe_b_router[layer])
        x1t, bucket, rank, counts = _post(parts, ws, x2, ffn_norm[layer], *router)
        pos, order, tile_first, tile_ea, tile_eb, n_tiles = _route(bucket, rank, counts, MOE_TILE)
        y_sorted = _moe_sparse(x1t, ffn_norm[layer], *router, order, tile_first, tile_ea, tile_eb, n_tiles,
                               moe_w_gate, moe_w_up, moe_w_down, layer, MOE_TILE)
        x2 = _unpermute(y_sorted, pos)
    return x2.reshape(b, s, d)
```

```python
import functools
import math

import numpy as np
import jax
import jax.numpy as jnp
from jax import lax
from jax.experimental import pallas as pl
from jax.experimental.pallas import tpu as pltpu

F32 = jnp.float32
BF16 = jnp.bfloat16

EPS = 1e-6
ROPE_THETA = 10000.0
LANES = 128
HEAD = 64
ROPE_HALF = 16
GRID_W = 64
WINDOW = 128
Q_BLOCK = 128
REL_BUCKETS = 32
REL_MAX_DIST = 128
N_GROUPS = 4
EXPERTS_PER_GROUP = 4
N_EXPERTS = 16
PAIRS_PER_GROUP = 6
N_BUCKETS = N_GROUPS * PAIRS_PER_GROUP
PAIR_LO = (0, 0, 0, 1, 1, 2)
PAIR_HI = (1, 2, 3, 2, 3, 3)
TOKEN_TILE_ROWS = 8
KEY_CHUNK = 512
LOG2E = math.log2(math.e)
PREP_SUB = 256
WIN_ROW_TILE = 32
GATHER_UNROLL = 8
MOE_TILE = 256
ROUTER_ROWS = 32
VMEM_LIMIT = 56 * 1024 * 1024

_NT = (((1,), (1,)), ((), ()))


def _dot(a, b):
    return jnp.dot(a, b, preferred_element_type=F32)


def _dot_nt(a, b):
    return lax.dot_general(a, b, _NT, preferred_element_type=F32)


def _split_bf16(a):
    hi = a.astype(BF16)
    lo = (a - hi.astype(F32)).astype(BF16)
    return hi, lo


def _row_rmsnorm(t, gain):
    return t * lax.rsqrt(jnp.mean(t * t, axis=-1, keepdims=True) + EPS) * gain


def _seg_rmsnorm(t, mseg, msegt, invlen, gain):
    sums = _dot((t * t).astype(BF16), mseg)
    inv = lax.rsqrt(sums * invlen + EPS)
    ihi, ilo = _split_bf16(inv)
    scale = _dot(jnp.concatenate([ihi, ilo], axis=1), msegt)
    return t * scale * gain


def _rope128(t, cos, sin_signed, first_half):
    up = pltpu.roll(t, LANES - ROPE_HALF, 1)
    dn = pltpu.roll(t, ROPE_HALF, 1)
    return t * cos + jnp.where(first_half, up, dn) * sin_signed


def _dup_halves(blk, lo):
    sw = pltpu.roll(blk, HEAD, 1)
    return jnp.where(lo, blk, sw), jnp.where(lo, sw, blk)


def _lane_masks():
    lane = lax.broadcasted_iota(jnp.int32, (1, LANES), 1)
    return lane < HEAD, (lane % (2 * ROPE_HALF)) < ROPE_HALF


def _by_sub_tiles(rows_fn, refs, n_const):
    row_refs = (refs[0],) + tuple(refs[1 + n_const:])
    for h in range(refs[0].shape[0] // PREP_SUB):
        view = [r.at[pl.ds(PREP_SUB * h, PREP_SUB)] for r in row_refs]
        rows_fn(view[0], *refs[1:1 + n_const], *view[1:])


SEG_NORM_CONSTS = 4


def _prep_ab_kernel(*refs):
    _by_sub_tiles(_prep_ab_rows, refs, 6 + 5 * SEG_NORM_CONSTS)


def _prep_c_kernel(*refs):
    _by_sub_tiles(_prep_c_rows, refs, 2 + 2 * SEG_NORM_CONSTS)


def _prep_ab_rows(x_ref, gmix_ref, w0_ref, qag_ref, wq_ref, kvag_ref, wkv_ref,
                    mq_ref, mqt_ref, ilq_ref, gq_ref,
                    mk_ref, mkt_ref, ilk_ref, gk_ref,
                    mr_ref, mrt_ref, ilr_ref, gr_ref,
                    mbq_ref, mbqt_ref, ilbq_ref, gbq_ref,
                    mbk_ref, mbkt_ref, ilbk_ref, gbk_ref,
                    cosa_ref, sina_ref, cosb_ref, sinb_ref,
                    qa_ref, ka_ref, va_ref, qb_ref, kb_ref, vb_ref):
    lo, first_half = _lane_masks()
    h = _row_rmsnorm(x_ref[...], gmix_ref[...]).astype(BF16)
    proj = _dot(h, w0_ref[...])
    cosa, sina = cosa_ref[...], sina_ref[...]
    cosb, sinb = cosb_ref[...], sinb_ref[...]

    qn = _row_rmsnorm(proj[:, 0:256], qag_ref[...]).astype(BF16)
    q = _seg_rmsnorm(_dot(qn, wq_ref[...]), mq_ref[...], mqt_ref[...], ilq_ref[...], gq_ref[...])
    for hd in range(8):
        sl = slice(LANES * hd, LANES * (hd + 1))
        qa_ref[:, sl] = _rope128(q[:, sl], cosa, sina, first_half).astype(BF16)

    kvn = _row_rmsnorm(proj[:, 256:384], kvag_ref[...]).astype(BF16)
    kv = _dot(kvn, wkv_ref[...])
    ones_hi = jnp.where(lo, 0.0, 1.0)
    kn = _seg_rmsnorm(kv[:, 0:1024], mk_ref[...], mkt_ref[...], ilk_ref[...], gk_ref[...])
    kr = _seg_rmsnorm(proj[:, 384:512], mr_ref[...], mrt_ref[...], ilr_ref[...], gr_ref[...])
    kr = _rope128(kr, cosa, sina, first_half)
    for hd in range(8):
        sl = slice(LANES * hd, LANES * (hd + 1))
        ka_ref[:, sl] = (kn[:, sl] + kr).astype(BF16)
        va_ref[:, sl] = (kv[:, 1024 + LANES * hd:1024 + LANES * (hd + 1)] + ones_hi).astype(BF16)

    bq = _seg_rmsnorm(proj[:, 512:1024], mbq_ref[...], mbqt_ref[...], ilbq_ref[...], gbq_ref[...])
    for blk in range(4):
        sl = slice(LANES * blk, LANES * (blk + 1))
        qb_ref[:, sl] = _rope128(bq[:, sl], cosb, sinb, first_half).astype(BF16)
    bk = _seg_rmsnorm(proj[:, 1024:1152], mbk_ref[...], mbkt_ref[...], ilbk_ref[...], gbk_ref[...])
    bk = _rope128(bk, cosb, sinb, first_half)
    k0, k1 = _dup_halves(bk, lo)
    kb_ref[:, 0:LANES] = k0.astype(BF16)
    kb_ref[:, LANES:2 * LANES] = k1.astype(BF16)
    bv = proj[:, 1152:1280]
    vb_ref[:, 0:LANES] = jnp.where(lo, bv, 1.0).astype(BF16)
    vb_ref[:, LANES:2 * LANES] = jnp.where(lo, pltpu.roll(bv, HEAD, 1), 1.0).astype(BF16)


def _prep_c_rows(x_ref, gmix_ref, w_ref,
                   mq_ref, mqt_ref, ilq_ref, gq_ref,
                   mk_ref, mkt_ref, ilk_ref, gk_ref,
                   qc_ref, kc_ref, vc_ref):
    lo, _ = _lane_masks()
    h = _row_rmsnorm(x_ref[...], gmix_ref[...]).astype(BF16)
    proj = _dot(h, w_ref[...])
    q = _seg_rmsnorm(proj[:, 0:1024], mq_ref[...], mqt_ref[...], ilq_ref[...], gq_ref[...])
    qc_ref[...] = q.astype(BF16)
    k = _seg_rmsnorm(proj[:, 1024:1280], mk_ref[...], mkt_ref[...], ilk_ref[...], gk_ref[...])
    for blk in range(2):
        sl = slice(LANES * blk, LANES * (blk + 1))
        k0, k1 = _dup_halves(k[:, sl], lo)
        kc_ref[:, 2 * blk * LANES:(2 * blk + 1) * LANES] = k0.astype(BF16)
        kc_ref[:, (2 * blk + 1) * LANES:(2 * blk + 2) * LANES] = k1.astype(BF16)
        v = proj[:, 1280 + LANES * blk:1280 + LANES * (blk + 1)]
        vc_ref[:, 2 * blk * LANES:(2 * blk + 1) * LANES] = jnp.where(lo, v, 1.0).astype(BF16)
        vc_ref[:, (2 * blk + 1) * LANES:(2 * blk + 2) * LANES] = jnp.where(
            lo, pltpu.roll(v, HEAD, 1), 1.0).astype(BF16)


def _softmax_pv(s, v):
    m = jnp.max(s, axis=-1, keepdims=True)
    p = jnp.exp(s - m)
    l = jnp.sum(p, axis=-1, keepdims=True)
    return _dot(p.astype(BF16), v) / l


def _lane_chunk_reduce(op, t):
    out = t[:, 0:LANES]
    for j in range(1, t.shape[1] // LANES):
        out = op(out, t[:, LANES * j:LANES * (j + 1)])
    return out


def _scores_phase(q, k_ref, lanes, s_buf):
    m_part = None
    for c in range(k_ref.shape[1] // KEY_CHUNK):
        ks = slice(KEY_CHUNK * c, KEY_CHUNK * (c + 1))
        s_c = _dot_nt(q, k_ref[0, ks, lanes])
        s_buf[:, ks] = s_c
        mc = _lane_chunk_reduce(jnp.maximum, s_c)
        m_part = mc if m_part is None else jnp.maximum(m_part, mc)
    return jnp.max(m_part, axis=-1, keepdims=True)


def _pv_phase(s_buf, m, v_ref, lanes):
    lo, _ = _lane_masks()
    acc = None
    for c in range(v_ref.shape[1] // KEY_CHUNK):
        ks = slice(KEY_CHUNK * c, KEY_CHUNK * (c + 1))
        p = jnp.exp2(s_buf[:, ks] - m)
        pv = _dot(p.astype(BF16), v_ref[0, ks, lanes])
        acc = pv if acc is None else acc + pv
    return acc / jnp.where(lo, pltpu.roll(acc, HEAD, 1), 1.0)


def _pair_heads(o_even, o_odd):
    lo, _ = _lane_masks()
    return jnp.where(lo, o_even, pltpu.roll(o_odd, HEAD, 1))


def _mla_attn_kernel(q_ref, k_ref, v_ref, o_ref, s0_ref, s1_ref):
    first, second = slice(0, LANES), slice(LANES, 2 * LANES)
    m0 = _scores_phase(q_ref[0, :, first], k_ref, first, s0_ref)
    m1 = _scores_phase(q_ref[0, :, second], k_ref, second, s1_ref)
    o0 = _pv_phase(s0_ref, m0, v_ref, first)
    o1 = _pv_phase(s1_ref, m1, v_ref, second)
    o_ref[0] = _pair_heads(o0, o1).astype(o_ref.dtype)


def _gqa_attn_kernel(q_ref, k_ref, v_ref, o_ref, s0_ref, s1_ref):
    lo, _ = _lane_masks()
    tq = q_ref.shape[1]
    hi = jnp.logical_not(lo)
    ms = []
    for j, s_ref in enumerate((s0_ref, s1_ref)):
        blk = q_ref[0, :, LANES * j:LANES * (j + 1)]
        zero = jnp.zeros_like(blk)
        qs = jnp.concatenate([jnp.where(lo, blk, zero), jnp.where(hi, blk, zero)], axis=0)
        ms.append(_scores_phase(qs, k_ref, slice(0, LANES), s_ref))
    for j, s_ref in enumerate((s0_ref, s1_ref)):
        o = _pv_phase(s_ref, ms[j], v_ref, slice(0, LANES))
        o_ref[0, :, LANES * j:LANES * (j + 1)] = _pair_heads(o[0:tq], o[tq:2 * tq]).astype(o_ref.dtype)


def _win_attn_kernel(sink_ref, q_ref, kp_ref, kc_ref, kn_ref, vp_ref, vc_ref, vn_ref, bias_ref, o_ref,
                     s_scr, p_scr, m_scr, *, seq_len):
    lo, _ = _lane_masks()
    tq = Q_BLOCK
    nq = q_ref.shape[1] // tq
    i = pl.program_id(1)
    kj = lax.broadcasted_iota(jnp.int32, (1, 3 * tq), 1)
    units = [(g, u) for g in range(4) for u in range(nq)]
    for idx, (g, u) in enumerate(units):
        sl = slice(LANES * g, LANES * (g + 1))
        kcat = jnp.concatenate([kp_ref[0, :, sl], kc_ref[0, :, sl], kn_ref[0, :, sl]], axis=0)
        parts = []
        for a in range(4):
            blk = q_ref[0, tq * u:tq * (u + 1), LANES * (2 * g + a // 2):LANES * (2 * g + a // 2 + 1)]
            keep = lo if a % 2 == 0 else jnp.logical_not(lo)
            parts.append(jnp.where(keep, blk, jnp.zeros_like(blk)))
        s_scr[idx] = _dot_nt(jnp.concatenate(parts, axis=0), kcat[tq * u:tq * (u + 3)])
    for idx, (g, u) in enumerate(units):
        key_pos = (i * nq + u - 1) * tq + kj
        valid = jnp.logical_and(key_pos >= 0, key_pos < seq_len)
        for rt in range(4 * tq // WIN_ROW_TILE):
            a, r0 = divmod(rt * WIN_ROW_TILE, tq)
            rows = slice(rt * WIN_ROW_TILE, (rt + 1) * WIN_ROW_TILE)
            s = s_scr[idx, rows, :] + bias_ref[4 * g + a, r0:r0 + WIN_ROW_TILE, :]
            s = jnp.where(valid, s, -jnp.inf)
            m = jnp.maximum(jnp.max(_lane_chunk_reduce(jnp.maximum, s), axis=-1, keepdims=True),
                            sink_ref[4 * g + a] * LOG2E)
            p_scr[idx, rows, :] = jnp.exp2(s - m).astype(BF16)
            m_scr[idx, rows, :] = jnp.broadcast_to(m, (WIN_ROW_TILE, LANES))
    for idx, (g, u) in enumerate(units):
        sl = slice(LANES * g, LANES * (g + 1))
        vcat = jnp.concatenate([vp_ref[0, :, sl], vc_ref[0, :, sl], vn_ref[0, :, sl]], axis=0)
        acc = _dot(p_scr[idx], vcat[tq * u:tq * (u + 3)])
        outs = []
        for a in range(4):
            rows = slice(tq * a, tq * (a + 1))
            sink_term = jnp.exp2(sink_ref[4 * g + a] * LOG2E - m_scr[idx, rows, :])
            den = jnp.where(lo, pltpu.roll(acc[rows], HEAD, 1) + sink_term, 1.0)
            outs.append(acc[rows] / den)
        for j in range(2):
            blk = 2 * g + j
            o_ref[0, tq * u:tq * (u + 1), LANES * blk:LANES * (blk + 1)] = _pair_heads(
                outs[2 * j], outs[2 * j + 1]).astype(o_ref.dtype)


def _rel_bias_kernel(bucket_ref, band_ref, relt_ref, o_ref):
    bucket = bucket_ref[...]
    acc = jnp.zeros(o_ref.shape, F32)
    for r in range(REL_BUCKETS):
        acc = acc + jnp.where(bucket == r, relt_ref[:, r:r + 1], 0.0)
    o_ref[...] = jnp.where(band_ref[...] > 0, acc * LOG2E, -jnp.inf)


def _to_tiles(ref, val, first_token=0):
    for c in range(TOKEN_TILE_ROWS):
        ref[pl.ds(first_token * TOKEN_TILE_ROWS + c, val.shape[0], stride=TOKEN_TILE_ROWS), :] = (
            val[:, LANES * c:LANES * (c + 1)])


def _from_tiles(ref):
    rows = ref.shape[0] // TOKEN_TILE_ROWS
    return jnp.concatenate([ref[pl.ds(c, rows, stride=TOKEN_TILE_ROWS), :] for c in range(TOKEN_TILE_ROWS)],
                           axis=1)


def _route_bucket(xn, wh, wl, rb):
    xh, xl = _split_bf16(xn)
    logit = _dot_nt(wh, xh) + _dot_nt(wh, xl) + _dot_nt(wl, xh) + rb
    g = [logit[r:r + 1, :] for r in range(N_GROUPS)]
    gmax = jnp.maximum(jnp.maximum(g[0], g[1]), jnp.maximum(g[2], g[3]))
    gidx = jnp.where(g[0] == gmax, 0, jnp.where(g[1] == gmax, 1, jnp.where(g[2] == gmax, 2, 3)))
    e = []
    for j in range(EXPERTS_PER_GROUP):
        rows = [logit[N_GROUPS + EXPERTS_PER_GROUP * gg + j:N_GROUPS + EXPERTS_PER_GROUP * gg + j + 1, :]
                for gg in range(N_GROUPS)]
        e.append(jnp.where(gidx == 0, rows[0], jnp.where(gidx == 1, rows[1],
                                                         jnp.where(gidx == 2, rows[2], rows[3]))))
    emax = jnp.maximum(jnp.maximum(e[0], e[1]), jnp.maximum(e[2], e[3]))
    ex = [jnp.exp(ej - emax) for ej in e]
    esum = ex[0] + ex[1] + ex[2] + ex[3]
    pr = [exj / esum for exj in ex]
    p1 = jnp.maximum(jnp.maximum(pr[0], pr[1]), jnp.maximum(pr[2], pr[3]))
    i1 = jnp.where(pr[0] == p1, 0, jnp.where(pr[1] == p1, 1, jnp.where(pr[2] == p1, 2, 3)))
    rest = [jnp.where(i1 == j, -1.0, pr[j]) for j in range(EXPERTS_PER_GROUP)]
    p2 = jnp.maximum(jnp.maximum(rest[0], rest[1]), jnp.maximum(rest[2], rest[3]))
    i2 = jnp.where(rest[0] == p2, 0, jnp.where(rest[1] == p2, 1, jnp.where(rest[2] == p2, 2, 3)))
    lo_e = jnp.minimum(i1, i2)
    hi_e = jnp.maximum(i1, i2)
    pair = jnp.where(lo_e == 0, hi_e - 1, jnp.where(lo_e == 1, hi_e + 1, 5))
    return gidx * PAIRS_PER_GROUP + pair


def _post_kernel(*refs, n_parts):
    parts = refs[:n_parts]
    ws = refs[n_parts:2 * n_parts]
    (x_ref, gffn_ref, wrh_ref, wrl_ref, rb_ref, tri_ref,
     x1t_ref, bucket_ref, rank_ref, counts_ref, carry_ref) = refs[2 * n_parts:]

    @pl.when(pl.program_id(0) == 0)
    def _():
        carry_ref[...] = jnp.zeros_like(carry_ref)

    sub = tri_ref.shape[0]
    buckets = []
    for h in range(x_ref.shape[0] // sub):
        rows = slice(sub * h, sub * (h + 1))
        acc = x_ref[rows, :]
        for p_ref, w_ref in zip(parts, ws):
            acc = acc + _dot(p_ref[rows, :], w_ref[...])
        _to_tiles(x1t_ref, acc, sub * h)
        bucket = _route_bucket(_row_rmsnorm(acc, gffn_ref[...]), wrh_ref[...], wrl_ref[...], rb_ref[...])
        bucket_ref[:, rows] = bucket
        buckets.append(bucket)
    for h, bucket in enumerate(buckets):
        rows = slice(sub * h, sub * (h + 1))
        onehot = (lax.broadcasted_iota(jnp.int32, (ROUTER_ROWS, sub), 0) == bucket).astype(F32)
        before = _dot(onehot.astype(BF16), tri_ref[...]) + carry_ref[:, 0:1]
        rank_ref[:, rows] = jnp.sum(onehot * before, axis=0, keepdims=True).astype(jnp.int32)
        carry_ref[...] = carry_ref[...] + jnp.sum(onehot, axis=1, keepdims=True)
    counts_ref[...] = carry_ref[...]


def _start_row_gather(idx_of, table_hbm, dst, sem):
    def body(blk, carry):
        for j in range(GATHER_UNROLL):
            r = blk * GATHER_UNROLL + j
            src_row = pl.multiple_of(idx_of(r) * TOKEN_TILE_ROWS, TOKEN_TILE_ROWS)
            dst_row = pl.multiple_of(r * TOKEN_TILE_ROWS, TOKEN_TILE_ROWS)
            pltpu.make_async_copy(table_hbm.at[pl.ds(src_row, TOKEN_TILE_ROWS)],
                                  dst.at[pl.ds(dst_row, TOKEN_TILE_ROWS)], sem).start()
        return carry
    lax.fori_loop(0, dst.shape[0] // (TOKEN_TILE_ROWS * GATHER_UNROLL), body, 0)


def _wait_row_gather(table_hbm, dst, sem):
    pltpu.make_async_copy(table_hbm.at[pl.ds(0, dst.shape[0])], dst, sem).wait()


def _tile_gates(logit, ea, eb):
    lane = lax.broadcasted_iota(jnp.int32, (1, LANES), 1)
    pick = lambda idx: jnp.sum(jnp.where(lane == idx, logit, 0.0), axis=-1, keepdims=True)
    glog = jnp.where(lane < N_GROUPS, logit, -jnp.inf)
    gmax = jnp.max(glog, axis=-1, keepdims=True)
    gsum = jnp.sum(jnp.exp(glog - gmax), axis=-1, keepdims=True)
    g_p = jnp.exp(pick(ea // EXPERTS_PER_GROUP) - gmax) / gsum
    la, lb = pick(N_GROUPS + ea), pick(N_GROUPS + eb)
    top = jnp.maximum(la, lb)
    pa, pb = jnp.exp(la - top), jnp.exp(lb - top)
    return g_p * pa / (pa + pb), g_p * pb / (pa + pb)


def _moe_sparse_kernel(ea_ref, eb_ref, nt_ref, first_ref, order_ref, gffn_ref, wr_ref, rb_ref, x1t_hbm,
                       wga_ref, wua_ref, wda_ref, wgb_ref, wub_ref, wdb_ref, o_ref, buf, sem):
    t = pl.program_id(0)
    nt = nt_ref[0]
    slot = t % 2

    def tokens_of(tile):
        base = first_ref[tile]
        return lambda r: order_ref[base + r]

    @pl.when(t == 0)
    def _():
        _start_row_gather(tokens_of(0), x1t_hbm, buf.at[0], sem.at[0])

    @pl.when(t + 1 < nt)
    def _():
        _start_row_gather(tokens_of(t + 1), x1t_hbm, buf.at[1 - slot], sem.at[1 - slot])

    @pl.when(t < nt)
    def _():
        _wait_row_gather(x1t_hbm, buf.at[slot], sem.at[slot])
        x = _from_tiles(buf.at[slot])
        xn = _row_rmsnorm(x, gffn_ref[...])
        gate_a, gate_b = _tile_gates(_dot(xn, wr_ref[...]) + rb_ref[...], ea_ref[t], eb_ref[t])
        hid_a = jax.nn.silu(_dot(xn, wga_ref[0, 0])) * _dot(xn, wua_ref[0, 0])
        hid_b = jax.nn.silu(_dot(xn, wgb_ref[0, 0])) * _dot(xn, wub_ref[0, 0])
        out = x + gate_a * _dot(hid_a, wda_ref[0, 0]) + gate_b * _dot(hid_b, wdb_ref[0, 0])
        _to_tiles(o_ref, out)

    @pl.when(t >= nt)
    def _():
        o_ref[...] = jnp.zeros_like(o_ref)


def _unpermute_kernel(pos_cur_ref, pos_nxt_ref, sorted_hbm, o_ref, buf, sem):
    t = pl.program_id(0)
    slot = t % 2

    @pl.when(t == 0)
    def _():
        _start_row_gather(lambda r: pos_cur_ref[0, 0, r], sorted_hbm, buf.at[0], sem.at[0])

    @pl.when(t + 1 < pl.num_programs(0))
    def _():
        _start_row_gather(lambda r: pos_nxt_ref[0, 0, r], sorted_hbm, buf.at[1 - slot], sem.at[1 - slot])

    _wait_row_gather(sorted_hbm, buf.at[slot], sem.at[slot])
    o_ref[...] = _from_tiles(buf.at[slot])


def _seg_mats(width, segments):
    m = np.zeros((width, LANES), np.float32)
    invlen = np.ones((1, LANES), np.float32)
    for c, (start, length) in enumerate(segments):
        m[start:start + length, c] = 1.0
        invlen[0, c] = 1.0 / length
    return jnp.asarray(m, BF16), jnp.asarray(np.concatenate([m.T, m.T], axis=0), BF16), jnp.asarray(invlen)


def _rope_tables(pos_list, seq_len):
    inv = np.float32(ROPE_THETA) ** (-np.arange(0, 2 * ROPE_HALF, 2, dtype=np.float32) / np.float32(2 * ROPE_HALF))
    cos_cols, sin_cols = [], []
    for pos in pos_list:
        if pos is None:
            cos_cols.append(np.ones((seq_len, 2 * ROPE_HALF), np.float32))
            sin_cols.append(np.zeros((seq_len, 2 * ROPE_HALF), np.float32))
        else:
            ang = pos.astype(np.float32)[:, None] * inv[None, :]
            c, s = np.cos(ang), np.sin(ang)
            cos_cols.append(np.concatenate([c, c], axis=1))
            sin_cols.append(np.concatenate([-s, s], axis=1))
    return jnp.asarray(np.concatenate(cos_cols, axis=1)), jnp.asarray(np.concatenate(sin_cols, axis=1))


def _t5_bucket_np(rel):
    nb = REL_BUCKETS // 2
    max_exact = nb // 2
    ret = np.where(rel > 0, nb, 0)
    n = np.abs(rel)
    nf = np.maximum(n, 1).astype(np.float32)
    large = max_exact + (np.log(nf / np.float32(max_exact)) / np.float32(math.log(REL_MAX_DIST / max_exact))
                         * np.float32(nb - max_exact)).astype(np.int32)
    large = np.minimum(large, nb - 1)
    return ret + np.where(n < max_exact, n, large)


def _full(shape):
    nd = len(shape)
    return pl.BlockSpec(shape, lambda *_: (0,) * nd)


def _params(*sem):
    return pltpu.CompilerParams(dimension_semantics=sem, vmem_limit_bytes=VMEM_LIMIT)


def _prep_ab(x2, seq_len, gmix, w_in, qag, wq, kvag, wkv, qn_g, kn_g, qr_g, kr_g, bq_g, bk_g, tm=512):
    n = x2.shape[0]
    scale_a = (HEAD + 2 * ROPE_HALF) ** -0.5
    scale_b = HEAD ** -0.5
    zeros = lambda r, c: jnp.zeros((r, c), F32)
    d = w_in.shape[0]
    w0 = jnp.concatenate([w_in[:, 0:384], zeros(d, HEAD), w_in[:, 384:416], zeros(d, 32), w_in[:, 416:1184]],
                         axis=1).astype(BF16)
    wq_p = jnp.concatenate([wq.reshape(-1, 8, 96), jnp.zeros((wq.shape[0], 8, 32), F32)], axis=2)
    wq_p = wq_p.reshape(-1, 1024).astype(BF16)
    wkv_r = wkv.reshape(-1, 8, 128)
    wk_p = jnp.concatenate([wkv_r[:, :, :HEAD], jnp.zeros_like(wkv_r[:, :, :HEAD])], axis=2).reshape(-1, 1024)
    wv_p = jnp.concatenate([wkv_r[:, :, HEAD:], jnp.zeros_like(wkv_r[:, :, HEAD:])], axis=2).reshape(-1, 1024)
    wkv_p = jnp.concatenate([wk_p, wv_p], axis=1).astype(BF16)

    seg_q = [(LANES * h, HEAD) for h in range(8)] + [(LANES * h + HEAD, 32) for h in range(8)]
    gq = jnp.tile(jnp.concatenate([qn_g, qr_g, jnp.zeros((32,), F32)]), 8)[None, :] * (scale_a * LOG2E)
    seg_k = [(LANES * h, HEAD) for h in range(8)]
    gk = jnp.tile(jnp.concatenate([kn_g, jnp.zeros((HEAD,), F32)]), 8)[None, :]
    seg_r = [(HEAD, 32)]
    gr = jnp.concatenate([jnp.zeros((HEAD,), F32), kr_g, jnp.zeros((32,), F32)])[None, :]
    seg_bq = [(HEAD * h, HEAD) for h in range(8)]
    gbq = jnp.tile(bq_g, 8)[None, :] * (scale_b * LOG2E)
    seg_bk = [(HEAD * h, HEAD) for h in range(2)]
    gbk = jnp.tile(bk_g, 2)[None, :]

    pos = np.arange(seq_len)
    cosa, sina = _rope_tables([None, None, pos, None], seq_len)
    row, col = pos // GRID_W, pos % GRID_W
    cosb, sinb = _rope_tables([row, col, row, col], seq_len)

    consts = [gmix[None, :], w0, qag[None, :], wq_p, kvag[None, :], wkv_p,
              *_seg_mats(1024, seg_q), gq, *_seg_mats(1024, seg_k), gk, *_seg_mats(LANES, seg_r), gr,
              *_seg_mats(512, seg_bq), gbq, *_seg_mats(LANES, seg_bk), gbk]
    nsb = seq_len // tm
    rope_spec = pl.BlockSpec((tm, LANES), lambda i: (i % nsb, 0))
    row_spec = lambda w: pl.BlockSpec((tm, w), lambda i: (i, 0))
    out_widths = (1024, 1024, 1024, 512, 256, 256)
    return pl.pallas_call(
        _prep_ab_kernel,
        grid=(n // tm,),
        in_specs=[row_spec(x2.shape[1])] + [_full(c.shape) for c in consts] + [rope_spec] * 4,
        out_specs=[row_spec(w) for w in out_widths],
        out_shape=[jax.ShapeDtypeStruct((n, w), BF16) for w in out_widths],
        compiler_params=_params("parallel"),
        name="prep_ab",
    )(x2, *consts, cosa, sina, cosb, sinb)


def _prep_c(x2, gmix, w_in, q_g, k_g, tm=512):
    n = x2.shape[0]
    seg_q = [(HEAD * h, HEAD) for h in range(16)]
    gq = jnp.tile(q_g, 16)[None, :] * (HEAD ** -0.5 * LOG2E)
    seg_k = [(HEAD * h, HEAD) for h in range(4)]
    gk = jnp.tile(k_g, 4)[None, :]
    consts = [gmix[None, :], w_in.astype(BF16), *_seg_mats(1024, seg_q), gq, *_seg_mats(256, seg_k), gk]
    row_spec = lambda w: pl.BlockSpec((tm, w), lambda i: (i, 0))
    out_widths = (1024, 512, 512)
    return pl.pallas_call(
        _prep_c_kernel,
        grid=(n // tm,),
        in_specs=[row_spec(x2.shape[1])] + [_full(c.shape) for c in consts],
        out_specs=[row_spec(w) for w in out_widths],
        out_shape=[jax.ShapeDtypeStruct((n, w), BF16) for w in out_widths],
        compiler_params=_params("parallel"),
        name="prep_c",
    )(x2, *consts)


def _mla_attn(qa, ka, va, tq=512):
    b, s, _ = qa.shape
    return pl.pallas_call(
        _mla_attn_kernel,
        grid=(b, 4, s // tq),
        in_specs=[pl.BlockSpec((1, tq, 2 * LANES), lambda bi, hp, qi: (bi, qi, hp)),
                  pl.BlockSpec((1, s, 2 * LANES), lambda bi, hp, qi: (bi, 0, hp)),
                  pl.BlockSpec((1, s, 2 * LANES), lambda bi, hp, qi: (bi, 0, hp))],
        out_specs=pl.BlockSpec((1, tq, LANES), lambda bi, hp, qi: (bi, qi, hp)),
        out_shape=jax.ShapeDtypeStruct((b, s, 512), BF16),
        scratch_shapes=[pltpu.VMEM((tq, s), F32), pltpu.VMEM((tq, s), F32)],
        compiler_params=_params("parallel", "parallel", "parallel"),
        name="mla_attn",
    )(qa, ka, va)


def _gqa_attn(qb, kb, vb, tq=256):
    b, s, _ = qb.shape
    return pl.pallas_call(
        _gqa_attn_kernel,
        grid=(b, 2, s // tq),
        in_specs=[pl.BlockSpec((1, tq, 2 * LANES), lambda bi, g, qi: (bi, qi, g)),
                  pl.BlockSpec((1, s, LANES), lambda bi, g, qi: (bi, 0, g)),
                  pl.BlockSpec((1, s, LANES), lambda bi, g, qi: (bi, 0, g))],
        out_specs=pl.BlockSpec((1, tq, 2 * LANES), lambda bi, g, qi: (bi, qi, g)),
        out_shape=jax.ShapeDtypeStruct((b, s, 512), BF16),
        scratch_shapes=[pltpu.VMEM((2 * tq, s), F32), pltpu.VMEM((2 * tq, s), F32)],
        compiler_params=_params("parallel", "parallel", "parallel"),
        name="gqa_attn",
    )(qb, kb, vb)


def _rel_bias_table(rel_bias):
    span = Q_BLOCK + 2 * WINDOW
    rel = np.arange(span)[None, :] - WINDOW - np.arange(Q_BLOCK)[:, None]
    bucket = _t5_bucket_np(rel).astype(np.int32).reshape(1, -1)
    band = (np.abs(rel) <= WINDOW).astype(np.int32).reshape(1, -1)
    heads = rel_bias.shape[1]
    cols = bucket.shape[1]
    chunk = cols // 8
    table = pl.pallas_call(
        _rel_bias_kernel,
        grid=(8,),
        in_specs=[pl.BlockSpec((1, chunk), lambda i: (0, i)), pl.BlockSpec((1, chunk), lambda i: (0, i)),
                  _full((heads, REL_BUCKETS))],
        out_specs=pl.BlockSpec((heads, chunk), lambda i: (0, i)),
        out_shape=jax.ShapeDtypeStruct((heads, cols), F32),
        compiler_params=_params("parallel"),
        name="rel_bias",
    )(jnp.asarray(bucket), jnp.asarray(band), rel_bias.T)
    return table.reshape(heads, Q_BLOCK, span)


def _win_attn(qc, kc, vc, sink, bias, nq=4):
    b, s, _ = qc.shape
    tq = Q_BLOCK
    nb = s // tq
    prev = lambda bi, i: (bi, jnp.maximum(i * nq - 1, 0), 0)
    cur = lambda bi, i: (bi, i, 0)
    nxt = lambda bi, i: (bi, jnp.minimum((i + 1) * nq, nb - 1), 0)
    edge_spec = lambda im: pl.BlockSpec((1, tq, 4 * LANES), im)
    cur_spec = pl.BlockSpec((1, nq * tq, 4 * LANES), cur)
    return pl.pallas_call(
        functools.partial(_win_attn_kernel, seq_len=s),
        grid=(b, nb // nq),
        in_specs=[pl.BlockSpec(memory_space=pltpu.SMEM),
                  pl.BlockSpec((1, nq * tq, 8 * LANES), cur),
                  edge_spec(prev), cur_spec, edge_spec(nxt),
                  edge_spec(prev), cur_spec, edge_spec(nxt),
                  _full(bias.shape)],
        out_specs=pl.BlockSpec((1, nq * tq, 8 * LANES), cur),
        scratch_shapes=[pltpu.VMEM((4 * nq, 4 * tq, 3 * tq), F32), pltpu.VMEM((4 * nq, 4 * tq, 3 * tq), BF16),
                        pltpu.VMEM((4 * nq, 4 * tq, LANES), F32)],
        out_shape=jax.ShapeDtypeStruct((b, s, 1024), BF16),
        compiler_params=_params("parallel", "parallel"),
        name="win_attn",
    )(sink, qc, kc, kc, kc, vc, vc, vc, bias)


def _post(parts, ws, x2, gffn, w_group, b_group, w_router, b_router, tm=1024, sub=256):
    n, d = x2.shape
    wr = jnp.concatenate([w_group.T, w_router.T, jnp.zeros((ROUTER_ROWS - N_GROUPS - N_EXPERTS, d), F32)], axis=0)
    wrh = wr.astype(BF16)
    wrl = (wr - wrh.astype(F32)).astype(BF16)
    rb = jnp.concatenate([b_group, b_router, jnp.zeros((ROUTER_ROWS - N_GROUPS - N_EXPERTS,), F32)])[:, None]
    row_spec = lambda w: pl.BlockSpec((tm, w), lambda i: (i, 0))
    lane_spec = lambda r: pl.BlockSpec((r, tm), lambda i: (0, i))
    ws = [w.astype(BF16) for w in ws]
    tri = jnp.asarray(np.triu(np.ones((sub, sub), np.float32), 1), BF16)
    return pl.pallas_call(
        functools.partial(_post_kernel, n_parts=len(parts)),
        grid=(n // tm,),
        in_specs=[row_spec(p.shape[1]) for p in parts] + [_full(w.shape) for w in ws]
                 + [row_spec(d), _full((1, d)), _full(wrh.shape), _full(wrl.shape), _full(rb.shape),
                    _full(tri.shape)],
        out_specs=[pl.BlockSpec((tm * TOKEN_TILE_ROWS, LANES), lambda i: (i, 0)),
                   lane_spec(1), lane_spec(1), _full((ROUTER_ROWS, LANES))],
        out_shape=[jax.ShapeDtypeStruct((n * TOKEN_TILE_ROWS, LANES), F32),
                   jax.ShapeDtypeStruct((1, n), jnp.int32), jax.ShapeDtypeStruct((1, n), jnp.int32),
                   jax.ShapeDtypeStruct((ROUTER_ROWS, LANES), F32)],
        scratch_shapes=[pltpu.VMEM((ROUTER_ROWS, LANES), F32)],
        compiler_params=_params("arbitrary"),
        name="post",
    )(*parts, *ws, x2, gffn[None, :], wrh, wrl, rb, tri)


def _route(bucket, rank, counts, tmm):
    n = bucket.shape[1]
    n_tiles_max = n // tmm + N_BUCKETS
    bucket, rank = bucket[0], rank[0]
    cnt = counts[:N_BUCKETS, 0].astype(jnp.int32)
    padded = (cnt + tmm - 1) // tmm * tmm
    ends = jnp.cumsum(padded)
    starts = ends - padded
    pos = starts[bucket] + rank
    n_tiles = ends[-1] // tmm
    tile = jnp.minimum(jnp.arange(n_tiles_max, dtype=jnp.int32), n_tiles - 1)
    tile_bucket = jnp.sum((tile * tmm)[:, None] >= ends[None, :], axis=1).astype(jnp.int32)
    grp, pair = tile_bucket // PAIRS_PER_GROUP, tile_bucket % PAIRS_PER_GROUP
    tile_ea = grp * EXPERTS_PER_GROUP + jnp.asarray(PAIR_LO, jnp.int32)[pair]
    tile_eb = grp * EXPERTS_PER_GROUP + jnp.asarray(PAIR_HI, jnp.int32)[pair]
    token = jnp.arange(n, dtype=jnp.int32)
    _, by_bucket = lax.sort((bucket, token), num_keys=1, is_stable=True)
    order = jnp.concatenate([by_bucket, token[:tmm]])
    first = jnp.cumsum(cnt) - cnt
    tile_first = first[tile_bucket] + tile * tmm - starts[tile_bucket]
    return pos, order, tile_first, tile_ea, tile_eb, n_tiles.reshape(1)


def _moe_sparse(x1t, gffn, w_group, b_group, w_router, b_router, order, tile_first, tile_ea, tile_eb, n_tiles,
                w_gate, w_up, w_down, layer, tmm):
    n_tiles_max = tile_ea.shape[0]
    _, ne, d, de = w_gate.shape
    up_spec = lambda which: pl.BlockSpec((1, 1, d, de), lambda t, ea, eb, *_: (layer, (ea, eb)[which][t], 0, 0))
    down_spec = lambda which: pl.BlockSpec((1, 1, de, d), lambda t, ea, eb, *_: (layer, (ea, eb)[which][t], 0, 0))
    wg, wu, wd = w_gate, w_up, w_down
    pad = LANES - N_GROUPS - N_EXPERTS
    wr = jnp.concatenate([w_group, w_router, jnp.zeros((d, pad), F32)], axis=1)
    rb = jnp.concatenate([b_group, b_router, jnp.zeros((pad,), F32)])[None, :]
    return pl.pallas_call(
        _moe_sparse_kernel,
        grid_spec=pltpu.PrefetchScalarGridSpec(
            num_scalar_prefetch=5,
            grid=(n_tiles_max,),
            in_specs=[pl.BlockSpec((1, d), lambda t, *_: (0, 0)),
                      pl.BlockSpec((d, LANES), lambda t, *_: (0, 0)),
                      pl.BlockSpec((1, LANES), lambda t, *_: (0, 0)),
                      pl.BlockSpec(memory_space=pl.ANY),
                      up_spec(0), up_spec(0), down_spec(0), up_spec(1), up_spec(1), down_spec(1)],
            out_specs=pl.BlockSpec((tmm * TOKEN_TILE_ROWS, LANES), lambda t, *_: (t, 0)),
            scratch_shapes=[pltpu.VMEM((2, tmm * TOKEN_TILE_ROWS, LANES), F32), pltpu.SemaphoreType.DMA((2,))]),
        out_shape=jax.ShapeDtypeStruct((n_tiles_max * tmm * TOKEN_TILE_ROWS, LANES), F32),
        compiler_params=_params("arbitrary"),
        name="moe_sparse",
    )(tile_ea, tile_eb, n_tiles, tile_first, order, gffn[None, :], wr, rb, x1t, wg, wu, wd, wg, wu, wd)


def _unpermute(sorted_rows, pos, tm=256):
    n = pos.shape[0]
    d = TOKEN_TILE_ROWS * LANES
    pos2 = jnp.concatenate([pos, jnp.zeros((tm,), jnp.int32)]).reshape(n // tm + 1, 1, tm)
    idx_spec = lambda off: pl.BlockSpec((1, 1, tm), lambda t: (t + off, 0, 0), memory_space=pltpu.SMEM)
    return pl.pallas_call(
        _unpermute_kernel,
        grid=(n // tm,),
        in_specs=[idx_spec(0), idx_spec(1), pl.BlockSpec(memory_space=pl.ANY)],
        out_specs=pl.BlockSpec((tm, d), lambda t: (t, 0)),
        out_shape=jax.ShapeDtypeStruct((n, d), F32),
        scratch_shapes=[pltpu.VMEM((2, tm * TOKEN_TILE_ROWS, LANES), F32), pltpu.SemaphoreType.DMA((2,))],
        compiler_params=_params("arbitrary"),
        name="unpermute",
    )(pos2, pos2, sorted_rows)


def kernel(x, mix_norm, ffn_norm, w_in_ab, mla_q_a_norm, mla_w_q_up, mla_kv_a_norm, mla_w_kv_up, mla_qn_gain, mla_kn_gain, mla_qr_gain, mla_kr_gain, gqa_q_gain, gqa_k_gain, w_out_ab, w_in_c, win_q_gain, win_k_gain, win_sink, w_out_c, rel_bias, moe_w_group, moe_b_group, moe_w_router, moe_b_router, moe_w_gate, moe_w_up, moe_w_down):
    b, s, d = x.shape
    n = b * s
    depth = mix_norm.shape[0]
    x2 = x.reshape(n, d)
    bias = None
    for layer in range(depth):
        i = layer // 2
        if layer % 2 == 0:
            qa, ka, va, qb, kb, vb = _prep_ab(
                x2, s, mix_norm[layer], w_in_ab[i], mla_q_a_norm[i], mla_w_q_up[i], mla_kv_a_norm[i],
                mla_w_kv_up[i], mla_qn_gain[i], mla_kn_gain[i], mla_qr_gain[i], mla_kr_gain[i],
                gqa_q_gain[i], gqa_k_gain[i])
            r3 = lambda t: t.reshape(b, s, t.shape[1])
            out_a = _mla_attn(r3(qa), r3(ka), r3(va)).reshape(n, 512)
            out_b = _gqa_attn(r3(qb), r3(kb), r3(vb)).reshape(n, 512)
            parts, ws = [out_a, out_b], [w_out_ab[i][:512], w_out_ab[i][512:]]
        else:
            if bias is None:
                bias = _rel_bias_table(rel_bias)
            qc, kc, vc = _prep_c(x2, mix_norm[layer], w_in_c[i], win_q_gain[i], win_k_gain[i])
            r3 = lambda t: t.reshape(b, s, t.shape[1])
            out_c = _win_attn(r3(qc), r3(kc), r3(vc), win_sink[i], bias).reshape(n, 1024)
            parts, ws = [out_c], [w_out_c[i]]
        router = (moe_w_group[layer], moe_b_group[layer], moe_w_router[layer], moe_b_router[layer])
        x1t, bucket, rank, counts = _post(parts, ws, x2, ffn_norm[layer], *router)
        pos, order, tile_first, tile_ea, tile_eb, n_tiles = _route(bucket, rank, counts, MOE_TILE)
        y_sorted = _moe_sparse(x1t, ffn_norm[layer], *router, order, tile_first, tile_ea, tile_eb, n_tiles,
                               moe_w_gate, moe_w_up, moe_w_down, layer, MOE_TILE)
        x2 = _unpermute(y_sorted, pos)
    return x2.reshape(b, s, d)
```

```python
import functools
import math

import numpy as np
import jax
import jax.numpy as jnp
from jax import lax
from jax.experimental import pallas as pl
from jax.experimental.pallas import tpu as pltpu

F32 = jnp.float32
BF16 = jnp.bfloat16

EPS = 1e-6
ROPE_THETA = 10000.0
LANES = 128
HEAD = 64
ROPE_HALF = 16
GRID_W = 64
WINDOW = 128
Q_BLOCK = 128
REL_BUCKETS = 32
REL_MAX_DIST = 128
N_GROUPS = 4
EXPERTS_PER_GROUP = 4
N_EXPERTS = 16
PAIRS_PER_GROUP = 6
N_BUCKETS = N_GROUPS * PAIRS_PER_GROUP
PAIR_LO = (0, 0, 0, 1, 1, 2)
PAIR_HI = (1, 2, 3, 2, 3, 3)
TOKEN_TILE_ROWS = 8
KEY_CHUNK = 512
LOG2E = math.log2(math.e)
PREP_SUB = 256
WIN_ROW_TILE = 32
GATHER_UNROLL = 8
MOE_TILE = 256
ROUTER_ROWS = 32
VMEM_LIMIT = 56 * 1024 * 1024

_NT = (((1,), (1,)), ((), ()))


def _dot(a, b):
    return jnp.dot(a, b, preferred_element_type=F32)


def _dot_nt(a, b):
    return lax.dot_general(a, b, _NT, preferred_element_type=F32)


def _split_bf16(a):
    hi = a.astype(BF16)
    lo = (a - hi.astype(F32)).astype(BF16)
    return hi, lo


def _row_rmsnorm(t, gain):
    return t * lax.rsqrt(jnp.mean(t * t, axis=-1, keepdims=True) + EPS) * gain


def _seg_rmsnorm(t, mseg, msegt, invlen, gain):
    sums = _dot((t * t).astype(BF16), mseg)
    inv = lax.rsqrt(sums * invlen + EPS)
    ihi, ilo = _split_bf16(inv)
    scale = _dot(jnp.concatenate([ihi, ilo], axis=1), msegt)
    return t * scale * gain


def _rope128(t, cos, sin_signed, first_half):
    up = pltpu.roll(t, LANES - ROPE_HALF, 1)
    dn = pltpu.roll(t, ROPE_HALF, 1)
    return t * cos + jnp.where(first_half, up, dn) * sin_signed


def _dup_halves(blk, lo):
    sw = pltpu.roll(blk, HEAD, 1)
    return jnp.where(lo, blk, sw), jnp.where(lo, sw, blk)


def _lane_masks():
    lane = lax.broadcasted_iota(jnp.int32, (1, LANES), 1)
    return lane < HEAD, (lane % (2 * ROPE_HALF)) < ROPE_HALF


def _by_sub_tiles(rows_fn, refs, n_const):
    row_refs = (refs[0],) + tuple(refs[1 + n_const:])
    for h in range(refs[0].shape[0] // PREP_SUB):
        view = [r.at[pl.ds(PREP_SUB * h, PREP_SUB)] for r in row_refs]
        rows_fn(view[0][...], *refs[1:1 + n_const], *view[1:])


SEG_NORM_CONSTS = 4


def _prep_ab_kernel(*refs):
    _by_sub_tiles(_prep_ab_rows, refs, 6 + 5 * SEG_NORM_CONSTS)


def _prep_c_kernel(pos_cur_ref, pos_nxt_ref, *refs):
    n_const = 2 + 2 * SEG_NORM_CONSTS
    consts = refs[:n_const]
    sorted_hbm, x_ref = refs[n_const:n_const + 2]
    outs = refs[n_const + 2:-2]
    buf, sem = refs[-2:]
    t = pl.program_id(0)
    slot = t % 2

    @pl.when(t == 0)
    def _():
        _start_row_gather(lambda r: pos_cur_ref[0, 0, r], sorted_hbm, buf.at[0], sem.at[0])

    @pl.when(t + 1 < pl.num_programs(0))
    def _():
        _start_row_gather(lambda r: pos_nxt_ref[0, 0, r], sorted_hbm, buf.at[1 - slot], sem.at[1 - slot])

    _wait_row_gather(sorted_hbm, buf.at[slot], sem.at[slot])
    for h in range(x_ref.shape[0] // PREP_SUB):
        rows = pl.ds(PREP_SUB * h, PREP_SUB)
        x = _from_tiles(buf.at[slot, pl.ds(PREP_SUB * TOKEN_TILE_ROWS * h, PREP_SUB * TOKEN_TILE_ROWS)])
        x_ref[rows, :] = x
        _prep_c_rows(x, *consts, *[o.at[rows] for o in outs])


def _prep_ab_rows(x, gmix_ref, w0_ref, qag_ref, wq_ref, kvag_ref, wkv_ref,
                    mq_ref, mqt_ref, ilq_ref, gq_ref,
                    mk_ref, mkt_ref, ilk_ref, gk_ref,
                    mr_ref, mrt_ref, ilr_ref, gr_ref,
                    mbq_ref, mbqt_ref, ilbq_ref, gbq_ref,
                    mbk_ref, mbkt_ref, ilbk_ref, gbk_ref,
                    cosa_ref, sina_ref, cosb_ref, sinb_ref,
                    qa_ref, ka_ref, va_ref, qb_ref, kb_ref, vb_ref):
    lo, first_half = _lane_masks()
    h = _row_rmsnorm(x, gmix_ref[...]).astype(BF16)
    proj = _dot(h, w0_ref[...])
    cosa, sina = cosa_ref[...], sina_ref[...]
    cosb, sinb = cosb_ref[...], sinb_ref[...]

    qn = _row_rmsnorm(proj[:, 0:256], qag_ref[...]).astype(BF16)
    q = _seg_rmsnorm(_dot(qn, wq_ref[...]), mq_ref[...], mqt_ref[...], ilq_ref[...], gq_ref[...])
    for hd in range(8):
        sl = slice(LANES * hd, LANES * (hd + 1))
        qa_ref[:, sl] = _rope128(q[:, sl], cosa, sina, first_half).astype(BF16)

    kvn = _row_rmsnorm(proj[:, 256:384], kvag_ref[...]).astype(BF16)
    kv = _dot(kvn, wkv_ref[...])
    ones_hi = jnp.where(lo, 0.0, 1.0)
    kn = _seg_rmsnorm(kv[:, 0:1024], mk_ref[...], mkt_ref[...], ilk_ref[...], gk_ref[...])
    kr = _seg_rmsnorm(proj[:, 384:512], mr_ref[...], mrt_ref[...], ilr_ref[...], gr_ref[...])
    kr = _rope128(kr, cosa, sina, first_half)
    for hd in range(8):
        sl = slice(LANES * hd, LANES * (hd + 1))
        ka_ref[:, sl] = (kn[:, sl] + kr).astype(BF16)
        va_ref[:, sl] = (kv[:, 1024 + LANES * hd:1024 + LANES * (hd + 1)] + ones_hi).astype(BF16)

    bq = _seg_rmsnorm(proj[:, 512:1024], mbq_ref[...], mbqt_ref[...], ilbq_ref[...], gbq_ref[...])
    for blk in range(4):
        sl = slice(LANES * blk, LANES * (blk + 1))
        qb_ref[:, sl] = _rope128(bq[:, sl], cosb, sinb, first_half).astype(BF16)
    bk = _seg_rmsnorm(proj[:, 1024:1152], mbk_ref[...], mbkt_ref[...], ilbk_ref[...], gbk_ref[...])
    bk = _rope128(bk, cosb, sinb, first_half)
    k0, k1 = _dup_halves(bk, lo)
    kb_ref[:, 0:LANES] = k0.astype(BF16)
    kb_ref[:, LANES:2 * LANES] = k1.astype(BF16)
    bv = proj[:, 1152:1280]
    vb_ref[:, 0:LANES] = jnp.where(lo, bv, 1.0).astype(BF16)
    vb_ref[:, LANES:2 * LANES] = jnp.where(lo, pltpu.roll(bv, HEAD, 1), 1.0).astype(BF16)


def _prep_c_rows(x, gmix_ref, w_ref,
                   mq_ref, mqt_ref, ilq_ref, gq_ref,
                   mk_ref, mkt_ref, ilk_ref, gk_ref,
                   qc_ref, kc_ref, vc_ref):
    lo, _ = _lane_masks()
    h = _row_rmsnorm(x, gmix_ref[...]).astype(BF16)
    proj = _dot(h, w_ref[...])
    q = _seg_rmsnorm(proj[:, 0:1024], mq_ref[...], mqt_ref[...], ilq_ref[...], gq_ref[...])
    qc_ref[...] = q.astype(BF16)
    k = _seg_rmsnorm(proj[:, 1024:1280], mk_ref[...], mkt_ref[...], ilk_ref[...], gk_ref[...])
    for blk in range(2):
        sl = slice(LANES * blk, LANES * (blk + 1))
        k0, k1 = _dup_halves(k[:, sl], lo)
        kc_ref[:, 2 * blk * LANES:(2 * blk + 1) * LANES] = k0.astype(BF16)
        kc_ref[:, (2 * blk + 1) * LANES:(2 * blk + 2) * LANES] = k1.astype(BF16)
        v = proj[:, 1280 + LANES * blk:1280 + LANES * (blk + 1)]
        vc_ref[:, 2 * blk * LANES:(2 * blk + 1) * LANES] = jnp.where(lo, v, 1.0).astype(BF16)
        vc_ref[:, (2 * blk + 1) * LANES:(2 * blk + 2) * LANES] = jnp.where(
            lo, pltpu.roll(v, HEAD, 1), 1.0).astype(BF16)


def _softmax_pv(s, v):
    m = jnp.max(s, axis=-1, keepdims=True)
    p = jnp.exp(s - m)
    l = jnp.sum(p, axis=-1, keepdims=True)
    return _dot(p.astype(BF16), v) / l


def _lane_chunk_reduce(op, t):
    out = t[:, 0:LANES]
    for j in range(1, t.shape[1] // LANES):
        out = op(out, t[:, LANES * j:LANES * (j + 1)])
    return out


def _scores_phase(q, k_ref, lanes, s_buf):
    m_part = None
    for c in range(k_ref.shape[1] // KEY_CHUNK):
        ks = slice(KEY_CHUNK * c, KEY_CHUNK * (c + 1))
        s_c = _dot_nt(q, k_ref[0, ks, lanes])
        s_buf[:, ks] = s_c
        mc = _lane_chunk_reduce(jnp.maximum, s_c)
        m_part = mc if m_part is None else jnp.maximum(m_part, mc)
    return jnp.max(m_part, axis=-1, keepdims=True)


def _pv_phase(s_buf, m, v_ref, lanes):
    lo, _ = _lane_masks()
    acc = None
    for c in range(v_ref.shape[1] // KEY_CHUNK):
        ks = slice(KEY_CHUNK * c, KEY_CHUNK * (c + 1))
        p = jnp.exp2(s_buf[:, ks] - m)
        pv = _dot(p.astype(BF16), v_ref[0, ks, lanes])
        acc = pv if acc is None else acc + pv
    return acc / jnp.where(lo, pltpu.roll(acc, HEAD, 1), 1.0)


def _pair_heads(o_even, o_odd):
    lo, _ = _lane_masks()
    return jnp.where(lo, o_even, pltpu.roll(o_odd, HEAD, 1))


def _mla_attn_kernel(q_ref, k_ref, v_ref, o_ref, s0_ref, s1_ref):
    first, second = slice(0, LANES), slice(LANES, 2 * LANES)
    m0 = _scores_phase(q_ref[0, :, first], k_ref, first, s0_ref)
    m1 = _scores_phase(q_ref[0, :, second], k_ref, second, s1_ref)
    o0 = _pv_phase(s0_ref, m0, v_ref, first)
    o1 = _pv_phase(s1_ref, m1, v_ref, second)
    o_ref[0] = _pair_heads(o0, o1).astype(o_ref.dtype)


def _gqa_attn_kernel(q_ref, k_ref, v_ref, o_ref, s0_ref, s1_ref):
    lo, _ = _lane_masks()
    tq = q_ref.shape[1]
    hi = jnp.logical_not(lo)
    ms = []
    for j, s_ref in enumerate((s0_ref, s1_ref)):
        blk = q_ref[0, :, LANES * j:LANES * (j + 1)]
        zero = jnp.zeros_like(blk)
        qs = jnp.concatenate([jnp.where(lo, blk, zero), jnp.where(hi, blk, zero)], axis=0)
        ms.append(_scores_phase(qs, k_ref, slice(0, LANES), s_ref))
    for j, s_ref in enumerate((s0_ref, s1_ref)):
        o = _pv_phase(s_ref, ms[j], v_ref, slice(0, LANES))
        o_ref[0, :, LANES * j:LANES * (j + 1)] = _pair_heads(o[0:tq], o[tq:2 * tq]).astype(o_ref.dtype)


def _win_attn_kernel(sink_ref, q_ref, kp_ref, kc_ref, kn_ref, vp_ref, vc_ref, vn_ref, bias_ref, o_ref,
                     s_scr, p_scr, m_scr, *, seq_len):
    lo, _ = _lane_masks()
    tq = Q_BLOCK
    nq = q_ref.shape[1] // tq
    i = pl.program_id(1)
    kj = lax.broadcasted_iota(jnp.int32, (1, 3 * tq), 1)
    units = [(g, u) for g in range(4) for u in range(nq)]
    for idx, (g, u) in enumerate(units):
        sl = slice(LANES * g, LANES * (g + 1))
        kcat = jnp.concatenate([kp_ref[0, :, sl], kc_ref[0, :, sl], kn_ref[0, :, sl]], axis=0)
        parts = []
        for a in range(4):
            blk = q_ref[0, tq * u:tq * (u + 1), LANES * (2 * g + a // 2):LANES * (2 * g + a // 2 + 1)]
            keep = lo if a % 2 == 0 else jnp.logical_not(lo)
            parts.append(jnp.where(keep, blk, jnp.zeros_like(blk)))
        s_scr[idx] = _dot_nt(jnp.concatenate(parts, axis=0), kcat[tq * u:tq * (u + 3)])
    for idx, (g, u) in enumerate(units):
        key_pos = (i * nq + u - 1) * tq + kj
        valid = jnp.logical_and(key_pos >= 0, key_pos < seq_len)
        for rt in range(4 * tq // WIN_ROW_TILE):
            a, r0 = divmod(rt * WIN_ROW_TILE, tq)
            rows = slice(rt * WIN_ROW_TILE, (rt + 1) * WIN_ROW_TILE)
            s = s_scr[idx, rows, :] + bias_ref[4 * g + a, r0:r0 + WIN_ROW_TILE, :]
            s = jnp.where(valid, s, -jnp.inf)
            m = jnp.maximum(jnp.max(_lane_chunk_reduce(jnp.maximum, s), axis=-1, keepdims=True),
                            sink_ref[4 * g + a] * LOG2E)
            p_scr[idx, rows, :] = jnp.exp2(s - m).astype(BF16)
            m_scr[idx, rows, :] = jnp.broadcast_to(m, (WIN_ROW_TILE, LANES))
    for idx, (g, u) in enumerate(units):
        sl = slice(LANES * g, LANES * (g + 1))
        vcat = jnp.concatenate([vp_ref[0, :, sl], vc_ref[0, :, sl], vn_ref[0, :, sl]], axis=0)
        acc = _dot(p_scr[idx], vcat[tq * u:tq * (u + 3)])
        outs = []
        for a in range(4):
            rows = slice(tq * a, tq * (a + 1))
            sink_term = jnp.exp2(sink_ref[4 * g + a] * LOG2E - m_scr[idx, rows, :])
            den = jnp.where(lo, pltpu.roll(acc[rows], HEAD, 1) + sink_term, 1.0)
            outs.append(acc[rows] / den)
        for j in range(2):
            blk = 2 * g + j
            o_ref[0, tq * u:tq * (u + 1), LANES * blk:LANES * (blk + 1)] = _pair_heads(
                outs[2 * j], outs[2 * j + 1]).astype(o_ref.dtype)


def _rel_bias_kernel(bucket_ref, band_ref, relt_ref, o_ref):
    bucket = bucket_ref[...]
    acc = jnp.zeros(o_ref.shape, F32)
    for r in range(REL_BUCKETS):
        acc = acc + jnp.where(bucket == r, relt_ref[:, r:r + 1], 0.0)
    o_ref[...] = jnp.where(band_ref[...] > 0, acc * LOG2E, -jnp.inf)


def _to_tiles(ref, val, first_token=0):
    for c in range(TOKEN_TILE_ROWS):
        ref[pl.ds(first_token * TOKEN_TILE_ROWS + c, val.shape[0], stride=TOKEN_TILE_ROWS), :] = (
            val[:, LANES * c:LANES * (c + 1)])


def _from_tiles(ref):
    rows = ref.shape[0] // TOKEN_TILE_ROWS
    return jnp.concatenate([ref[pl.ds(c, rows, stride=TOKEN_TILE_ROWS), :] for c in range(TOKEN_TILE_ROWS)],
                           axis=1)


def _route_bucket(xn, wh, wl, rb):
    xh, xl = _split_bf16(xn)
    logit = _dot_nt(wh, xh) + _dot_nt(wh, xl) + _dot_nt(wl, xh) + rb
    g = [logit[r:r + 1, :] for r in range(N_GROUPS)]
    gmax = jnp.maximum(jnp.maximum(g[0], g[1]), jnp.maximum(g[2], g[3]))
    gidx = jnp.where(g[0] == gmax, 0, jnp.where(g[1] == gmax, 1, jnp.where(g[2] == gmax, 2, 3)))
    e = []
    for j in range(EXPERTS_PER_GROUP):
        rows = [logit[N_GROUPS + EXPERTS_PER_GROUP * gg + j:N_GROUPS + EXPERTS_PER_GROUP * gg + j + 1, :]
                for gg in range(N_GROUPS)]
        e.append(jnp.where(gidx == 0, rows[0], jnp.where(gidx == 1, rows[1],
                                                         jnp.where(gidx == 2, rows[2], rows[3]))))
    emax = jnp.maximum(jnp.maximum(e[0], e[1]), jnp.maximum(e[2], e[3]))
    ex = [jnp.exp(ej - emax) for ej in e]
    esum = ex[0] + ex[1] + ex[2] + ex[3]
    pr = [exj / esum for exj in ex]
    p1 = jnp.maximum(jnp.maximum(pr[0], pr[1]), jnp.maximum(pr[2], pr[3]))
    i1 = jnp.where(pr[0] == p1, 0, jnp.where(pr[1] == p1, 1, jnp.where(pr[2] == p1, 2, 3)))
    rest = [jnp.where(i1 == j, -1.0, pr[j]) for j in range(EXPERTS_PER_GROUP)]
    p2 = jnp.maximum(jnp.maximum(rest[0], rest[1]), jnp.maximum(rest[2], rest[3]))
    i2 = jnp.where(rest[0] == p2, 0, jnp.where(rest[1] == p2, 1, jnp.where(rest[2] == p2, 2, 3)))
    lo_e = jnp.minimum(i1, i2)
    hi_e = jnp.maximum(i1, i2)
    pair = jnp.where(lo_e == 0, hi_e - 1, jnp.where(lo_e == 1, hi_e + 1, 5))
    return gidx * PAIRS_PER_GROUP + pair


def _post_kernel(*refs, n_parts):
    parts = refs[:n_parts]
    ws = refs[n_parts:2 * n_parts]
    (x_ref, gffn_ref, wrh_ref, wrl_ref, rb_ref, tri_ref,
     x1t_ref, bucket_ref, rank_ref, counts_ref, carry_ref) = refs[2 * n_parts:]

    @pl.when(pl.program_id(0) == 0)
    def _():
        carry_ref[...] = jnp.zeros_like(carry_ref)

    sub = tri_ref.shape[0]
    buckets = []
    for h in range(x_ref.shape[0] // sub):
        rows = slice(sub * h, sub * (h + 1))
        acc = x_ref[rows, :]
        for p_ref, w_ref in zip(parts, ws):
            acc = acc + _dot(p_ref[rows, :], w_ref[...])
        _to_tiles(x1t_ref, acc, sub * h)
        bucket = _route_bucket(_row_rmsnorm(acc, gffn_ref[...]), wrh_ref[...], wrl_ref[...], rb_ref[...])
        bucket_ref[:, rows] = bucket
        buckets.append(bucket)
    for h, bucket in enumerate(buckets):
        rows = slice(sub * h, sub * (h + 1))
        onehot = (lax.broadcasted_iota(jnp.int32, (ROUTER_ROWS, sub), 0) == bucket).astype(F32)
        before = _dot(onehot.astype(BF16), tri_ref[...]) + carry_ref[:, 0:1]
        rank_ref[:, rows] = jnp.sum(onehot * before, axis=0, keepdims=True).astype(jnp.int32)
        carry_ref[...] = carry_ref[...] + jnp.sum(onehot, axis=1, keepdims=True)
    counts_ref[...] = carry_ref[...]


def _start_row_gather(idx_of, table_hbm, dst, sem):
    def body(blk, carry):
        for j in range(GATHER_UNROLL):
            r = blk * GATHER_UNROLL + j
            src_row = pl.multiple_of(idx_of(r) * TOKEN_TILE_ROWS, TOKEN_TILE_ROWS)
            dst_row = pl.multiple_of(r * TOKEN_TILE_ROWS, TOKEN_TILE_ROWS)
            pltpu.make_async_copy(table_hbm.at[pl.ds(src_row, TOKEN_TILE_ROWS)],
                                  dst.at[pl.ds(dst_row, TOKEN_TILE_ROWS)], sem).start()
        return carry
    lax.fori_loop(0, dst.shape[0] // (TOKEN_TILE_ROWS * GATHER_UNROLL), body, 0)


def _wait_row_gather(table_hbm, dst, sem):
    pltpu.make_async_copy(table_hbm.at[pl.ds(0, dst.shape[0])], dst, sem).wait()


def _tile_gates(logit, ea, eb):
    lane = lax.broadcasted_iota(jnp.int32, (1, LANES), 1)
    pick = lambda idx: jnp.sum(jnp.where(lane == idx, logit, 0.0), axis=-1, keepdims=True)
    glog = jnp.where(lane < N_GROUPS, logit, -jnp.inf)
    gmax = jnp.max(glog, axis=-1, keepdims=True)
    gsum = jnp.sum(jnp.exp(glog - gmax), axis=-1, keepdims=True)
    g_p = jnp.exp(pick(ea // EXPERTS_PER_GROUP) - gmax) / gsum
    la, lb = pick(N_GROUPS + ea), pick(N_GROUPS + eb)
    top = jnp.maximum(la, lb)
    pa, pb = jnp.exp(la - top), jnp.exp(lb - top)
    return g_p * pa / (pa + pb), g_p * pb / (pa + pb)


def _moe_sparse_kernel(ea_ref, eb_ref, nt_ref, first_ref, order_ref, gffn_ref, wr_ref, rb_ref, x1t_hbm,
                       wga_ref, wua_ref, wda_ref, wgb_ref, wub_ref, wdb_ref, o_ref, buf, sem):
    t = pl.program_id(0)
    nt = nt_ref[0]
    slot = t % 2

    def tokens_of(tile):
        base = first_ref[tile]
        return lambda r: order_ref[base + r]

    @pl.when(t == 0)
    def _():
        _start_row_gather(tokens_of(0), x1t_hbm, buf.at[0], sem.at[0])

    @pl.when(t + 1 < nt)
    def _():
        _start_row_gather(tokens_of(t + 1), x1t_hbm, buf.at[1 - slot], sem.at[1 - slot])

    @pl.when(t < nt)
    def _():
        _wait_row_gather(x1t_hbm, buf.at[slot], sem.at[slot])
        x = _from_tiles(buf.at[slot])
        xn = _row_rmsnorm(x, gffn_ref[...])
        gate_a, gate_b = _tile_gates(_dot(xn, wr_ref[...]) + rb_ref[...], ea_ref[t], eb_ref[t])
        hid_a = jax.nn.silu(_dot(xn, wga_ref[0, 0])) * _dot(xn, wua_ref[0, 0])
        hid_b = jax.nn.silu(_dot(xn, wgb_ref[0, 0])) * _dot(xn, wub_ref[0, 0])
        out = x + gate_a * _dot(hid_a, wda_ref[0, 0]) + gate_b * _dot(hid_b, wdb_ref[0, 0])
        _to_tiles(o_ref, out)

    @pl.when(t >= nt)
    def _():
        o_ref[...] = jnp.zeros_like(o_ref)


def _unpermute_kernel(pos_cur_ref, pos_nxt_ref, sorted_hbm, o_ref, buf, sem):
    t = pl.program_id(0)
    slot = t % 2

    @pl.when(t == 0)
    def _():
        _start_row_gather(lambda r: pos_cur_ref[0, 0, r], sorted_hbm, buf.at[0], sem.at[0])

    @pl.when(t + 1 < pl.num_programs(0))
    def _():
        _start_row_gather(lambda r: pos_nxt_ref[0, 0, r], sorted_hbm, buf.at[1 - slot], sem.at[1 - slot])

    _wait_row_gather(sorted_hbm, buf.at[slot], sem.at[slot])
    o_ref[...] = _from_tiles(buf.at[slot])


def _seg_mats(width, segments):
    m = np.zeros((width, LANES), np.float32)
    invlen = np.ones((1, LANES), np.float32)
    for c, (start, length) in enumerate(segments):
        m[start:start + length, c] = 1.0
        invlen[0, c] = 1.0 / length
    return jnp.asarray(m, BF16), jnp.asarray(np.concatenate([m.T, m.T], axis=0), BF16), jnp.asarray(invlen)


def _rope_tables(pos_list, seq_len):
    inv = np.float32(ROPE_THETA) ** (-np.arange(0, 2 * ROPE_HALF, 2, dtype=np.float32) / np.float32(2 * ROPE_HALF))
    cos_cols, sin_cols = [], []
    for pos in pos_list:
        if pos is None:
            cos_cols.append(np.ones((seq_len, 2 * ROPE_HALF), np.float32))
            sin_cols.append(np.zeros((seq_len, 2 * ROPE_HALF), np.float32))
        else:
            ang = pos.astype(np.float32)[:, None] * inv[None, :]
            c, s = np.cos(ang), np.sin(ang)
            cos_cols.append(np.concatenate([c, c], axis=1))
            sin_cols.append(np.concatenate([-s, s], axis=1))
    return jnp.asarray(np.concatenate(cos_cols, axis=1)), jnp.asarray(np.concatenate(sin_cols, axis=1))


def _t5_bucket_np(rel):
    nb = REL_BUCKETS // 2
    max_exact = nb // 2
    ret = np.where(rel > 0, nb, 0)
    n = np.abs(rel)
    nf = np.maximum(n, 1).astype(np.float32)
    large = max_exact + (np.log(nf / np.float32(max_exact)) / np.float32(math.log(REL_MAX_DIST / max_exact))
                         * np.float32(nb - max_exact)).astype(np.int32)
    large = np.minimum(large, nb - 1)
    return ret + np.where(n < max_exact, n, large)


def _full(shape):
    nd = len(shape)
    return pl.BlockSpec(shape, lambda *_: (0,) * nd)


def _params(*sem):
    return pltpu.CompilerParams(dimension_semantics=sem, vmem_limit_bytes=VMEM_LIMIT)


def _prep_ab(x2, seq_len, gmix, w_in, qag, wq, kvag, wkv, qn_g, kn_g, qr_g, kr_g, bq_g, bk_g, tm=512):
    n = x2.shape[0]
    scale_a = (HEAD + 2 * ROPE_HALF) ** -0.5
    scale_b = HEAD ** -0.5
    zeros = lambda r, c: jnp.zeros((r, c), F32)
    d = w_in.shape[0]
    w0 = jnp.concatenate([w_in[:, 0:384], zeros(d, HEAD), w_in[:, 384:416], zeros(d, 32), w_in[:, 416:1184]],
                         axis=1).astype(BF16)
    wq_p = jnp.concatenate([wq.reshape(-1, 8, 96), jnp.zeros((wq.shape[0], 8, 32), F32)], axis=2)
    wq_p = wq_p.reshape(-1, 1024).astype(BF16)
    wkv_r = wkv.reshape(-1, 8, 128)
    wk_p = jnp.concatenate([wkv_r[:, :, :HEAD], jnp.zeros_like(wkv_r[:, :, :HEAD])], axis=2).reshape(-1, 1024)
    wv_p = jnp.concatenate([wkv_r[:, :, HEAD:], jnp.zeros_like(wkv_r[:, :, HEAD:])], axis=2).reshape(-1, 1024)
    wkv_p = jnp.concatenate([wk_p, wv_p], axis=1).astype(BF16)

    seg_q = [(LANES * h, HEAD) for h in range(8)] + [(LANES * h + HEAD, 32) for h in range(8)]
    gq = jnp.tile(jnp.concatenate([qn_g, qr_g, jnp.zeros((32,), F32)]), 8)[None, :] * (scale_a * LOG2E)
    seg_k = [(LANES * h, HEAD) for h in range(8)]
    gk = jnp.tile(jnp.concatenate([kn_g, jnp.zeros((HEAD,), F32)]), 8)[None, :]
    seg_r = [(HEAD, 32)]
    gr = jnp.concatenate([jnp.zeros((HEAD,), F32), kr_g, jnp.zeros((32,), F32)])[None, :]
    seg_bq = [(HEAD * h, HEAD) for h in range(8)]
    gbq = jnp.tile(bq_g, 8)[None, :] * (scale_b * LOG2E)
    seg_bk = [(HEAD * h, HEAD) for h in range(2)]
    gbk = jnp.tile(bk_g, 2)[None, :]

    pos = np.arange(seq_len)
    cosa, sina = _rope_tables([None, None, pos, None], seq_len)
    row, col = pos // GRID_W, pos % GRID_W
    cosb, sinb = _rope_tables([row, col, row, col], seq_len)

    consts = [gmix[None, :], w0, qag[None, :], wq_p, kvag[None, :], wkv_p,
              *_seg_mats(1024, seg_q), gq, *_seg_mats(1024, seg_k), gk, *_seg_mats(LANES, seg_r), gr,
              *_seg_mats(512, seg_bq), gbq, *_seg_mats(LANES, seg_bk), gbk]
    nsb = seq_len // tm
    rope_spec = pl.BlockSpec((tm, LANES), lambda i: (i % nsb, 0))
    row_spec = lambda w: pl.BlockSpec((tm, w), lambda i: (i, 0))
    out_widths = (1024, 1024, 1024, 512, 256, 256)
    return pl.pallas_call(
        _prep_ab_kernel,
        grid=(n // tm,),
        in_specs=[row_spec(x2.shape[1])] + [_full(c.shape) for c in consts] + [rope_spec] * 4,
        out_specs=[row_spec(w) for w in out_widths],
        out_shape=[jax.ShapeDtypeStruct((n, w), BF16) for w in out_widths],
        compiler_params=_params("parallel"),
        name="prep_ab",
    )(x2, *consts, cosa, sina, cosb, sinb)


def _prep_c(sorted_rows, pos, gmix, w_in, q_g, k_g, tm=512):
    n = pos.shape[0]
    d = TOKEN_TILE_ROWS * LANES
    seg_q = [(HEAD * h, HEAD) for h in range(16)]
    gq = jnp.tile(q_g, 16)[None, :] * (HEAD ** -0.5 * LOG2E)
    seg_k = [(HEAD * h, HEAD) for h in range(4)]
    gk = jnp.tile(k_g, 4)[None, :]
    consts = [gmix[None, :], w_in.astype(BF16), *_seg_mats(1024, seg_q), gq, *_seg_mats(256, seg_k), gk]
    row_spec = lambda w: pl.BlockSpec((tm, w), lambda i: (i, 0))
    out_widths = (1024, 512, 512)
    pos2 = jnp.concatenate([pos, jnp.zeros((tm,), jnp.int32)]).reshape(n // tm + 1, 1, tm)
    idx_spec = lambda off: pl.BlockSpec((1, 1, tm), lambda i: (i + off, 0, 0), memory_space=pltpu.SMEM)
    return pl.pallas_call(
        _prep_c_kernel,
        grid=(n // tm,),
        in_specs=[idx_spec(0), idx_spec(1)] + [_full(c.shape) for c in consts] + [pl.BlockSpec(memory_space=pl.ANY)],
        out_specs=[row_spec(d)] + [row_spec(w) for w in out_widths],
        out_shape=[jax.ShapeDtypeStruct((n, d), F32)] + [jax.ShapeDtypeStruct((n, w), BF16) for w in out_widths],
        scratch_shapes=[pltpu.VMEM((2, tm * TOKEN_TILE_ROWS, LANES), F32), pltpu.SemaphoreType.DMA((2,))],
        compiler_params=_params("arbitrary"),
        name="prep_c",
    )(pos2, pos2, *consts, sorted_rows)


def _mla_attn(qa, ka, va, tq=512):
    b, s, _ = qa.shape
    return pl.pallas_call(
        _mla_attn_kernel,
        grid=(b, 4, s // tq),
        in_specs=[pl.BlockSpec((1, tq, 2 * LANES), lambda bi, hp, qi: (bi, qi, hp)),
                  pl.BlockSpec((1, s, 2 * LANES), lambda bi, hp, qi: (bi, 0, hp)),
                  pl.BlockSpec((1, s, 2 * LANES), lambda bi, hp, qi: (bi, 0, hp))],
        out_specs=pl.BlockSpec((1, tq, LANES), lambda bi, hp, qi: (bi, qi, hp)),
        out_shape=jax.ShapeDtypeStruct((b, s, 512), BF16),
        scratch_shapes=[pltpu.VMEM((tq, s), F32), pltpu.VMEM((tq, s), F32)],
        compiler_params=_params("parallel", "parallel", "parallel"),
        name="mla_attn",
    )(qa, ka, va)


def _gqa_attn(qb, kb, vb, tq=256):
    b, s, _ = qb.shape
    return pl.pallas_call(
        _gqa_attn_kernel,
        grid=(b, 2, s // tq),
        in_specs=[pl.BlockSpec((1, tq, 2 * LANES), lambda bi, g, qi: (bi, qi, g)),
                  pl.BlockSpec((1, s, LANES), lambda bi, g, qi: (bi, 0, g)),
                  pl.BlockSpec((1, s, LANES), lambda bi, g, qi: (bi, 0, g))],
        out_specs=pl.BlockSpec((1, tq, 2 * LANES), lambda bi, g, qi: (bi, qi, g)),
        out_shape=jax.ShapeDtypeStruct((b, s, 512), BF16),
        scratch_shapes=[pltpu.VMEM((2 * tq, s), F32), pltpu.VMEM((2 * tq, s), F32)],
        compiler_params=_params("parallel", "parallel", "parallel"),
        name="gqa_attn",
    )(qb, kb, vb)


def _rel_bias_table(rel_bias):
    span = Q_BLOCK + 2 * WINDOW
    rel = np.arange(span)[None, :] - WINDOW - np.arange(Q_BLOCK)[:, None]
    bucket = _t5_bucket_np(rel).astype(np.int32).reshape(1, -1)
    band = (np.abs(rel) <= WINDOW).astype(np.int32).reshape(1, -1)
    heads = rel_bias.shape[1]
    cols = bucket.shape[1]
    chunk = cols // 8
    table = pl.pallas_call(
        _rel_bias_kernel,
        grid=(8,),
        in_specs=[pl.BlockSpec((1, chunk), lambda i: (0, i)), pl.BlockSpec((1, chunk), lambda i: (0, i)),
                  _full((heads, REL_BUCKETS))],
        out_specs=pl.BlockSpec((heads, chunk), lambda i: (0, i)),
        out_shape=jax.ShapeDtypeStruct((heads, cols), F32),
        compiler_params=_params("parallel"),
        name="rel_bias",
    )(jnp.asarray(bucket), jnp.asarray(band), rel_bias.T)
    return table.reshape(heads, Q_BLOCK, span)


def _win_attn(qc, kc, vc, sink, bias, nq=4):
    b, s, _ = qc.shape
    tq = Q_BLOCK
    nb = s // tq
    prev = lambda bi, i: (bi, jnp.maximum(i * nq - 1, 0), 0)
    cur = lambda bi, i: (bi, i, 0)
    nxt = lambda bi, i: (bi, jnp.minimum((i + 1) * nq, nb - 1), 0)
    edge_spec = lambda im: pl.BlockSpec((1, tq, 4 * LANES), im)
    cur_spec = pl.BlockSpec((1, nq * tq, 4 * LANES), cur)
    return pl.pallas_call(
        functools.partial(_win_attn_kernel, seq_len=s),
        grid=(b, nb // nq),
        in_specs=[pl.BlockSpec(memory_space=pltpu.SMEM),
                  pl.BlockSpec((1, nq * tq, 8 * LANES), cur),
                  edge_spec(prev), cur_spec, edge_spec(nxt),
                  edge_spec(prev), cur_spec, edge_spec(nxt),
                  _full(bias.shape)],
        out_specs=pl.BlockSpec((1, nq * tq, 8 * LANES), cur),
        scratch_shapes=[pltpu.VMEM((4 * nq, 4 * tq, 3 * tq), F32), pltpu.VMEM((4 * nq, 4 * tq, 3 * tq), BF16),
                        pltpu.VMEM((4 * nq, 4 * tq, LANES), F32)],
        out_shape=jax.ShapeDtypeStruct((b, s, 1024), BF16),
        compiler_params=_params("parallel", "parallel"),
        name="win_attn",
    )(sink, qc, kc, kc, kc, vc, vc, vc, bias)


def _post(parts, ws, x2, gffn, w_group, b_group, w_router, b_router, tm=1024, sub=256):
    n, d = x2.shape
    wr = jnp.concatenate([w_group.T, w_router.T, jnp.zeros((ROUTER_ROWS - N_GROUPS - N_EXPERTS, d), F32)], axis=0)
    wrh = wr.astype(BF16)
    wrl = (wr - wrh.astype(F32)).astype(BF16)
    rb = jnp.concatenate([b_group, b_router, jnp.zeros((ROUTER_ROWS - N_GROUPS - N_EXPERTS,), F32)])[:, None]
    row_spec = lambda w: pl.BlockSpec((tm, w), lambda i: (i, 0))
    lane_spec = lambda r: pl.BlockSpec((r, tm), lambda i: (0, i))
    ws = [w.astype(BF16) for w in ws]
    tri = jnp.asarray(np.triu(np.ones((sub, sub), np.float32), 1), BF16)
    return pl.pallas_call(
        functools.partial(_post_kernel, n_parts=len(parts)),
        grid=(n // tm,),
        in_specs=[row_spec(p.shape[1]) for p in parts] + [_full(w.shape) for w in ws]
                 + [row_spec(d), _full((1, d)), _full(wrh.shape), _full(wrl.shape), _full(rb.shape),
                    _full(tri.shape)],
        out_specs=[pl.BlockSpec((tm * TOKEN_TILE_ROWS, LANES), lambda i: (i, 0)),
                   lane_spec(1), lane_spec(1), _full((ROUTER_ROWS, LANES))],
        out_shape=[jax.ShapeDtypeStruct((n * TOKEN_TILE_ROWS, LANES), F32),
                   jax.ShapeDtypeStruct((1, n), jnp.int32), jax.ShapeDtypeStruct((1, n), jnp.int32),
                   jax.ShapeDtypeStruct((ROUTER_ROWS, LANES), F32)],
        scratch_shapes=[pltpu.VMEM((ROUTER_ROWS, LANES), F32)],
        compiler_params=_params("arbitrary"),
        name="post",
    )(*parts, *ws, x2, gffn[None, :], wrh, wrl, rb, tri)


def _route(bucket, rank, counts, tmm):
    n = bucket.shape[1]
    n_tiles_max = n // tmm + N_BUCKETS
    bucket, rank = bucket[0], rank[0]
    cnt = counts[:N_BUCKETS, 0].astype(jnp.int32)
    padded = (cnt + tmm - 1) // tmm * tmm
    ends = jnp.cumsum(padded)
    starts = ends - padded
    pos = starts[bucket] + rank
    n_tiles = ends[-1] // tmm
    tile = jnp.minimum(jnp.arange(n_tiles_max, dtype=jnp.int32), n_tiles - 1)
    tile_bucket = jnp.sum((tile * tmm)[:, None] >= ends[None, :], axis=1).astype(jnp.int32)
    grp, pair = tile_bucket // PAIRS_PER_GROUP, tile_bucket % PAIRS_PER_GROUP
    tile_ea = grp * EXPERTS_PER_GROUP + jnp.asarray(PAIR_LO, jnp.int32)[pair]
    tile_eb = grp * EXPERTS_PER_GROUP + jnp.asarray(PAIR_HI, jnp.int32)[pair]
    token = jnp.arange(n, dtype=jnp.int32)
    _, by_bucket = lax.sort((bucket, token), num_keys=1, is_stable=True)
    order = jnp.concatenate([by_bucket, token[:tmm]])
    first = jnp.cumsum(cnt) - cnt
    tile_first = first[tile_bucket] + tile * tmm - starts[tile_bucket]
    return pos, order, tile_first, tile_ea, tile_eb, n_tiles.reshape(1)


def _moe_sparse(x1t, gffn, w_group, b_group, w_router, b_router, order, tile_first, tile_ea, tile_eb, n_tiles,
                w_gate, w_up, w_down, layer, tmm):
    n_tiles_max = tile_ea.shape[0]
    _, ne, d, de = w_gate.shape
    up_spec = lambda which: pl.BlockSpec((1, 1, d, de), lambda t, ea, eb, *_: (layer, (ea, eb)[which][t], 0, 0))
    down_spec = lambda which: pl.BlockSpec((1, 1, de, d), lambda t, ea, eb, *_: (layer, (ea, eb)[which][t], 0, 0))
    wg, wu, wd = w_gate, w_up, w_down
    pad = LANES - N_GROUPS - N_EXPERTS
    wr = jnp.concatenate([w_group, w_router, jnp.zeros((d, pad), F32)], axis=1)
    rb = jnp.concatenate([b_group, b_router, jnp.zeros((pad,), F32)])[None, :]
    return pl.pallas_call(
        _moe_sparse_kernel,
        grid_spec=pltpu.PrefetchScalarGridSpec(
            num_scalar_prefetch=5,
            grid=(n_tiles_max,),
            in_specs=[pl.BlockSpec((1, d), lambda t, *_: (0, 0)),
                      pl.BlockSpec((d, LANES), lambda t, *_: (0, 0)),
                      pl.BlockSpec((1, LANES), lambda t, *_: (0, 0)),
                      pl.BlockSpec(memory_space=pl.ANY),
                      up_spec(0), up_spec(0), down_spec(0), up_spec(1), up_spec(1), down_spec(1)],
            out_specs=pl.BlockSpec((tmm * TOKEN_TILE_ROWS, LANES), lambda t, *_: (t, 0)),
            scratch_shapes=[pltpu.VMEM((2, tmm * TOKEN_TILE_ROWS, LANES), F32), pltpu.SemaphoreType.DMA((2,))]),
        out_shape=jax.ShapeDtypeStruct((n_tiles_max * tmm * TOKEN_TILE_ROWS, LANES), F32),
        compiler_params=_params("arbitrary"),
        name="moe_sparse",
    )(tile_ea, tile_eb, n_tiles, tile_first, order, gffn[None, :], wr, rb, x1t, wg, wu, wd, wg, wu, wd)


def _unpermute(sorted_rows, pos, tm=256):
    n = pos.shape[0]
    d = TOKEN_TILE_ROWS * LANES
    pos2 = jnp.concatenate([pos, jnp.zeros((tm,), jnp.int32)]).reshape(n // tm + 1, 1, tm)
    idx_spec = lambda off: pl.BlockSpec((1, 1, tm), lambda t: (t + off, 0, 0), memory_space=pltpu.SMEM)
    return pl.pallas_call(
        _unpermute_kernel,
        grid=(n // tm,),
        in_specs=[idx_spec(0), idx_spec(1), pl.BlockSpec(memory_space=pl.ANY)],
        out_specs=pl.BlockSpec((tm, d), lambda t: (t, 0)),
        out_shape=jax.ShapeDtypeStruct((n, d), F32),
        scratch_shapes=[pltpu.VMEM((2, tm * TOKEN_TILE_ROWS, LANES), F32), pltpu.SemaphoreType.DMA((2,))],
        compiler_params=_params("arbitrary"),
        name="unpermute",
    )(pos2, pos2, sorted_rows)


def kernel(x, mix_norm, ffn_norm, w_in_ab, mla_q_a_norm, mla_w_q_up, mla_kv_a_norm, mla_w_kv_up, mla_qn_gain, mla_kn_gain, mla_qr_gain, mla_kr_gain, gqa_q_gain, gqa_k_gain, w_out_ab, w_in_c, win_q_gain, win_k_gain, win_sink, w_out_c, rel_bias, moe_w_group, moe_b_group, moe_w_router, moe_b_router, moe_w_gate, moe_w_up, moe_w_down):
    b, s, d = x.shape
    n = b * s
    depth = mix_norm.shape[0]
    x2 = x.reshape(n, d)
    bias = None
    moe_out = None
    for layer in range(depth):
        i = layer // 2
        if layer % 2 == 0:
            if moe_out is not None:
                x2 = _unpermute(*moe_out)
            qa, ka, va, qb, kb, vb = _prep_ab(
                x2, s, mix_norm[layer], w_in_ab[i], mla_q_a_norm[i], mla_w_q_up[i], mla_kv_a_norm[i],
                mla_w_kv_up[i], mla_qn_gain[i], mla_kn_gain[i], mla_qr_gain[i], mla_kr_gain[i],
                gqa_q_gain[i], gqa_k_gain[i])
            r3 = lambda t: t.reshape(b, s, t.shape[1])
            out_a = _mla_attn(r3(qa), r3(ka), r3(va)).reshape(n, 512)
            out_b = _gqa_attn(r3(qb), r3(kb), r3(vb)).reshape(n, 512)
            parts, ws = [out_a, out_b], [w_out_ab[i][:512], w_out_ab[i][512:]]
        else:
            if bias is None:
                bias = _rel_bias_table(rel_bias)
            x2, qc, kc, vc = _prep_c(*moe_out, mix_norm[layer], w_in_c[i], win_q_gain[i], win_k_gain[i])
            r3 = lambda t: t.reshape(b, s, t.shape[1])
            out_c = _win_attn(r3(qc), r3(kc), r3(vc), win_sink[i], bias).reshape(n, 1024)
            parts, ws = [out_c], [w_out_c[i]]
        router = (moe_w_group[layer], moe_b_group[layer], moe_w_router[layer], moe_b_router[layer])
        x1t, bucket, rank, counts = _post(parts, ws, x2, ffn_norm[layer], *router)
        pos, order, tile_first, tile_ea, tile_eb, n_tiles = _route(bucket, rank, counts, MOE_TILE)
        y_sorted = _moe_sparse(x1t, ffn_norm[layer], *router, order, tile_first, tile_ea, tile_eb, n_tiles,
                               moe_w_gate, moe_w_up, moe_w_down, layer, MOE_TILE)
        moe_out = (y_sorted, pos)
    return _unpermute(*moe_out).reshape(b, s, d)
```

```python
import functools
import math

import numpy as np
import jax
import jax.numpy as jnp
from jax import lax
from jax.experimental import pallas as pl
from jax.experimental.pallas import tpu as pltpu

F32 = jnp.float32
BF16 = jnp.bfloat16

EPS = 1e-6
ROPE_THETA = 10000.0
LANES = 128
HEAD = 64
ROPE_HALF = 16
GRID_W = 64
WINDOW = 128
Q_BLOCK = 128
REL_BUCKETS = 32
REL_MAX_DIST = 128
N_GROUPS = 4
EXPERTS_PER_GROUP = 4
N_EXPERTS = 16
PAIRS_PER_GROUP = 6
N_BUCKETS = N_GROUPS * PAIRS_PER_GROUP
PAIR_LO = (0, 0, 0, 1, 1, 2)
PAIR_HI = (1, 2, 3, 2, 3, 3)
TOKEN_TILE_ROWS = 8
KEY_CHUNK = 512
LOG2E = math.log2(math.e)
PREP_SUB = 256
WIN_ROW_TILE = 32
GATHER_UNROLL = 8
MOE_TILE = 256
ROUTER_ROWS = 32
VMEM_LIMIT = 56 * 1024 * 1024

_NT = (((1,), (1,)), ((), ()))


def _dot(a, b):
    return jnp.dot(a, b, preferred_element_type=F32)


def _dot_nt(a, b):
    return lax.dot_general(a, b, _NT, preferred_element_type=F32)


def _split_bf16(a):
    hi = a.astype(BF16)
    lo = (a - hi.astype(F32)).astype(BF16)
    return hi, lo


def _row_rmsnorm(t, gain):
    return t * lax.rsqrt(jnp.mean(t * t, axis=-1, keepdims=True) + EPS) * gain


def _seg_rmsnorm(t, mseg, msegt, invlen, gain):
    sums = _dot((t * t).astype(BF16), mseg)
    inv = lax.rsqrt(sums * invlen + EPS)
    ihi, ilo = _split_bf16(inv)
    scale = _dot(jnp.concatenate([ihi, ilo], axis=1), msegt)
    return t * scale * gain


def _rope128(t, cos, sin_signed, first_half):
    up = pltpu.roll(t, LANES - ROPE_HALF, 1)
    dn = pltpu.roll(t, ROPE_HALF, 1)
    return t * cos + jnp.where(first_half, up, dn) * sin_signed


def _dup_halves(blk, lo):
    sw = pltpu.roll(blk, HEAD, 1)
    return jnp.where(lo, blk, sw), jnp.where(lo, sw, blk)


def _lane_masks():
    lane = lax.broadcasted_iota(jnp.int32, (1, LANES), 1)
    return lane < HEAD, (lane % (2 * ROPE_HALF)) < ROPE_HALF


def _by_sub_tiles(rows_fn, refs, n_const):
    row_refs = (refs[0],) + tuple(refs[1 + n_const:])
    for h in range(refs[0].shape[0] // PREP_SUB):
        view = [r.at[pl.ds(PREP_SUB * h, PREP_SUB)] for r in row_refs]
        rows_fn(view[0][...], *refs[1:1 + n_const], *view[1:])


SEG_NORM_CONSTS = 4


def _prep_ab_kernel(*refs):
    _by_sub_tiles(_prep_ab_rows, refs, 6 + 5 * SEG_NORM_CONSTS)


def _prep_c_kernel(pos_cur_ref, pos_nxt_ref, *refs):
    n_const = 2 + 2 * SEG_NORM_CONSTS
    consts = refs[:n_const]
    sorted_hbm, x_ref = refs[n_const:n_const + 2]
    outs = refs[n_const + 2:-2]
    buf, sem = refs[-2:]
    t = pl.program_id(0)
    slot = t % 2

    @pl.when(t == 0)
    def _():
        _start_row_gather(lambda r: pos_cur_ref[0, 0, r], sorted_hbm, buf.at[0], sem.at[0])

    @pl.when(t + 1 < pl.num_programs(0))
    def _():
        _start_row_gather(lambda r: pos_nxt_ref[0, 0, r], sorted_hbm, buf.at[1 - slot], sem.at[1 - slot])

    _wait_row_gather(sorted_hbm, buf.at[slot], sem.at[slot])
    for h in range(x_ref.shape[0] // PREP_SUB):
        rows = pl.ds(PREP_SUB * h, PREP_SUB)
        x = _from_tiles(buf.at[slot, pl.ds(PREP_SUB * TOKEN_TILE_ROWS * h, PREP_SUB * TOKEN_TILE_ROWS)])
        x_ref[rows, :] = x
        _prep_c_rows(x, *consts, *[o.at[rows] for o in outs])


def _prep_ab_rows(x, gmix_ref, w0_ref, qag_ref, wq_ref, kvag_ref, wkv_ref,
                    mq_ref, mqt_ref, ilq_ref, gq_ref,
                    mk_ref, mkt_ref, ilk_ref, gk_ref,
                    mr_ref, mrt_ref, ilr_ref, gr_ref,
                    mbq_ref, mbqt_ref, ilbq_ref, gbq_ref,
                    mbk_ref, mbkt_ref, ilbk_ref, gbk_ref,
                    cosa_ref, sina_ref, cosb_ref, sinb_ref,
                    qa_ref, ka_ref, va_ref, qb_ref, kb_ref, vb_ref):
    lo, first_half = _lane_masks()
    h = _row_rmsnorm(x, gmix_ref[...]).astype(BF16)
    proj = _dot(h, w0_ref[...])
    cosa, sina = cosa_ref[...], sina_ref[...]
    cosb, sinb = cosb_ref[...], sinb_ref[...]

    qn = _row_rmsnorm(proj[:, 0:256], qag_ref[...]).astype(BF16)
    q = _seg_rmsnorm(_dot(qn, wq_ref[...]), mq_ref[...], mqt_ref[...], ilq_ref[...], gq_ref[...])
    for hd in range(8):
        sl = slice(LANES * hd, LANES * (hd + 1))
        qa_ref[:, sl] = _rope128(q[:, sl], cosa, sina, first_half).astype(BF16)

    kvn = _row_rmsnorm(proj[:, 256:384], kvag_ref[...]).astype(BF16)
    kv = _dot(kvn, wkv_ref[...])
    ones_hi = jnp.where(lo, 0.0, 1.0)
    kn = _seg_rmsnorm(kv[:, 0:1024], mk_ref[...], mkt_ref[...], ilk_ref[...], gk_ref[...])
    kr = _seg_rmsnorm(proj[:, 384:512], mr_ref[...], mrt_ref[...], ilr_ref[...], gr_ref[...])
    kr = _rope128(kr, cosa, sina, first_half)
    for hd in range(8):
        sl = slice(LANES * hd, LANES * (hd + 1))
        ka_ref[:, sl] = (kn[:, sl] + kr).astype(BF16)
        va_ref[:, sl] = (kv[:, 1024 + LANES * hd:1024 + LANES * (hd + 1)] + ones_hi).astype(BF16)

    bq = _seg_rmsnorm(proj[:, 512:1024], mbq_ref[...], mbqt_ref[...], ilbq_ref[...], gbq_ref[...])
    for blk in range(4):
        sl = slice(LANES * blk, LANES * (blk + 1))
        qb_ref[:, sl] = _rope128(bq[:, sl], cosb, sinb, first_half).astype(BF16)
    bk = _seg_rmsnorm(proj[:, 1024:1152], mbk_ref[...], mbkt_ref[...], ilbk_ref[...], gbk_ref[...])
    bk = _rope128(bk, cosb, sinb, first_half)
    k0, k1 = _dup_halves(bk, lo)
    kb_ref[:, 0:LANES] = k0.astype(BF16)
    kb_ref[:, LANES:2 * LANES] = k1.astype(BF16)
    bv = proj[:, 1152:1280]
    vb_ref[:, 0:LANES] = jnp.where(lo, bv, 1.0).astype(BF16)
    vb_ref[:, LANES:2 * LANES] = jnp.where(lo, pltpu.roll(bv, HEAD, 1), 1.0).astype(BF16)


def _prep_c_rows(x, gmix_ref, w_ref,
                   mq_ref, mqt_ref, ilq_ref, gq_ref,
                   mk_ref, mkt_ref, ilk_ref, gk_ref,
                   qc_ref, kc_ref, vc_ref):
    lo, _ = _lane_masks()
    h = _row_rmsnorm(x, gmix_ref[...]).astype(BF16)
    proj = _dot(h, w_ref[...])
    q = _seg_rmsnorm(proj[:, 0:1024], mq_ref[...], mqt_ref[...], ilq_ref[...], gq_ref[...])
    qc_ref[...] = q.astype(BF16)
    k = _seg_rmsnorm(proj[:, 1024:1280], mk_ref[...], mkt_ref[...], ilk_ref[...], gk_ref[...])
    for blk in range(2):
        sl = slice(LANES * blk, LANES * (blk + 1))
        k0, k1 = _dup_halves(k[:, sl], lo)
        kc_ref[:, 2 * blk * LANES:(2 * blk + 1) * LANES] = k0.astype(BF16)
        kc_ref[:, (2 * blk + 1) * LANES:(2 * blk + 2) * LANES] = k1.astype(BF16)
        v = proj[:, 1280 + LANES * blk:1280 + LANES * (blk + 1)]
        vc_ref[:, 2 * blk * LANES:(2 * blk + 1) * LANES] = jnp.where(lo, v, 1.0).astype(BF16)
        vc_ref[:, (2 * blk + 1) * LANES:(2 * blk + 2) * LANES] = jnp.where(
            lo, pltpu.roll(v, HEAD, 1), 1.0).astype(BF16)


def _softmax_pv(s, v):
    m = jnp.max(s, axis=-1, keepdims=True)
    p = jnp.exp(s - m)
    l = jnp.sum(p, axis=-1, keepdims=True)
    return _dot(p.astype(BF16), v) / l


def _lane_chunk_reduce(op, t):
    out = t[:, 0:LANES]
    for j in range(1, t.shape[1] // LANES):
        out = op(out, t[:, LANES * j:LANES * (j + 1)])
    return out


def _scores_phase(q, k_ref, lanes, s_buf):
    m_part = None
    for c in range(k_ref.shape[1] // KEY_CHUNK):
        ks = slice(KEY_CHUNK * c, KEY_CHUNK * (c + 1))
        s_c = _dot_nt(q, k_ref[0, ks, lanes])
        s_buf[:, ks] = s_c
        mc = _lane_chunk_reduce(jnp.maximum, s_c)
        m_part = mc if m_part is None else jnp.maximum(m_part, mc)
    return jnp.max(m_part, axis=-1, keepdims=True)


def _pv_phase(s_buf, m, v_ref, lanes):
    lo, _ = _lane_masks()
    acc = None
    for c in range(v_ref.shape[1] // KEY_CHUNK):
        ks = slice(KEY_CHUNK * c, KEY_CHUNK * (c + 1))
        p = jnp.exp2(s_buf[:, ks] - m)
        pv = _dot(p.astype(BF16), v_ref[0, ks, lanes])
        acc = pv if acc is None else acc + pv
    return acc / jnp.where(lo, pltpu.roll(acc, HEAD, 1), 1.0)


def _pair_heads(o_even, o_odd):
    lo, _ = _lane_masks()
    return jnp.where(lo, o_even, pltpu.roll(o_odd, HEAD, 1))


def _mla_attn_kernel(q_ref, k_ref, v_ref, o_ref, s0_ref, s1_ref):
    first, second = slice(0, LANES), slice(LANES, 2 * LANES)
    m0 = _scores_phase(q_ref[0, :, first], k_ref, first, s0_ref)
    m1 = _scores_phase(q_ref[0, :, second], k_ref, second, s1_ref)
    o0 = _pv_phase(s0_ref, m0, v_ref, first)
    o1 = _pv_phase(s1_ref, m1, v_ref, second)
    o_ref[0] = _pair_heads(o0, o1).astype(o_ref.dtype)


def _gqa_attn_kernel(q_ref, k_ref, v_ref, o_ref, s0_ref, s1_ref):
    lo, _ = _lane_masks()
    tq = q_ref.shape[1]
    hi = jnp.logical_not(lo)
    ms = []
    for j, s_ref in enumerate((s0_ref, s1_ref)):
        blk = q_ref[0, :, LANES * j:LANES * (j + 1)]
        zero = jnp.zeros_like(blk)
        qs = jnp.concatenate([jnp.where(lo, blk, zero), jnp.where(hi, blk, zero)], axis=0)
        ms.append(_scores_phase(qs, k_ref, slice(0, LANES), s_ref))
    for j, s_ref in enumerate((s0_ref, s1_ref)):
        o = _pv_phase(s_ref, ms[j], v_ref, slice(0, LANES))
        o_ref[0, :, LANES * j:LANES * (j + 1)] = _pair_heads(o[0:tq], o[tq:2 * tq]).astype(o_ref.dtype)


def _win_attn_kernel(sink_ref, q_ref, kp_ref, kc_ref, kn_ref, vp_ref, vc_ref, vn_ref, bias_ref, o_ref,
                     s_scr, p_scr, m_scr, *, seq_len):
    lo, _ = _lane_masks()
    tq = Q_BLOCK
    nq = q_ref.shape[1] // tq
    i = pl.program_id(1)
    kj = lax.broadcasted_iota(jnp.int32, (1, 3 * tq), 1)
    units = [(g, u) for g in range(4) for u in range(nq)]
    for idx, (g, u) in enumerate(units):
        sl = slice(LANES * g, LANES * (g + 1))
        kcat = jnp.concatenate([kp_ref[0, :, sl], kc_ref[0, :, sl], kn_ref[0, :, sl]], axis=0)
        parts = []
        for a in range(4):
            blk = q_ref[0, tq * u:tq * (u + 1), LANES * (2 * g + a // 2):LANES * (2 * g + a // 2 + 1)]
            keep = lo if a % 2 == 0 else jnp.logical_not(lo)
            parts.append(jnp.where(keep, blk, jnp.zeros_like(blk)))
        s_scr[idx] = _dot_nt(jnp.concatenate(parts, axis=0), kcat[tq * u:tq * (u + 3)])
    for idx, (g, u) in enumerate(units):
        key_pos = (i * nq + u - 1) * tq + kj
        valid = jnp.logical_and(key_pos >= 0, key_pos < seq_len)
        for rt in range(4 * tq // WIN_ROW_TILE):
            a, r0 = divmod(rt * WIN_ROW_TILE, tq)
            rows = slice(rt * WIN_ROW_TILE, (rt + 1) * WIN_ROW_TILE)
            s = s_scr[idx, rows, :] + bias_ref[4 * g + a, r0:r0 + WIN_ROW_TILE, :]
            s = jnp.where(valid, s, -jnp.inf)
            m = jnp.maximum(jnp.max(_lane_chunk_reduce(jnp.maximum, s), axis=-1, keepdims=True),
                            sink_ref[4 * g + a] * LOG2E)
            p_scr[idx, rows, :] = jnp.exp2(s - m).astype(BF16)
            m_scr[idx, rows, :] = jnp.broadcast_to(m, (WIN_ROW_TILE, LANES))
    for idx, (g, u) in enumerate(units):
        sl = slice(LANES * g, LANES * (g + 1))
        vcat = jnp.concatenate([vp_ref[0, :, sl], vc_ref[0, :, sl], vn_ref[0, :, sl]], axis=0)
        acc = _dot(p_scr[idx], vcat[tq * u:tq * (u + 3)])
        outs = []
        for a in range(4):
            rows = slice(tq * a, tq * (a + 1))
            sink_term = jnp.exp2(sink_ref[4 * g + a] * LOG2E - m_scr[idx, rows, :])
            den = jnp.where(lo, pltpu.roll(acc[rows], HEAD, 1) + sink_term, 1.0)
            outs.append(acc[rows] / den)
        for j in range(2):
            blk = 2 * g + j
            o_ref[0, tq * u:tq * (u + 1), LANES * blk:LANES * (blk + 1)] = _pair_heads(
                outs[2 * j], outs[2 * j + 1]).astype(o_ref.dtype)


def _rel_bias_kernel(bucket_ref, band_ref, relt_ref, o_ref):
    bucket = bucket_ref[...]
    acc = jnp.zeros(o_ref.shape, F32)
    for r in range(REL_BUCKETS):
        acc = acc + jnp.where(bucket == r, relt_ref[:, r:r + 1], 0.0)
    o_ref[...] = jnp.where(band_ref[...] > 0, acc * LOG2E, -jnp.inf)


def _to_tiles(ref, val, first_token=0):
    for c in range(TOKEN_TILE_ROWS):
        ref[pl.ds(first_token * TOKEN_TILE_ROWS + c, val.shape[0], stride=TOKEN_TILE_ROWS), :] = (
            val[:, LANES * c:LANES * (c + 1)])


def _from_tiles(ref):
    rows = ref.shape[0] // TOKEN_TILE_ROWS
    return jnp.concatenate([ref[pl.ds(c, rows, stride=TOKEN_TILE_ROWS), :] for c in range(TOKEN_TILE_ROWS)],
                           axis=1)


def _route_bucket(xn, wh, wl, rb):
    xh, xl = _split_bf16(xn)
    logit = _dot_nt(wh, xh) + _dot_nt(wh, xl) + _dot_nt(wl, xh) + rb
    g = [logit[r:r + 1, :] for r in range(N_GROUPS)]
    gmax = jnp.maximum(jnp.maximum(g[0], g[1]), jnp.maximum(g[2], g[3]))
    gidx = jnp.where(g[0] == gmax, 0, jnp.where(g[1] == gmax, 1, jnp.where(g[2] == gmax, 2, 3)))
    e = []
    for j in range(EXPERTS_PER_GROUP):
        rows = [logit[N_GROUPS + EXPERTS_PER_GROUP * gg + j:N_GROUPS + EXPERTS_PER_GROUP * gg + j + 1, :]
                for gg in range(N_GROUPS)]
        e.append(jnp.where(gidx == 0, rows[0], jnp.where(gidx == 1, rows[1],
                                                         jnp.where(gidx == 2, rows[2], rows[3]))))
    emax = jnp.maximum(jnp.maximum(e[0], e[1]), jnp.maximum(e[2], e[3]))
    ex = [jnp.exp(ej - emax) for ej in e]
    esum = ex[0] + ex[1] + ex[2] + ex[3]
    pr = [exj / esum for exj in ex]
    p1 = jnp.maximum(jnp.maximum(pr[0], pr[1]), jnp.maximum(pr[2], pr[3]))
    i1 = jnp.where(pr[0] == p1, 0, jnp.where(pr[1] == p1, 1, jnp.where(pr[2] == p1, 2, 3)))
    rest = [jnp.where(i1 == j, -1.0, pr[j]) for j in range(EXPERTS_PER_GROUP)]
    p2 = jnp.maximum(jnp.maximum(rest[0], rest[1]), jnp.maximum(rest[2], rest[3]))
    i2 = jnp.where(rest[0] == p2, 0, jnp.where(rest[1] == p2, 1, jnp.where(rest[2] == p2, 2, 3)))
    lo_e = jnp.minimum(i1, i2)
    hi_e = jnp.maximum(i1, i2)
    pair = jnp.where(lo_e == 0, hi_e - 1, jnp.where(lo_e == 1, hi_e + 1, 5))
    return gidx * PAIRS_PER_GROUP + pair


def _post_kernel(*refs, n_parts):
    parts = refs[:n_parts]
    ws = refs[n_parts:2 * n_parts]
    (x_ref, gffn_ref, wrh_ref, wrl_ref, rb_ref, tri_ref,
     x1t_ref, bucket_ref, rank_ref, counts_ref, carry_ref) = refs[2 * n_parts:]

    @pl.when(pl.program_id(0) == 0)
    def _():
        carry_ref[...] = jnp.zeros_like(carry_ref)

    sub = tri_ref.shape[0]
    buckets = []
    for h in range(x_ref.shape[0] // sub):
        rows = slice(sub * h, sub * (h + 1))
        acc = x_ref[rows, :]
        for p_ref, w_ref in zip(parts, ws):
            acc = acc + _dot(p_ref[rows, :], w_ref[...])
        _to_tiles(x1t_ref, acc, sub * h)
        bucket = _route_bucket(_row_rmsnorm(acc, gffn_ref[...]), wrh_ref[...], wrl_ref[...], rb_ref[...])
        bucket_ref[:, rows] = bucket
        buckets.append(bucket)
    for h, bucket in enumerate(buckets):
        rows = slice(sub * h, sub * (h + 1))
        onehot = (lax.broadcasted_iota(jnp.int32, (ROUTER_ROWS, sub), 0) == bucket).astype(F32)
        before = _dot(onehot.astype(BF16), tri_ref[...]) + carry_ref[:, 0:1]
        rank_ref[:, rows] = jnp.sum(onehot * before, axis=0, keepdims=True).astype(jnp.int32)
        carry_ref[...] = carry_ref[...] + jnp.sum(onehot, axis=1, keepdims=True)
    counts_ref[...] = carry_ref[...]


def _start_row_gather(idx_of, table_hbm, dst, sem):
    def body(blk, carry):
        for j in range(GATHER_UNROLL):
            r = blk * GATHER_UNROLL + j
            src_row = pl.multiple_of(idx_of(r) * TOKEN_TILE_ROWS, TOKEN_TILE_ROWS)
            dst_row = pl.multiple_of(r * TOKEN_TILE_ROWS, TOKEN_TILE_ROWS)
            pltpu.make_async_copy(table_hbm.at[pl.ds(src_row, TOKEN_TILE_ROWS)],
                                  dst.at[pl.ds(dst_row, TOKEN_TILE_ROWS)], sem).start()
        return carry
    lax.fori_loop(0, dst.shape[0] // (TOKEN_TILE_ROWS * GATHER_UNROLL), body, 0)


def _wait_row_gather(table_hbm, dst, sem):
    pltpu.make_async_copy(table_hbm.at[pl.ds(0, dst.shape[0])], dst, sem).wait()


def _tile_gates(logit, ea, eb):
    lane = lax.broadcasted_iota(jnp.int32, (1, LANES), 1)
    pick = lambda idx: jnp.sum(jnp.where(lane == idx, logit, 0.0), axis=-1, keepdims=True)
    glog = jnp.where(lane < N_GROUPS, logit, -jnp.inf)
    gmax = jnp.max(glog, axis=-1, keepdims=True)
    gsum = jnp.sum(jnp.exp(glog - gmax), axis=-1, keepdims=True)
    g_p = jnp.exp(pick(ea // EXPERTS_PER_GROUP) - gmax) / gsum
    la, lb = pick(N_GROUPS + ea), pick(N_GROUPS + eb)
    top = jnp.maximum(la, lb)
    pa, pb = jnp.exp(la - top), jnp.exp(lb - top)
    return g_p * pa / (pa + pb), g_p * pb / (pa + pb)


def _moe_sparse_kernel(ea_ref, eb_ref, nt_ref, first_ref, order_ref, gffn_ref, wr_ref, rb_ref, x1t_hbm,
                       wga_ref, wua_ref, wda_ref, wgb_ref, wub_ref, wdb_ref, o_ref, buf, sem):
    t = pl.program_id(0)
    nt = nt_ref[0]
    slot = t % 2

    def tokens_of(tile):
        base = first_ref[tile]
        return lambda r: order_ref[base + r]

    @pl.when(t == 0)
    def _():
        _start_row_gather(tokens_of(0), x1t_hbm, buf.at[0], sem.at[0])

    @pl.when(t + 1 < nt)
    def _():
        _start_row_gather(tokens_of(t + 1), x1t_hbm, buf.at[1 - slot], sem.at[1 - slot])

    @pl.when(t < nt)
    def _():
        _wait_row_gather(x1t_hbm, buf.at[slot], sem.at[slot])
        x = _from_tiles(buf.at[slot])
        xn = _row_rmsnorm(x, gffn_ref[...])
        gate_a, gate_b = _tile_gates(_dot(xn, wr_ref[...]) + rb_ref[...], ea_ref[t], eb_ref[t])
        hid_a = jax.nn.silu(_dot(xn, wga_ref[0, 0])) * _dot(xn, wua_ref[0, 0])
        hid_b = jax.nn.silu(_dot(xn, wgb_ref[0, 0])) * _dot(xn, wub_ref[0, 0])
        out = x + gate_a * _dot(hid_a, wda_ref[0, 0]) + gate_b * _dot(hid_b, wdb_ref[0, 0])
        _to_tiles(o_ref, out)

    @pl.when(t >= nt)
    def _():
        o_ref[...] = jnp.zeros_like(o_ref)


def _unpermute_kernel(pos_cur_ref, pos_nxt_ref, sorted_hbm, o_ref, buf, sem):
    t = pl.program_id(0)
    slot = t % 2

    @pl.when(t == 0)
    def _():
        _start_row_gather(lambda r: pos_cur_ref[0, 0, r], sorted_hbm, buf.at[0], sem.at[0])

    @pl.when(t + 1 < pl.num_programs(0))
    def _():
        _start_row_gather(lambda r: pos_nxt_ref[0, 0, r], sorted_hbm, buf.at[1 - slot], sem.at[1 - slot])

    _wait_row_gather(sorted_hbm, buf.at[slot], sem.at[slot])
    o_ref[...] = _from_tiles(buf.at[slot])


def _seg_mats(width, segments):
    m = np.zeros((width, LANES), np.float32)
    invlen = np.ones((1, LANES), np.float32)
    for c, (start, length) in enumerate(segments):
        m[start:start + length, c] = 1.0
        invlen[0, c] = 1.0 / length
    return jnp.asarray(m, BF16), jnp.asarray(np.concatenate([m.T, m.T], axis=0), BF16), jnp.asarray(invlen)


def _rope_tables(pos_list, seq_len):
    inv = np.float32(ROPE_THETA) ** (-np.arange(0, 2 * ROPE_HALF, 2, dtype=np.float32) / np.float32(2 * ROPE_HALF))
    cos_cols, sin_cols = [], []
    for pos in pos_list:
        if pos is None:
            cos_cols.append(np.ones((seq_len, 2 * ROPE_HALF), np.float32))
            sin_cols.append(np.zeros((seq_len, 2 * ROPE_HALF), np.float32))
        else:
            ang = pos.astype(np.float32)[:, None] * inv[None, :]
            c, s = np.cos(ang), np.sin(ang)
            cos_cols.append(np.concatenate([c, c], axis=1))
            sin_cols.append(np.concatenate([-s, s], axis=1))
    return jnp.asarray(np.concatenate(cos_cols, axis=1)), jnp.asarray(np.concatenate(sin_cols, axis=1))


def _t5_bucket_np(rel):
    nb = REL_BUCKETS // 2
    max_exact = nb // 2
    ret = np.where(rel > 0, nb, 0)
    n = np.abs(rel)
    nf = np.maximum(n, 1).astype(np.float32)
    large = max_exact + (np.log(nf / np.float32(max_exact)) / np.float32(math.log(REL_MAX_DIST / max_exact))
                         * np.float32(nb - max_exact)).astype(np.int32)
    large = np.minimum(large, nb - 1)
    return ret + np.where(n < max_exact, n, large)


def _full(shape):
    nd = len(shape)
    return pl.BlockSpec(shape, lambda *_: (0,) * nd)


def _params(*sem):
    return pltpu.CompilerParams(dimension_semantics=sem, vmem_limit_bytes=VMEM_LIMIT)


def _prep_ab(x2, seq_len, gmix, w_in, qag, wq, kvag, wkv, qn_g, kn_g, qr_g, kr_g, bq_g, bk_g, tm=512):
    n = x2.shape[0]
    scale_a = (HEAD + 2 * ROPE_HALF) ** -0.5
    scale_b = HEAD ** -0.5
    zeros = lambda r, c: jnp.zeros((r, c), F32)
    d = w_in.shape[0]
    w0 = jnp.concatenate([w_in[:, 0:384], zeros(d, HEAD), w_in[:, 384:416], zeros(d, 32), w_in[:, 416:1184]],
                         axis=1).astype(BF16)
    wq_p = jnp.concatenate([wq.reshape(-1, 8, 96), jnp.zeros((wq.shape[0], 8, 32), F32)], axis=2)
    wq_p = wq_p.reshape(-1, 1024).astype(BF16)
    wkv_r = wkv.reshape(-1, 8, 128)
    wk_p = jnp.concatenate([wkv_r[:, :, :HEAD], jnp.zeros_like(wkv_r[:, :, :HEAD])], axis=2).reshape(-1, 1024)
    wv_p = jnp.concatenate([wkv_r[:, :, HEAD:], jnp.zeros_like(wkv_r[:, :, HEAD:])], axis=2).reshape(-1, 1024)
    wkv_p = jnp.concatenate([wk_p, wv_p], axis=1).astype(BF16)

    seg_q = [(LANES * h, HEAD) for h in range(8)] + [(LANES * h + HEAD, 32) for h in range(8)]
    gq = jnp.tile(jnp.concatenate([qn_g, qr_g, jnp.zeros((32,), F32)]), 8)[None, :] * (scale_a * LOG2E)
    seg_k = [(LANES * h, HEAD) for h in range(8)]
    gk = jnp.tile(jnp.concatenate([kn_g, jnp.zeros((HEAD,), F32)]), 8)[None, :]
    seg_r = [(HEAD, 32)]
    gr = jnp.concatenate([jnp.zeros((HEAD,), F32), kr_g, jnp.zeros((32,), F32)])[None, :]
    seg_bq = [(HEAD * h, HEAD) for h in range(8)]
    gbq = jnp.tile(bq_g, 8)[None, :] * (scale_b * LOG2E)
    seg_bk = [(HEAD * h, HEAD) for h in range(2)]
    gbk = jnp.tile(bk_g, 2)[None, :]

    pos = np.arange(seq_len)
    cosa, sina = _rope_tables([None, None, pos, None], seq_len)
    row, col = pos // GRID_W, pos % GRID_W
    cosb, sinb = _rope_tables([row, col, row, col], seq_len)

    consts = [gmix[None, :], w0, qag[None, :], wq_p, kvag[None, :], wkv_p,
              *_seg_mats(1024, seg_q), gq, *_seg_mats(1024, seg_k), gk, *_seg_mats(LANES, seg_r), gr,
              *_seg_mats(512, seg_bq), gbq, *_seg_mats(LANES, seg_bk), gbk]
    nsb = seq_len // tm
    rope_spec = pl.BlockSpec((tm, LANES), lambda i: (i % nsb, 0))
    row_spec = lambda w: pl.BlockSpec((tm, w), lambda i: (i, 0))
    out_widths = (1024, 1024, 1024, 512, 256, 256)
    return pl.pallas_call(
        _prep_ab_kernel,
        grid=(n // tm,),
        in_specs=[row_spec(x2.shape[1])] + [_full(c.shape) for c in consts] + [rope_spec] * 4,
        out_specs=[row_spec(w) for w in out_widths],
        out_shape=[jax.ShapeDtypeStruct((n, w), BF16) for w in out_widths],
        compiler_params=_params("parallel"),
        name="prep_ab",
    )(x2, *consts, cosa, sina, cosb, sinb)


def _prep_c(sorted_rows, pos, gmix, w_in, q_g, k_g, tm=512):
    n = pos.shape[0]
    d = TOKEN_TILE_ROWS * LANES
    seg_q = [(HEAD * h, HEAD) for h in range(16)]
    gq = jnp.tile(q_g, 16)[None, :] * (HEAD ** -0.5 * LOG2E)
    seg_k = [(HEAD * h, HEAD) for h in range(4)]
    gk = jnp.tile(k_g, 4)[None, :]
    consts = [gmix[None, :], w_in.astype(BF16), *_seg_mats(1024, seg_q), gq, *_seg_mats(256, seg_k), gk]
    row_spec = lambda w: pl.BlockSpec((tm, w), lambda i: (i, 0))
    out_widths = (1024, 512, 512)
    pos2 = jnp.concatenate([pos, jnp.zeros((tm,), jnp.int32)]).reshape(n // tm + 1, 1, tm)
    idx_spec = lambda off: pl.BlockSpec((1, 1, tm), lambda i: (i + off, 0, 0), memory_space=pltpu.SMEM)
    return pl.pallas_call(
        _prep_c_kernel,
        grid=(n // tm,),
        in_specs=[idx_spec(0), idx_spec(1)] + [_full(c.shape) for c in consts] + [pl.BlockSpec(memory_space=pl.ANY)],
        out_specs=[row_spec(d)] + [row_spec(w) for w in out_widths],
        out_shape=[jax.ShapeDtypeStruct((n, d), F32)] + [jax.ShapeDtypeStruct((n, w), BF16) for w in out_widths],
        scratch_shapes=[pltpu.VMEM((2, tm * TOKEN_TILE_ROWS, LANES), F32), pltpu.SemaphoreType.DMA((2,))],
        compiler_params=_params("arbitrary"),
        name="prep_c",
    )(pos2, pos2, *consts, sorted_rows)


def _mla_attn(qa, ka, va, tq=1024):
    b, s, _ = qa.shape
    return pl.pallas_call(
        _mla_attn_kernel,
        grid=(b, 4, s // tq),
        in_specs=[pl.BlockSpec((1, tq, 2 * LANES), lambda bi, hp, qi: (bi, qi, hp)),
                  pl.BlockSpec((1, s, 2 * LANES), lambda bi, hp, qi: (bi, 0, hp)),
                  pl.BlockSpec((1, s, 2 * LANES), lambda bi, hp, qi: (bi, 0, hp))],
        out_specs=pl.BlockSpec((1, tq, LANES), lambda bi, hp, qi: (bi, qi, hp)),
        out_shape=jax.ShapeDtypeStruct((b, s, 512), BF16),
        scratch_shapes=[pltpu.VMEM((tq, s), F32), pltpu.VMEM((tq, s), F32)],
        compiler_params=_params("parallel", "parallel", "parallel"),
        name="mla_attn",
    )(qa, ka, va)


def _gqa_attn(qb, kb, vb, tq=512):
    b, s, _ = qb.shape
    return pl.pallas_call(
        _gqa_attn_kernel,
        grid=(b, 2, s // tq),
        in_specs=[pl.BlockSpec((1, tq, 2 * LANES), lambda bi, g, qi: (bi, qi, g)),
                  pl.BlockSpec((1, s, LANES), lambda bi, g, qi: (bi, 0, g)),
                  pl.BlockSpec((1, s, LANES), lambda bi, g, qi: (bi, 0, g))],
        out_specs=pl.BlockSpec((1, tq, 2 * LANES), lambda bi, g, qi: (bi, qi, g)),
        out_shape=jax.ShapeDtypeStruct((b, s, 512), BF16),
        scratch_shapes=[pltpu.VMEM((2 * tq, s), F32), pltpu.VMEM((2 * tq, s), F32)],
        compiler_params=_params("parallel", "parallel", "parallel"),
        name="gqa_attn",
    )(qb, kb, vb)


def _rel_bias_table(rel_bias):
    span = Q_BLOCK + 2 * WINDOW
    rel = np.arange(span)[None, :] - WINDOW - np.arange(Q_BLOCK)[:, None]
    bucket = _t5_bucket_np(rel).astype(np.int32).reshape(1, -1)
    band = (np.abs(rel) <= WINDOW).astype(np.int32).reshape(1, -1)
    heads = rel_bias.shape[1]
    cols = bucket.shape[1]
    chunk = cols // 8
    table = pl.pallas_call(
        _rel_bias_kernel,
        grid=(8,),
        in_specs=[pl.BlockSpec((1, chunk), lambda i: (0, i)), pl.BlockSpec((1, chunk), lambda i: (0, i)),
                  _full((heads, REL_BUCKETS))],
        out_specs=pl.BlockSpec((heads, chunk), lambda i: (0, i)),
        out_shape=jax.ShapeDtypeStruct((heads, cols), F32),
        compiler_params=_params("parallel"),
        name="rel_bias",
    )(jnp.asarray(bucket), jnp.asarray(band), rel_bias.T)
    return table.reshape(heads, Q_BLOCK, span)


def _win_attn(qc, kc, vc, sink, bias, nq=4):
    b, s, _ = qc.shape
    tq = Q_BLOCK
    nb = s // tq
    prev = lambda bi, i: (bi, jnp.maximum(i * nq - 1, 0), 0)
    cur = lambda bi, i: (bi, i, 0)
    nxt = lambda bi, i: (bi, jnp.minimum((i + 1) * nq, nb - 1), 0)
    edge_spec = lambda im: pl.BlockSpec((1, tq, 4 * LANES), im)
    cur_spec = pl.BlockSpec((1, nq * tq, 4 * LANES), cur)
    return pl.pallas_call(
        functools.partial(_win_attn_kernel, seq_len=s),
        grid=(b, nb // nq),
        in_specs=[pl.BlockSpec(memory_space=pltpu.SMEM),
                  pl.BlockSpec((1, nq * tq, 8 * LANES), cur),
                  edge_spec(prev), cur_spec, edge_spec(nxt),
                  edge_spec(prev), cur_spec, edge_spec(nxt),
                  _full(bias.shape)],
        out_specs=pl.BlockSpec((1, nq * tq, 8 * LANES), cur),
        scratch_shapes=[pltpu.VMEM((4 * nq, 4 * tq, 3 * tq), F32), pltpu.VMEM((4 * nq, 4 * tq, 3 * tq), BF16),
                        pltpu.VMEM((4 * nq, 4 * tq, LANES), F32)],
        out_shape=jax.ShapeDtypeStruct((b, s, 1024), BF16),
        compiler_params=_params("parallel", "parallel"),
        name="win_attn",
    )(sink, qc, kc, kc, kc, vc, vc, vc, bias)


def _post(parts, ws, x2, gffn, w_group, b_group, w_router, b_router, tm=1024, sub=256):
    n, d = x2.shape
    wr = jnp.concatenate([w_group.T, w_router.T, jnp.zeros((ROUTER_ROWS - N_GROUPS - N_EXPERTS, d), F32)], axis=0)
    wrh = wr.astype(BF16)
    wrl = (wr - wrh.astype(F32)).astype(BF16)
    rb = jnp.concatenate([b_group, b_router, jnp.zeros((ROUTER_ROWS - N_GROUPS - N_EXPERTS,), F32)])[:, None]
    row_spec = lambda w: pl.BlockSpec((tm, w), lambda i: (i, 0))
    lane_spec = lambda r: pl.BlockSpec((r, tm), lambda i: (0, i))
    ws = [w.astype(BF16) for w in ws]
    tri = jnp.asarray(np.triu(np.ones((sub, sub), np.float32), 1), BF16)
    return pl.pallas_call(
        functools.partial(_post_kernel, n_parts=len(parts)),
        grid=(n // tm,),
        in_specs=[row_spec(p.shape[1]) for p in parts] + [_full(w.shape) for w in ws]
                 + [row_spec(d), _full((1, d)), _full(wrh.shape), _full(wrl.shape), _full(rb.shape),
                    _full(tri.shape)],
        out_specs=[pl.BlockSpec((tm * TOKEN_TILE_ROWS, LANES), lambda i: (i, 0)),
                   lane_spec(1), lane_spec(1), _full((ROUTER_ROWS, LANES))],
        out_shape=[jax.ShapeDtypeStruct((n * TOKEN_TILE_ROWS, LANES), F32),
                   jax.ShapeDtypeStruct((1, n), jnp.int32), jax.ShapeDtypeStruct((1, n), jnp.int32),
                   jax.ShapeDtypeStruct((ROUTER_ROWS, LANES), F32)],
        scratch_shapes=[pltpu.VMEM((ROUTER_ROWS, LANES), F32)],
        compiler_params=_params("arbitrary"),
        name="post",
    )(*parts, *ws, x2, gffn[None, :], wrh, wrl, rb, tri)


def _route(bucket, rank, counts, tmm):
    n = bucket.shape[1]
    n_tiles_max = n // tmm + N_BUCKETS
    bucket, rank = bucket[0], rank[0]
    cnt = counts[:N_BUCKETS, 0].astype(jnp.int32)
    padded = (cnt + tmm - 1) // tmm * tmm
    ends = jnp.cumsum(padded)
    starts = ends - padded
    pos = starts[bucket] + rank
    n_tiles = ends[-1] // tmm
    tile = jnp.minimum(jnp.arange(n_tiles_max, dtype=jnp.int32), n_tiles - 1)
    tile_bucket = jnp.sum((tile * tmm)[:, None] >= ends[None, :], axis=1).astype(jnp.int32)
    grp, pair = tile_bucket // PAIRS_PER_GROUP, tile_bucket % PAIRS_PER_GROUP
    tile_ea = grp * EXPERTS_PER_GROUP + jnp.asarray(PAIR_LO, jnp.int32)[pair]
    tile_eb = grp * EXPERTS_PER_GROUP + jnp.asarray(PAIR_HI, jnp.int32)[pair]
    token = jnp.arange(n, dtype=jnp.int32)
    _, by_bucket = lax.sort((bucket, token), num_keys=1, is_stable=True)
    order = jnp.concatenate([by_bucket, token[:tmm]])
    first = jnp.cumsum(cnt) - cnt
    tile_first = first[tile_bucket] + tile * tmm - starts[tile_bucket]
    return pos, order, tile_first, tile_ea, tile_eb, n_tiles.reshape(1)


def _moe_sparse(x1t, gffn, w_group, b_group, w_router, b_router, order, tile_first, tile_ea, tile_eb, n_tiles,
                w_gate, w_up, w_down, layer, tmm):
    n_tiles_max = tile_ea.shape[0]
    _, ne, d, de = w_gate.shape
    up_spec = lambda which: pl.BlockSpec((1, 1, d, de), lambda t, ea, eb, *_: (layer, (ea, eb)[which][t], 0, 0))
    down_spec = lambda which: pl.BlockSpec((1, 1, de, d), lambda t, ea, eb, *_: (layer, (ea, eb)[which][t], 0, 0))
    wg, wu, wd = w_gate, w_up, w_down
    pad = LANES - N_GROUPS - N_EXPERTS
    wr = jnp.concatenate([w_group, w_router, jnp.zeros((d, pad), F32)], axis=1)
    rb = jnp.concatenate([b_group, b_router, jnp.zeros((pad,), F32)])[None, :]
    return pl.pallas_call(
        _moe_sparse_kernel,
        grid_spec=pltpu.PrefetchScalarGridSpec(
            num_scalar_prefetch=5,
            grid=(n_tiles_max,),
            in_specs=[pl.BlockSpec((1, d), lambda t, *_: (0, 0)),
                      pl.BlockSpec((d, LANES), lambda t, *_: (0, 0)),
                      pl.BlockSpec((1, LANES), lambda t, *_: (0, 0)),
                      pl.BlockSpec(memory_space=pl.ANY),
                      up_spec(0), up_spec(0), down_spec(0), up_spec(1), up_spec(1), down_spec(1)],
            out_specs=pl.BlockSpec((tmm * TOKEN_TILE_ROWS, LANES), lambda t, *_: (t, 0)),
            scratch_shapes=[pltpu.VMEM((2, tmm * TOKEN_TILE_ROWS, LANES), F32), pltpu.SemaphoreType.DMA((2,))]),
        out_shape=jax.ShapeDtypeStruct((n_tiles_max * tmm * TOKEN_TILE_ROWS, LANES), F32),
        compiler_params=_params("arbitrary"),
        name="moe_sparse",
    )(tile_ea, tile_eb, n_tiles, tile_first, order, gffn[None, :], wr, rb, x1t, wg, wu, wd, wg, wu, wd)


def _unpermute(sorted_rows, pos, tm=256):
    n = pos.shape[0]
    d = TOKEN_TILE_ROWS * LANES
    pos2 = jnp.concatenate([pos, jnp.zeros((tm,), jnp.int32)]).reshape(n // tm + 1, 1, tm)
    idx_spec = lambda off: pl.BlockSpec((1, 1, tm), lambda t: (t + off, 0, 0), memory_space=pltpu.SMEM)
    return pl.pallas_call(
        _unpermute_kernel,
        grid=(n // tm,),
        in_specs=[idx_spec(0), idx_spec(1), pl.BlockSpec(memory_space=pl.ANY)],
        out_specs=pl.BlockSpec((tm, d), lambda t: (t, 0)),
        out_shape=jax.ShapeDtypeStruct((n, d), F32),
        scratch_shapes=[pltpu.VMEM((2, tm * TOKEN_TILE_ROWS, LANES), F32), pltpu.SemaphoreType.DMA((2,))],
        compiler_params=_params("arbitrary"),
        name="unpermute",
    )(pos2, pos2, sorted_rows)


def kernel(x, mix_norm, ffn_norm, w_in_ab, mla_q_a_norm, mla_w_q_up, mla_kv_a_norm, mla_w_kv_up, mla_qn_gain, mla_kn_gain, mla_qr_gain, mla_kr_gain, gqa_q_gain, gqa_k_gain, w_out_ab, w_in_c, win_q_gain, win_k_gain, win_sink, w_out_c, rel_bias, moe_w_group, moe_b_group, moe_w_router, moe_b_router, moe_w_gate, moe_w_up, moe_w_down):
    b, s, d = x.shape
    n = b * s
    depth = mix_norm.shape[0]
    x2 = x.reshape(n, d)
    bias = None
    moe_out = None
    for layer in range(depth):
        i = layer // 2
        if layer % 2 == 0:
            if moe_out is not None:
                x2 = _unpermute(*moe_out)
            qa, ka, va, qb, kb, vb = _prep_ab(
                x2, s, mix_norm[layer], w_in_ab[i], mla_q_a_norm[i], mla_w_q_up[i], mla_kv_a_norm[i],
                mla_w_kv_up[i], mla_qn_gain[i], mla_kn_gain[i], mla_qr_gain[i], mla_kr_gain[i],
                gqa_q_gain[i], gqa_k_gain[i])
            r3 = lambda t: t.reshape(b, s, t.shape[1])
            out_a = _mla_attn(r3(qa), r3(ka), r3(va)).reshape(n, 512)
            out_b = _gqa_attn(r3(qb), r3(kb), r3(vb)).reshape(n, 512)
            parts, ws = [out_a, out_b], [w_out_ab[i][:512], w_out_ab[i][512:]]
        else:
            if bias is None:
                bias = _rel_bias_table(rel_bias)
            x2, qc, kc, vc = _prep_c(*moe_out, mix_norm[layer], w_in_c[i], win_q_gain[i], win_k_gain[i])
            r3 = lambda t: t.reshape(b, s, t.shape[1])
            out_c = _win_attn(r3(qc), r3(kc), r3(vc), win_sink[i], bias).reshape(n, 1024)
            parts, ws = [out_c], [w_out_c[i]]
        router = (moe_w_group[layer], moe_b_group[layer], moe_w_router[layer], moe_b_router[layer])
        x1t, bucket, rank, counts = _post(parts, ws, x2, ffn_norm[layer], *router)
        pos, order, tile_first, tile_ea, tile_eb, n_tiles = _route(bucket, rank, counts, MOE_TILE)
        y_sorted = _moe_sparse(x1t, ffn_norm[layer], *router, order, tile_first, tile_ea, tile_eb, n_tiles,
                               moe_w_gate, moe_w_up, moe_w_down, layer, MOE_TILE)
        moe_out = (y_sorted, pos)
    return _unpermute(*moe_out).reshape(b, s, d)
```

```python
import functools
import math

import numpy as np
import jax
import jax.numpy as jnp
from jax import lax
from jax.experimental import pallas as pl
from jax.experimental.pallas import tpu as pltpu

F32 = jnp.float32
BF16 = jnp.bfloat16

EPS = 1e-6
ROPE_THETA = 10000.0
LANES = 128
HEAD = 64
ROPE_HALF = 16
GRID_W = 64
WINDOW = 128
Q_BLOCK = 128
REL_BUCKETS = 32
REL_MAX_DIST = 128
N_GROUPS = 4
EXPERTS_PER_GROUP = 4
N_EXPERTS = 16
PAIRS_PER_GROUP = 6
N_BUCKETS = N_GROUPS * PAIRS_PER_GROUP
PAIR_LO = (0, 0, 0, 1, 1, 2)
PAIR_HI = (1, 2, 3, 2, 3, 3)
TOKEN_TILE_ROWS = 8
KEY_CHUNK = 512
LOG2E = math.log2(math.e)
PREP_SUB = 256
WIN_ROW_TILE = 32
GATHER_UNROLL = 8
MOE_TILE = 256
ROUTER_ROWS = 32
VMEM_LIMIT = 56 * 1024 * 1024

_NT = (((1,), (1,)), ((), ()))


def _dot(a, b):
    return jnp.dot(a, b, preferred_element_type=F32)


def _dot_nt(a, b):
    return lax.dot_general(a, b, _NT, preferred_element_type=F32)


def _split_bf16(a):
    hi = a.astype(BF16)
    lo = (a - hi.astype(F32)).astype(BF16)
    return hi, lo


def _row_rmsnorm(t, gain):
    return t * lax.rsqrt(jnp.mean(t * t, axis=-1, keepdims=True) + EPS) * gain


def _seg_rmsnorm(t, mseg, msegt, invlen, gain):
    sums = _dot((t * t).astype(BF16), mseg)
    inv = lax.rsqrt(sums * invlen + EPS)
    ihi, ilo = _split_bf16(inv)
    scale = _dot(jnp.concatenate([ihi, ilo], axis=1), msegt)
    return t * scale * gain


def _rope128(t, cos, sin_signed, first_half):
    up = pltpu.roll(t, LANES - ROPE_HALF, 1)
    dn = pltpu.roll(t, ROPE_HALF, 1)
    return t * cos + jnp.where(first_half, up, dn) * sin_signed


def _dup_halves(blk, lo):
    sw = pltpu.roll(blk, HEAD, 1)
    return jnp.where(lo, blk, sw), jnp.where(lo, sw, blk)


def _lane_masks():
    lane = lax.broadcasted_iota(jnp.int32, (1, LANES), 1)
    return lane < HEAD, (lane % (2 * ROPE_HALF)) < ROPE_HALF


def _by_sub_tiles(rows_fn, refs, n_const):
    row_refs = (refs[0],) + tuple(refs[1 + n_const:])
    for h in range(refs[0].shape[0] // PREP_SUB):
        view = [r.at[pl.ds(PREP_SUB * h, PREP_SUB)] for r in row_refs]
        rows_fn(view[0][...], *refs[1:1 + n_const], *view[1:])


SEG_NORM_CONSTS = 4


def _prep_ab_kernel(*refs):
    _by_sub_tiles(_prep_ab_rows, refs, 6 + 5 * SEG_NORM_CONSTS)


def _prep_c_kernel(pos_cur_ref, pos_nxt_ref, *refs):
    n_const = 2 + 2 * SEG_NORM_CONSTS
    consts = refs[:n_const]
    sorted_hbm, x_ref = refs[n_const:n_const + 2]
    outs = refs[n_const + 2:-2]
    buf, sem = refs[-2:]
    t = pl.program_id(0)
    slot = t % 2

    @pl.when(t == 0)
    def _():
        _start_row_gather(lambda r: pos_cur_ref[0, 0, r], sorted_hbm, buf.at[0], sem.at[0])

    @pl.when(t + 1 < pl.num_programs(0))
    def _():
        _start_row_gather(lambda r: pos_nxt_ref[0, 0, r], sorted_hbm, buf.at[1 - slot], sem.at[1 - slot])

    _wait_row_gather(sorted_hbm, buf.at[slot], sem.at[slot])
    for h in range(x_ref.shape[0] // PREP_SUB):
        rows = pl.ds(PREP_SUB * h, PREP_SUB)
        x = _from_tiles(buf.at[slot, pl.ds(PREP_SUB * TOKEN_TILE_ROWS * h, PREP_SUB * TOKEN_TILE_ROWS)])
        x_ref[rows, :] = x
        _prep_c_rows(x, *consts, *[o.at[rows] for o in outs])


def _prep_ab_rows(x, gmix_ref, w0_ref, qag_ref, wq_ref, kvag_ref, wkv_ref,
                    mq_ref, mqt_ref, ilq_ref, gq_ref,
                    mk_ref, mkt_ref, ilk_ref, gk_ref,
                    mr_ref, mrt_ref, ilr_ref, gr_ref,
                    mbq_ref, mbqt_ref, ilbq_ref, gbq_ref,
                    mbk_ref, mbkt_ref, ilbk_ref, gbk_ref,
                    cosa_ref, sina_ref, cosb_ref, sinb_ref,
                    qa_ref, ka_ref, va_ref, qb_ref, kb_ref, vb_ref):
    lo, first_half = _lane_masks()
    h = _row_rmsnorm(x, gmix_ref[...]).astype(BF16)
    proj = _dot(h, w0_ref[...])
    cosa, sina = cosa_ref[...], sina_ref[...]
    cosb, sinb = cosb_ref[...], sinb_ref[...]

    qn = _row_rmsnorm(proj[:, 0:256], qag_ref[...]).astype(BF16)
    q = _seg_rmsnorm(_dot(qn, wq_ref[...]), mq_ref[...], mqt_ref[...], ilq_ref[...], gq_ref[...])
    for hd in range(8):
        sl = slice(LANES * hd, LANES * (hd + 1))
        qa_ref[:, sl] = _rope128(q[:, sl], cosa, sina, first_half).astype(BF16)

    kvn = _row_rmsnorm(proj[:, 256:384], kvag_ref[...]).astype(BF16)
    kv = _dot(kvn, wkv_ref[...])
    ones_hi = jnp.where(lo, 0.0, 1.0)
    kn = _seg_rmsnorm(kv[:, 0:1024], mk_ref[...], mkt_ref[...], ilk_ref[...], gk_ref[...])
    kr = _seg_rmsnorm(proj[:, 384:512], mr_ref[...], mrt_ref[...], ilr_ref[...], gr_ref[...])
    kr = _rope128(kr, cosa, sina, first_half)
    for hd in range(8):
        sl = slice(LANES * hd, LANES * (hd + 1))
        ka_ref[:, sl] = (kn[:, sl] + kr).astype(BF16)
        va_ref[:, sl] = (kv[:, 1024 + LANES * hd:1024 + LANES * (hd + 1)] + ones_hi).astype(BF16)

    bq = _seg_rmsnorm(proj[:, 512:1024], mbq_ref[...], mbqt_ref[...], ilbq_ref[...], gbq_ref[...])
    for blk in range(4):
        sl = slice(LANES * blk, LANES * (blk + 1))
        qb_ref[:, sl] = _rope128(bq[:, sl], cosb, sinb, first_half).astype(BF16)
    bk = _seg_rmsnorm(proj[:, 1024:1152], mbk_ref[...], mbkt_ref[...], ilbk_ref[...], gbk_ref[...])
    bk = _rope128(bk, cosb, sinb, first_half)
    k0, k1 = _dup_halves(bk, lo)
    kb_ref[:, 0:LANES] = k0.astype(BF16)
    kb_ref[:, LANES:2 * LANES] = k1.astype(BF16)
    bv = proj[:, 1152:1280]
    vb_ref[:, 0:LANES] = jnp.where(lo, bv, 1.0).astype(BF16)
    vb_ref[:, LANES:2 * LANES] = jnp.where(lo, pltpu.roll(bv, HEAD, 1), 1.0).astype(BF16)


def _prep_c_rows(x, gmix_ref, w_ref,
                   mq_ref, mqt_ref, ilq_ref, gq_ref,
                   mk_ref, mkt_ref, ilk_ref, gk_ref,
                   qc_ref, kc_ref, vc_ref):
    lo, _ = _lane_masks()
    h = _row_rmsnorm(x, gmix_ref[...]).astype(BF16)
    proj = _dot(h, w_ref[...])
    q = _seg_rmsnorm(proj[:, 0:1024], mq_ref[...], mqt_ref[...], ilq_ref[...], gq_ref[...])
    qc_ref[...] = q.astype(BF16)
    k = _seg_rmsnorm(proj[:, 1024:1280], mk_ref[...], mkt_ref[...], ilk_ref[...], gk_ref[...])
    for blk in range(2):
        sl = slice(LANES * blk, LANES * (blk + 1))
        k0, k1 = _dup_halves(k[:, sl], lo)
        kc_ref[:, 2 * blk * LANES:(2 * blk + 1) * LANES] = k0.astype(BF16)
        kc_ref[:, (2 * blk + 1) * LANES:(2 * blk + 2) * LANES] = k1.astype(BF16)
        v = proj[:, 1280 + LANES * blk:1280 + LANES * (blk + 1)]
        vc_ref[:, 2 * blk * LANES:(2 * blk + 1) * LANES] = jnp.where(lo, v, 1.0).astype(BF16)
        vc_ref[:, (2 * blk + 1) * LANES:(2 * blk + 2) * LANES] = jnp.where(
            lo, pltpu.roll(v, HEAD, 1), 1.0).astype(BF16)


def _softmax_pv(s, v):
    m = jnp.max(s, axis=-1, keepdims=True)
    p = jnp.exp(s - m)
    l = jnp.sum(p, axis=-1, keepdims=True)
    return _dot(p.astype(BF16), v) / l


def _lane_chunk_reduce(op, t):
    out = t[:, 0:LANES]
    for j in range(1, t.shape[1] // LANES):
        out = op(out, t[:, LANES * j:LANES * (j + 1)])
    return out


def _scores_phase(q, k_ref, lanes, s_buf):
    m_part = None
    for c in range(k_ref.shape[1] // KEY_CHUNK):
        ks = slice(KEY_CHUNK * c, KEY_CHUNK * (c + 1))
        s_c = _dot_nt(q, k_ref[0, ks, lanes])
        s_buf[:, ks] = s_c
        mc = _lane_chunk_reduce(jnp.maximum, s_c)
        m_part = mc if m_part is None else jnp.maximum(m_part, mc)
    return jnp.max(m_part, axis=-1, keepdims=True)


def _pv_phase(s_buf, m, v_ref, lanes):
    lo, _ = _lane_masks()
    acc = None
    for c in range(v_ref.shape[1] // KEY_CHUNK):
        ks = slice(KEY_CHUNK * c, KEY_CHUNK * (c + 1))
        p = jnp.exp2(s_buf[:, ks] - m)
        pv = _dot(p.astype(BF16), v_ref[0, ks, lanes])
        acc = pv if acc is None else acc + pv
    return acc / jnp.where(lo, pltpu.roll(acc, HEAD, 1), 1.0)


def _pair_heads(o_even, o_odd):
    lo, _ = _lane_masks()
    return jnp.where(lo, o_even, pltpu.roll(o_odd, HEAD, 1))


def _mla_attn_kernel(q_ref, k_ref, v_ref, o_ref, s0_ref, s1_ref):
    first, second = slice(0, LANES), slice(LANES, 2 * LANES)
    m0 = _scores_phase(q_ref[0, :, first], k_ref, first, s0_ref)
    m1 = _scores_phase(q_ref[0, :, second], k_ref, second, s1_ref)
    o0 = _pv_phase(s0_ref, m0, v_ref, first)
    o1 = _pv_phase(s1_ref, m1, v_ref, second)
    o_ref[0] = _pair_heads(o0, o1).astype(o_ref.dtype)


def _gqa_attn_kernel(q_ref, k_ref, v_ref, o_ref, s0_ref, s1_ref):
    lo, _ = _lane_masks()
    tq = q_ref.shape[1]
    hi = jnp.logical_not(lo)
    ms = []
    for j, s_ref in enumerate((s0_ref, s1_ref)):
        blk = q_ref[0, :, LANES * j:LANES * (j + 1)]
        zero = jnp.zeros_like(blk)
        qs = jnp.concatenate([jnp.where(lo, blk, zero), jnp.where(hi, blk, zero)], axis=0)
        ms.append(_scores_phase(qs, k_ref, slice(0, LANES), s_ref))
    for j, s_ref in enumerate((s0_ref, s1_ref)):
        o = _pv_phase(s_ref, ms[j], v_ref, slice(0, LANES))
        o_ref[0, :, LANES * j:LANES * (j + 1)] = _pair_heads(o[0:tq], o[tq:2 * tq]).astype(o_ref.dtype)


def _win_attn_kernel(sink_ref, q_ref, kp_ref, kc_ref, kn_ref, vp_ref, vc_ref, vn_ref, bias_ref, o_ref,
                     s_scr, p_scr, m_scr, *, seq_len):
    lo, _ = _lane_masks()
    tq = Q_BLOCK
    nq = q_ref.shape[1] // tq
    i = pl.program_id(1)
    kj = lax.broadcasted_iota(jnp.int32, (1, 3 * tq), 1)
    units = [(g, u) for g in range(4) for u in range(nq)]
    for idx, (g, u) in enumerate(units):
        sl = slice(LANES * g, LANES * (g + 1))
        kcat = jnp.concatenate([kp_ref[0, :, sl], kc_ref[0, :, sl], kn_ref[0, :, sl]], axis=0)
        parts = []
        for a in range(4):
            blk = q_ref[0, tq * u:tq * (u + 1), LANES * (2 * g + a // 2):LANES * (2 * g + a // 2 + 1)]
            keep = lo if a % 2 == 0 else jnp.logical_not(lo)
            parts.append(jnp.where(keep, blk, jnp.zeros_like(blk)))
        s_scr[idx] = _dot_nt(jnp.concatenate(parts, axis=0), kcat[tq * u:tq * (u + 3)])
    for idx, (g, u) in enumerate(units):
        key_pos = (i * nq + u - 1) * tq + kj
        valid = jnp.logical_and(key_pos >= 0, key_pos < seq_len)
        for rt in range(4 * tq // WIN_ROW_TILE):
            a, r0 = divmod(rt * WIN_ROW_TILE, tq)
            rows = slice(rt * WIN_ROW_TILE, (rt + 1) * WIN_ROW_TILE)
            s = s_scr[idx, rows, :] + bias_ref[4 * g + a, r0:r0 + WIN_ROW_TILE, :]
            s = jnp.where(valid, s, -jnp.inf)
            m = jnp.maximum(jnp.max(_lane_chunk_reduce(jnp.maximum, s), axis=-1, keepdims=True),
                            sink_ref[4 * g + a] * LOG2E)
            p_scr[idx, rows, :] = jnp.exp2(s - m).astype(BF16)
            m_scr[idx, rows, :] = jnp.broadcast_to(m, (WIN_ROW_TILE, LANES))
    for idx, (g, u) in enumerate(units):
        sl = slice(LANES * g, LANES * (g + 1))
        vcat = jnp.concatenate([vp_ref[0, :, sl], vc_ref[0, :, sl], vn_ref[0, :, sl]], axis=0)
        acc = _dot(p_scr[idx], vcat[tq * u:tq * (u + 3)])
        outs = []
        for a in range(4):
            rows = slice(tq * a, tq * (a + 1))
            sink_term = jnp.exp2(sink_ref[4 * g + a] * LOG2E - m_scr[idx, rows, :])
            den = jnp.where(lo, pltpu.roll(acc[rows], HEAD, 1) + sink_term, 1.0)
            outs.append(acc[rows] / den)
        for j in range(2):
            blk = 2 * g + j
            o_ref[0, tq * u:tq * (u + 1), LANES * blk:LANES * (blk + 1)] = _pair_heads(
                outs[2 * j], outs[2 * j + 1]).astype(o_ref.dtype)


def _rel_bias_kernel(bucket_ref, band_ref, relt_ref, o_ref):
    bucket = bucket_ref[...]
    acc = jnp.zeros(o_ref.shape, F32)
    for r in range(REL_BUCKETS):
        acc = acc + jnp.where(bucket == r, relt_ref[:, r:r + 1], 0.0)
    o_ref[...] = jnp.where(band_ref[...] > 0, acc * LOG2E, -jnp.inf)


def _to_tiles(ref, val, first_token=0):
    for c in range(TOKEN_TILE_ROWS):
        ref[pl.ds(first_token * TOKEN_TILE_ROWS + c, val.shape[0], stride=TOKEN_TILE_ROWS), :] = (
            val[:, LANES * c:LANES * (c + 1)])


def _from_tiles(ref):
    rows = ref.shape[0] // TOKEN_TILE_ROWS
    return jnp.concatenate([ref[pl.ds(c, rows, stride=TOKEN_TILE_ROWS), :] for c in range(TOKEN_TILE_ROWS)],
                           axis=1)


def _route_bucket(xn, wh, wl, rb):
    xh, xl = _split_bf16(xn)
    logit = _dot_nt(wh, xh) + _dot_nt(wh, xl) + _dot_nt(wl, xh) + rb
    g = [logit[r:r + 1, :] for r in range(N_GROUPS)]
    gmax = jnp.maximum(jnp.maximum(g[0], g[1]), jnp.maximum(g[2], g[3]))
    gidx = jnp.where(g[0] == gmax, 0, jnp.where(g[1] == gmax, 1, jnp.where(g[2] == gmax, 2, 3)))
    e = []
    for j in range(EXPERTS_PER_GROUP):
        rows = [logit[N_GROUPS + EXPERTS_PER_GROUP * gg + j:N_GROUPS + EXPERTS_PER_GROUP * gg + j + 1, :]
                for gg in range(N_GROUPS)]
        e.append(jnp.where(gidx == 0, rows[0], jnp.where(gidx == 1, rows[1],
                                                         jnp.where(gidx == 2, rows[2], rows[3]))))
    emax = jnp.maximum(jnp.maximum(e[0], e[1]), jnp.maximum(e[2], e[3]))
    ex = [jnp.exp(ej - emax) for ej in e]
    esum = ex[0] + ex[1] + ex[2] + ex[3]
    pr = [exj / esum for exj in ex]
    p1 = jnp.maximum(jnp.maximum(pr[0], pr[1]), jnp.maximum(pr[2], pr[3]))
    i1 = jnp.where(pr[0] == p1, 0, jnp.where(pr[1] == p1, 1, jnp.where(pr[2] == p1, 2, 3)))
    rest = [jnp.where(i1 == j, -1.0, pr[j]) for j in range(EXPERTS_PER_GROUP)]
    p2 = jnp.maximum(jnp.maximum(rest[0], rest[1]), jnp.maximum(rest[2], rest[3]))
    i2 = jnp.where(rest[0] == p2, 0, jnp.where(rest[1] == p2, 1, jnp.where(rest[2] == p2, 2, 3)))
    lo_e = jnp.minimum(i1, i2)
    hi_e = jnp.maximum(i1, i2)
    pair = jnp.where(lo_e == 0, hi_e - 1, jnp.where(lo_e == 1, hi_e + 1, 5))
    return gidx * PAIRS_PER_GROUP + pair


def _post_kernel(*refs, n_parts):
    parts = refs[:n_parts]
    ws = refs[n_parts:2 * n_parts]
    (x_ref, gffn_ref, wrh_ref, wrl_ref, rb_ref, tri_ref,
     x1t_ref, bucket_ref, rank_ref, counts_ref, carry_ref) = refs[2 * n_parts:]

    @pl.when(pl.program_id(0) == 0)
    def _():
        carry_ref[...] = jnp.zeros_like(carry_ref)

    sub = tri_ref.shape[0]
    buckets = []
    for h in range(x_ref.shape[0] // sub):
        rows = slice(sub * h, sub * (h + 1))
        acc = x_ref[rows, :]
        for p_ref, w_ref in zip(parts, ws):
            acc = acc + _dot(p_ref[rows, :], w_ref[...])
        _to_tiles(x1t_ref, acc, sub * h)
        bucket = _route_bucket(_row_rmsnorm(acc, gffn_ref[...]), wrh_ref[...], wrl_ref[...], rb_ref[...])
        bucket_ref[:, rows] = bucket
        buckets.append(bucket)
    for h, bucket in enumerate(buckets):
        rows = slice(sub * h, sub * (h + 1))
        onehot = (lax.broadcasted_iota(jnp.int32, (ROUTER_ROWS, sub), 0) == bucket).astype(F32)
        before = _dot(onehot.astype(BF16), tri_ref[...]) + carry_ref[:, 0:1]
        rank_ref[:, rows] = jnp.sum(onehot * before, axis=0, keepdims=True).astype(jnp.int32)
        carry_ref[...] = carry_ref[...] + jnp.sum(onehot, axis=1, keepdims=True)
    counts_ref[...] = carry_ref[...]


def _start_row_gather(idx_of, table_hbm, dst, sem, both_priorities=False):
    def body(blk, carry):
        for j in range(GATHER_UNROLL):
            r = blk * GATHER_UNROLL + j
            src_row = pl.multiple_of(idx_of(r) * TOKEN_TILE_ROWS, TOKEN_TILE_ROWS)
            dst_row = pl.multiple_of(r * TOKEN_TILE_ROWS, TOKEN_TILE_ROWS)
            pltpu.make_async_copy(table_hbm.at[pl.ds(src_row, TOKEN_TILE_ROWS)],
                                  dst.at[pl.ds(dst_row, TOKEN_TILE_ROWS)], sem).start(
                                      priority=j % 2 if both_priorities else 0)
        return carry
    lax.fori_loop(0, dst.shape[0] // (TOKEN_TILE_ROWS * GATHER_UNROLL), body, 0)


def _wait_row_gather(table_hbm, dst, sem):
    pltpu.make_async_copy(table_hbm.at[pl.ds(0, dst.shape[0])], dst, sem).wait()


def _tile_gates(logit, ea, eb):
    lane = lax.broadcasted_iota(jnp.int32, (1, LANES), 1)
    pick = lambda idx: jnp.sum(jnp.where(lane == idx, logit, 0.0), axis=-1, keepdims=True)
    glog = jnp.where(lane < N_GROUPS, logit, -jnp.inf)
    gmax = jnp.max(glog, axis=-1, keepdims=True)
    gsum = jnp.sum(jnp.exp(glog - gmax), axis=-1, keepdims=True)
    g_p = jnp.exp(pick(ea // EXPERTS_PER_GROUP) - gmax) / gsum
    la, lb = pick(N_GROUPS + ea), pick(N_GROUPS + eb)
    top = jnp.maximum(la, lb)
    pa, pb = jnp.exp(la - top), jnp.exp(lb - top)
    return g_p * pa / (pa + pb), g_p * pb / (pa + pb)


def _moe_sparse_kernel(ea_ref, eb_ref, nt_ref, first_ref, order_ref, gffn_ref, wr_ref, rb_ref, x1t_hbm,
                       wga_ref, wua_ref, wda_ref, wgb_ref, wub_ref, wdb_ref, o_ref, buf, sem):
    t = pl.program_id(0)
    nt = nt_ref[0]
    slot = t % 2

    def tokens_of(tile):
        base = first_ref[tile]
        return lambda r: order_ref[base + r]

    @pl.when(t == 0)
    def _():
        _start_row_gather(tokens_of(0), x1t_hbm, buf.at[0], sem.at[0])

    @pl.when(t + 1 < nt)
    def _():
        _start_row_gather(tokens_of(t + 1), x1t_hbm, buf.at[1 - slot], sem.at[1 - slot])

    @pl.when(t < nt)
    def _():
        _wait_row_gather(x1t_hbm, buf.at[slot], sem.at[slot])
        x = _from_tiles(buf.at[slot])
        xn = _row_rmsnorm(x, gffn_ref[...])
        gate_a, gate_b = _tile_gates(_dot(xn, wr_ref[...]) + rb_ref[...], ea_ref[t], eb_ref[t])
        hid_a = jax.nn.silu(_dot(xn, wga_ref[0, 0])) * _dot(xn, wua_ref[0, 0])
        hid_b = jax.nn.silu(_dot(xn, wgb_ref[0, 0])) * _dot(xn, wub_ref[0, 0])
        out = x + gate_a * _dot(hid_a, wda_ref[0, 0]) + gate_b * _dot(hid_b, wdb_ref[0, 0])
        _to_tiles(o_ref, out)

    @pl.when(t >= nt)
    def _():
        o_ref[...] = jnp.zeros_like(o_ref)


def _unpermute_kernel(pos_cur_ref, pos_nxt_ref, sorted_hbm, o_ref, buf, sem):
    t = pl.program_id(0)
    slot = t % 2

    @pl.when(t == 0)
    def _():
        _start_row_gather(lambda r: pos_cur_ref[0, 0, r], sorted_hbm, buf.at[0], sem.at[0], both_priorities=True)

    @pl.when(t + 1 < pl.num_programs(0))
    def _():
        _start_row_gather(lambda r: pos_nxt_ref[0, 0, r], sorted_hbm, buf.at[1 - slot], sem.at[1 - slot],
                          both_priorities=True)

    _wait_row_gather(sorted_hbm, buf.at[slot], sem.at[slot])
    o_ref[...] = _from_tiles(buf.at[slot])


def _seg_mats(width, segments):
    m = np.zeros((width, LANES), np.float32)
    invlen = np.ones((1, LANES), np.float32)
    for c, (start, length) in enumerate(segments):
        m[start:start + length, c] = 1.0
        invlen[0, c] = 1.0 / length
    return jnp.asarray(m, BF16), jnp.asarray(np.concatenate([m.T, m.T], axis=0), BF16), jnp.asarray(invlen)


def _rope_tables(pos_list, seq_len):
    inv = np.float32(ROPE_THETA) ** (-np.arange(0, 2 * ROPE_HALF, 2, dtype=np.float32) / np.float32(2 * ROPE_HALF))
    cos_cols, sin_cols = [], []
    for pos in pos_list:
        if pos is None:
            cos_cols.append(np.ones((seq_len, 2 * ROPE_HALF), np.float32))
            sin_cols.append(np.zeros((seq_len, 2 * ROPE_HALF), np.float32))
        else:
            ang = pos.astype(np.float32)[:, None] * inv[None, :]
            c, s = np.cos(ang), np.sin(ang)
            cos_cols.append(np.concatenate([c, c], axis=1))
            sin_cols.append(np.concatenate([-s, s], axis=1))
    return jnp.asarray(np.concatenate(cos_cols, axis=1)), jnp.asarray(np.concatenate(sin_cols, axis=1))


def _t5_bucket_np(rel):
    nb = REL_BUCKETS // 2
    max_exact = nb // 2
    ret = np.where(rel > 0, nb, 0)
    n = np.abs(rel)
    nf = np.maximum(n, 1).astype(np.float32)
    large = max_exact + (np.log(nf / np.float32(max_exact)) / np.float32(math.log(REL_MAX_DIST / max_exact))
                         * np.float32(nb - max_exact)).astype(np.int32)
    large = np.minimum(large, nb - 1)
    return ret + np.where(n < max_exact, n, large)


def _full(shape):
    nd = len(shape)
    return pl.BlockSpec(shape, lambda *_: (0,) * nd)


def _params(*sem):
    return pltpu.CompilerParams(dimension_semantics=sem, vmem_limit_bytes=VMEM_LIMIT)


def _prep_ab(x2, seq_len, gmix, w_in, qag, wq, kvag, wkv, qn_g, kn_g, qr_g, kr_g, bq_g, bk_g, tm=512):
    n = x2.shape[0]
    scale_a = (HEAD + 2 * ROPE_HALF) ** -0.5
    scale_b = HEAD ** -0.5
    zeros = lambda r, c: jnp.zeros((r, c), F32)
    d = w_in.shape[0]
    w0 = jnp.concatenate([w_in[:, 0:384], zeros(d, HEAD), w_in[:, 384:416], zeros(d, 32), w_in[:, 416:1184]],
                         axis=1).astype(BF16)
    wq_p = jnp.concatenate([wq.reshape(-1, 8, 96), jnp.zeros((wq.shape[0], 8, 32), F32)], axis=2)
    wq_p = wq_p.reshape(-1, 1024).astype(BF16)
    wkv_r = wkv.reshape(-1, 8, 128)
    wk_p = jnp.concatenate([wkv_r[:, :, :HEAD], jnp.zeros_like(wkv_r[:, :, :HEAD])], axis=2).reshape(-1, 1024)
    wv_p = jnp.concatenate([wkv_r[:, :, HEAD:], jnp.zeros_like(wkv_r[:, :, HEAD:])], axis=2).reshape(-1, 1024)
    wkv_p = jnp.concatenate([wk_p, wv_p], axis=1).astype(BF16)

    seg_q = [(LANES * h, HEAD) for h in range(8)] + [(LANES * h + HEAD, 32) for h in range(8)]
    gq = jnp.tile(jnp.concatenate([qn_g, qr_g, jnp.zeros((32,), F32)]), 8)[None, :] * (scale_a * LOG2E)
    seg_k = [(LANES * h, HEAD) for h in range(8)]
    gk = jnp.tile(jnp.concatenate([kn_g, jnp.zeros((HEAD,), F32)]), 8)[None, :]
    seg_r = [(HEAD, 32)]
    gr = jnp.concatenate([jnp.zeros((HEAD,), F32), kr_g, jnp.zeros((32,), F32)])[None, :]
    seg_bq = [(HEAD * h, HEAD) for h in range(8)]
    gbq = jnp.tile(bq_g, 8)[None, :] * (scale_b * LOG2E)
    seg_bk = [(HEAD * h, HEAD) for h in range(2)]
    gbk = jnp.tile(bk_g, 2)[None, :]

    pos = np.arange(seq_len)
    cosa, sina = _rope_tables([None, None, pos, None], seq_len)
    row, col = pos // GRID_W, pos % GRID_W
    cosb, sinb = _rope_tables([row, col, row, col], seq_len)

    consts = [gmix[None, :], w0, qag[None, :], wq_p, kvag[None, :], wkv_p,
              *_seg_mats(1024, seg_q), gq, *_seg_mats(1024, seg_k), gk, *_seg_mats(LANES, seg_r), gr,
              *_seg_mats(512, seg_bq), gbq, *_seg_mats(LANES, seg_bk), gbk]
    nsb = seq_len // tm
    rope_spec = pl.BlockSpec((tm, LANES), lambda i: (i % nsb, 0))
    row_spec = lambda w: pl.BlockSpec((tm, w), lambda i: (i, 0))
    out_widths = (1024, 1024, 1024, 512, 256, 256)
    return pl.pallas_call(
        _prep_ab_kernel,
        grid=(n // tm,),
        in_specs=[row_spec(x2.shape[1])] + [_full(c.shape) for c in consts] + [rope_spec] * 4,
        out_specs=[row_spec(w) for w in out_widths],
        out_shape=[jax.ShapeDtypeStruct((n, w), BF16) for w in out_widths],
        compiler_params=_params("parallel"),
        name="prep_ab",
    )(x2, *consts, cosa, sina, cosb, sinb)


def _prep_c(sorted_rows, pos, gmix, w_in, q_g, k_g, tm=512):
    n = pos.shape[0]
    d = TOKEN_TILE_ROWS * LANES
    seg_q = [(HEAD * h, HEAD) for h in range(16)]
    gq = jnp.tile(q_g, 16)[None, :] * (HEAD ** -0.5 * LOG2E)
    seg_k = [(HEAD * h, HEAD) for h in range(4)]
    gk = jnp.tile(k_g, 4)[None, :]
    consts = [gmix[None, :], w_in.astype(BF16), *_seg_mats(1024, seg_q), gq, *_seg_mats(256, seg_k), gk]
    row_spec = lambda w: pl.BlockSpec((tm, w), lambda i: (i, 0))
    out_widths = (1024, 512, 512)
    pos2 = jnp.concatenate([pos, jnp.zeros((tm,), jnp.int32)]).reshape(n // tm + 1, 1, tm)
    idx_spec = lambda off: pl.BlockSpec((1, 1, tm), lambda i: (i + off, 0, 0), memory_space=pltpu.SMEM)
    return pl.pallas_call(
        _prep_c_kernel,
        grid=(n // tm,),
        in_specs=[idx_spec(0), idx_spec(1)] + [_full(c.shape) for c in consts] + [pl.BlockSpec(memory_space=pl.ANY)],
        out_specs=[row_spec(d)] + [row_spec(w) for w in out_widths],
        out_shape=[jax.ShapeDtypeStruct((n, d), F32)] + [jax.ShapeDtypeStruct((n, w), BF16) for w in out_widths],
        scratch_shapes=[pltpu.VMEM((2, tm * TOKEN_TILE_ROWS, LANES), F32), pltpu.SemaphoreType.DMA((2,))],
        compiler_params=_params("arbitrary"),
        name="prep_c",
    )(pos2, pos2, *consts, sorted_rows)


def _mla_attn(qa, ka, va, tq=1024):
    b, s, _ = qa.shape
    return pl.pallas_call(
        _mla_attn_kernel,
        grid=(b, 4, s // tq),
        in_specs=[pl.BlockSpec((1, tq, 2 * LANES), lambda bi, hp, qi: (bi, qi, hp)),
                  pl.BlockSpec((1, s, 2 * LANES), lambda bi, hp, qi: (bi, 0, hp)),
                  pl.BlockSpec((1, s, 2 * LANES), lambda bi, hp, qi: (bi, 0, hp))],
        out_specs=pl.BlockSpec((1, tq, LANES), lambda bi, hp, qi: (bi, qi, hp)),
        out_shape=jax.ShapeDtypeStruct((b, s, 512), BF16),
        scratch_shapes=[pltpu.VMEM((tq, s), F32), pltpu.VMEM((tq, s), F32)],
        compiler_params=_params("parallel", "parallel", "parallel"),
        name="mla_attn",
    )(qa, ka, va)


def _gqa_attn(qb, kb, vb, tq=512):
    b, s, _ = qb.shape
    return pl.pallas_call(
        _gqa_attn_kernel,
        grid=(b, 2, s // tq),
        in_specs=[pl.BlockSpec((1, tq, 2 * LANES), lambda bi, g, qi: (bi, qi, g)),
                  pl.BlockSpec((1, s, LANES), lambda bi, g, qi: (bi, 0, g)),
                  pl.BlockSpec((1, s, LANES), lambda bi, g, qi: (bi, 0, g))],
        out_specs=pl.BlockSpec((1, tq, 2 * LANES), lambda bi, g, qi: (bi, qi, g)),
        out_shape=jax.ShapeDtypeStruct((b, s, 512), BF16),
        scratch_shapes=[pltpu.VMEM((2 * tq, s), F32), pltpu.VMEM((2 * tq, s), F32)],
        compiler_params=_params("parallel", "parallel", "parallel"),
        name="gqa_attn",
    )(qb, kb, vb)


def _rel_bias_table(rel_bias):
    span = Q_BLOCK + 2 * WINDOW
    rel = np.arange(span)[None, :] - WINDOW - np.arange(Q_BLOCK)[:, None]
    bucket = _t5_bucket_np(rel).astype(np.int32).reshape(1, -1)
    band = (np.abs(rel) <= WINDOW).astype(np.int32).reshape(1, -1)
    heads = rel_bias.shape[1]
    cols = bucket.shape[1]
    chunk = cols // 8
    table = pl.pallas_call(
        _rel_bias_kernel,
        grid=(8,),
        in_specs=[pl.BlockSpec((1, chunk), lambda i: (0, i)), pl.BlockSpec((1, chunk), lambda i: (0, i)),
                  _full((heads, REL_BUCKETS))],
        out_specs=pl.BlockSpec((heads, chunk), lambda i: (0, i)),
        out_shape=jax.ShapeDtypeStruct((heads, cols), F32),
        compiler_params=_params("parallel"),
        name="rel_bias",
    )(jnp.asarray(bucket), jnp.asarray(band), rel_bias.T)
    return table.reshape(heads, Q_BLOCK, span)


def _win_attn(qc, kc, vc, sink, bias, nq=4):
    b, s, _ = qc.shape
    tq = Q_BLOCK
    nb = s // tq
    prev = lambda bi, i: (bi, jnp.maximum(i * nq - 1, 0), 0)
    cur = lambda bi, i: (bi, i, 0)
    nxt = lambda bi, i: (bi, jnp.minimum((i + 1) * nq, nb - 1), 0)
    edge_spec = lambda im: pl.BlockSpec((1, tq, 4 * LANES), im)
    cur_spec = pl.BlockSpec((1, nq * tq, 4 * LANES), cur)
    return pl.pallas_call(
        functools.partial(_win_attn_kernel, seq_len=s),
        grid=(b, nb // nq),
        in_specs=[pl.BlockSpec(memory_space=pltpu.SMEM),
                  pl.BlockSpec((1, nq * tq, 8 * LANES), cur),
                  edge_spec(prev), cur_spec, edge_spec(nxt),
                  edge_spec(prev), cur_spec, edge_spec(nxt),
                  _full(bias.shape)],
        out_specs=pl.BlockSpec((1, nq * tq, 8 * LANES), cur),
        scratch_shapes=[pltpu.VMEM((4 * nq, 4 * tq, 3 * tq), F32), pltpu.VMEM((4 * nq, 4 * tq, 3 * tq), BF16),
                        pltpu.VMEM((4 * nq, 4 * tq, LANES), F32)],
        out_shape=jax.ShapeDtypeStruct((b, s, 1024), BF16),
        compiler_params=_params("parallel", "parallel"),
        name="win_attn",
    )(sink, qc, kc, kc, kc, vc, vc, vc, bias)


def _post(parts, ws, x2, gffn, w_group, b_group, w_router, b_router, tm=1024, sub=256):
    n, d = x2.shape
    wr = jnp.concatenate([w_group.T, w_router.T, jnp.zeros((ROUTER_ROWS - N_GROUPS - N_EXPERTS, d), F32)], axis=0)
    wrh = wr.astype(BF16)
    wrl = (wr - wrh.astype(F32)).astype(BF16)
    rb = jnp.concatenate([b_group, b_router, jnp.zeros((ROUTER_ROWS - N_GROUPS - N_EXPERTS,), F32)])[:, None]
    row_spec = lambda w: pl.BlockSpec((tm, w), lambda i: (i, 0))
    lane_spec = lambda r: pl.BlockSpec((r, tm), lambda i: (0, i))
    ws = [w.astype(BF16) for w in ws]
    tri = jnp.asarray(np.triu(np.ones((sub, sub), np.float32), 1), BF16)
    return pl.pallas_call(
        functools.partial(_post_kernel, n_parts=len(parts)),
        grid=(n // tm,),
        in_specs=[row_spec(p.shape[1]) for p in parts] + [_full(w.shape) for w in ws]
                 + [row_spec(d), _full((1, d)), _full(wrh.shape), _full(wrl.shape), _full(rb.shape),
                    _full(tri.shape)],
        out_specs=[pl.BlockSpec((tm * TOKEN_TILE_ROWS, LANES), lambda i: (i, 0)),
                   lane_spec(1), lane_spec(1), _full((ROUTER_ROWS, LANES))],
        out_shape=[jax.ShapeDtypeStruct((n * TOKEN_TILE_ROWS, LANES), F32),
                   jax.ShapeDtypeStruct((1, n), jnp.int32), jax.ShapeDtypeStruct((1, n), jnp.int32),
                   jax.ShapeDtypeStruct((ROUTER_ROWS, LANES), F32)],
        scratch_shapes=[pltpu.VMEM((ROUTER_ROWS, LANES), F32)],
        compiler_params=_params("arbitrary"),
        name="post",
    )(*parts, *ws, x2, gffn[None, :], wrh, wrl, rb, tri)


def _route(bucket, rank, counts, tmm):
    n = bucket.shape[1]
    n_tiles_max = n // tmm + N_BUCKETS
    bucket, rank = bucket[0], rank[0]
    cnt = counts[:N_BUCKETS, 0].astype(jnp.int32)
    padded = (cnt + tmm - 1) // tmm * tmm
    ends = jnp.cumsum(padded)
    starts = ends - padded
    pos = starts[bucket] + rank
    n_tiles = ends[-1] // tmm
    tile = jnp.minimum(jnp.arange(n_tiles_max, dtype=jnp.int32), n_tiles - 1)
    tile_bucket = jnp.sum((tile * tmm)[:, None] >= ends[None, :], axis=1).astype(jnp.int32)
    grp, pair = tile_bucket // PAIRS_PER_GROUP, tile_bucket % PAIRS_PER_GROUP
    tile_ea = grp * EXPERTS_PER_GROUP + jnp.asarray(PAIR_LO, jnp.int32)[pair]
    tile_eb = grp * EXPERTS_PER_GROUP + jnp.asarray(PAIR_HI, jnp.int32)[pair]
    token = jnp.arange(n, dtype=jnp.int32)
    _, by_bucket = lax.sort((bucket, token), num_keys=1, is_stable=True)
    order = jnp.concatenate([by_bucket, token[:tmm]])
    first = jnp.cumsum(cnt) - cnt
    tile_first = first[tile_bucket] + tile * tmm - starts[tile_bucket]
    return pos, order, tile_first, tile_ea, tile_eb, n_tiles.reshape(1)


def _moe_sparse(x1t, gffn, w_group, b_group, w_router, b_router, order, tile_first, tile_ea, tile_eb, n_tiles,
                w_gate, w_up, w_down, layer, tmm):
    n_tiles_max = tile_ea.shape[0]
    _, ne, d, de = w_gate.shape
    up_spec = lambda which: pl.BlockSpec((1, 1, d, de), lambda t, ea, eb, *_: (layer, (ea, eb)[which][t], 0, 0))
    down_spec = lambda which: pl.BlockSpec((1, 1, de, d), lambda t, ea, eb, *_: (layer, (ea, eb)[which][t], 0, 0))
    wg, wu, wd = w_gate, w_up, w_down
    pad = LANES - N_GROUPS - N_EXPERTS
    wr = jnp.concatenate([w_group, w_router, jnp.zeros((d, pad), F32)], axis=1)
    rb = jnp.concatenate([b_group, b_router, jnp.zeros((pad,), F32)])[None, :]
    return pl.pallas_call(
        _moe_sparse_kernel,
        grid_spec=pltpu.PrefetchScalarGridSpec(
            num_scalar_prefetch=5,
            grid=(n_tiles_max,),
            in_specs=[pl.BlockSpec((1, d), lambda t, *_: (0, 0)),
                      pl.BlockSpec((d, LANES), lambda t, *_: (0, 0)),
                      pl.BlockSpec((1, LANES), lambda t, *_: (0, 0)),
                      pl.BlockSpec(memory_space=pl.ANY),
                      up_spec(0), up_spec(0), down_spec(0), up_spec(1), up_spec(1), down_spec(1)],
            out_specs=pl.BlockSpec((tmm * TOKEN_TILE_ROWS, LANES), lambda t, *_: (t, 0)),
            scratch_shapes=[pltpu.VMEM((2, tmm * TOKEN_TILE_ROWS, LANES), F32), pltpu.SemaphoreType.DMA((2,))]),
        out_shape=jax.ShapeDtypeStruct((n_tiles_max * tmm * TOKEN_TILE_ROWS, LANES), F32),
        compiler_params=_params("arbitrary"),
        name="moe_sparse",
    )(tile_ea, tile_eb, n_tiles, tile_first, order, gffn[None, :], wr, rb, x1t, wg, wu, wd, wg, wu, wd)


def _unpermute(sorted_rows, pos, tm=256):
    n = pos.shape[0]
    d = TOKEN_TILE_ROWS * LANES
    pos2 = jnp.concatenate([pos, jnp.zeros((tm,), jnp.int32)]).reshape(n // tm + 1, 1, tm)
    idx_spec = lambda off: pl.BlockSpec((1, 1, tm), lambda t: (t + off, 0, 0), memory_space=pltpu.SMEM)
    return pl.pallas_call(
        _unpermute_kernel,
        grid=(n // tm,),
        in_specs=[idx_spec(0), idx_spec(1), pl.BlockSpec(memory_space=pl.ANY)],
        out_specs=pl.BlockSpec((tm, d), lambda t: (t, 0)),
        out_shape=jax.ShapeDtypeStruct((n, d), F32),
        scratch_shapes=[pltpu.VMEM((2, tm * TOKEN_TILE_ROWS, LANES), F32), pltpu.SemaphoreType.DMA((2,))],
        compiler_params=_params("arbitrary"),
        name="unpermute",
    )(pos2, pos2, sorted_rows)


def kernel(x, mix_norm, ffn_norm, w_in_ab, mla_q_a_norm, mla_w_q_up, mla_kv_a_norm, mla_w_kv_up, mla_qn_gain, mla_kn_gain, mla_qr_gain, mla_kr_gain, gqa_q_gain, gqa_k_gain, w_out_ab, w_in_c, win_q_gain, win_k_gain, win_sink, w_out_c, rel_bias, moe_w_group, moe_b_group, moe_w_router, moe_b_router, moe_w_gate, moe_w_up, moe_w_down):
    b, s, d = x.shape
    n = b * s
    depth = mix_norm.shape[0]
    x2 = x.reshape(n, d)
    bias = None
    moe_out = None
    for layer in range(depth):
        i = layer // 2
        if layer % 2 == 0:
            if moe_out is not None:
                x2 = _unpermute(*moe_out)
            qa, ka, va, qb, kb, vb = _prep_ab(
                x2, s, mix_norm[layer], w_in_ab[i], mla_q_a_norm[i], mla_w_q_up[i], mla_kv_a_norm[i],
                mla_w_kv_up[i], mla_qn_gain[i], mla_kn_gain[i], mla_qr_gain[i], mla_kr_gain[i],
                gqa_q_gain[i], gqa_k_gain[i])
            r3 = lambda t: t.reshape(b, s, t.shape[1])
            out_a = _mla_attn(r3(qa), r3(ka), r3(va)).reshape(n, 512)
            out_b = _gqa_attn(r3(qb), r3(kb), r3(vb)).reshape(n, 512)
            parts, ws = [out_a, out_b], [w_out_ab[i][:512], w_out_ab[i][512:]]
        else:
            if bias is None:
                bias = _rel_bias_table(rel_bias)
            x2, qc, kc, vc = _prep_c(*moe_out, mix_norm[layer], w_in_c[i], win_q_gain[i], win_k_gain[i])
            r3 = lambda t: t.reshape(b, s, t.shape[1])
            out_c = _win_attn(r3(qc), r3(kc), r3(vc), win_sink[i], bias).reshape(n, 1024)
            parts, ws = [out_c], [w_out_c[i]]
        router = (moe_w_group[layer], moe_b_group[layer], moe_w_router[layer], moe_b_router[layer])
        x1t, bucket, rank, counts = _post(parts, ws, x2, ffn_norm[layer], *router)
        pos, order, tile_first, tile_ea, tile_eb, n_tiles = _route(bucket, rank, counts, MOE_TILE)
        y_sorted = _moe_sparse(x1t, ffn_norm[layer], *router, order, tile_first, tile_ea, tile_eb, n_tiles,
                               moe_w_gate, moe_w_up, moe_w_down, layer, MOE_TILE)
        moe_out = (y_sorted, pos)
    return _unpermute(*moe_out).reshape(b, s, d)
```
